```python
import math
import jax, jax.numpy as jnp
from jax import lax
import numpy as np

D_MODEL = 1024
BATCH = 4
SEQ = 8192
DEPTH = 2

CTX_LEN = 256
GRID_W = 64
D_MIX = D_MODEL
D_CONV = D_MIX // 2
D_RWKV = D_MIX - D_CONV
RWKV_HEAD = 64
RWKV_HEADS = D_RWKV // RWKV_HEAD
W_LORA = 64
A_LORA = 64
G_LORA = 128
N_DIR = 2
D_FF = ((8 * D_MODEL // 3 + 127) // 128) * 128
PROJ = 3 * D_CONV + 3 * D_RWKV + W_LORA + A_LORA + G_LORA
RMS_EPS = 1e-6
GN_EPS = 64e-5
DECAY_SCALE = math.exp(-0.5)

kernel_name = "hybrid_conv_rwkv7_convffn_dit"


def rms_norm(x, g):
    xf = x.astype(jnp.float32)
    y = xf * lax.rsqrt(jnp.mean(xf * xf, axis=-1, keepdims=True) + RMS_EPS)
    return (y * g.astype(jnp.float32)).astype(x.dtype)


def modulate(h, shift, scale):
    return h * (1 + scale) + shift


def split_proj(p):
    sizes = (D_CONV, D_CONV, D_CONV, D_RWKV, D_RWKV, D_RWKV, W_LORA, A_LORA, G_LORA)
    idx = [int(i) for i in np.cumsum(sizes)[:-1]]
    return jnp.split(p, idx, axis=-1)


def shortconv1d(u, w):
    up = jnp.pad(u, ((0, 0), (1, 1), (0, 0)))
    return up[:, :-2] * w[0] + up[:, 1:-1] * w[1] + up[:, 2:] * w[2]


def dwconv_grid(u, w, b, rows, cols):
    bsz, t, ch = u.shape
    img = u.reshape(bsz, rows, cols, ch)
    out = lax.conv_general_dilated(img, w[:, :, None, :].astype(u.dtype), (1, 1), "SAME",
                                   dimension_numbers=("NHWC", "HWIO", "NHWC"),
                                   feature_group_count=ch)
    return out.reshape(bsz, t, ch) + b


def to_heads(t):
    return t.astype(jnp.float32).reshape(t.shape[0], t.shape[1], RWKV_HEADS, RWKV_HEAD)


def wkv_scan(state0, r, decay, kk, b, v, k, reverse):
    def step(S, inp):
        r_t, w_t, kk_t, b_t, v_t, k_t = inp
        sa = jnp.einsum("bhvk,bhk->bhv", S, kk_t)
        S = (S * w_t[:, :, None, :] - sa[..., None] * b_t[:, :, None, :]
             + v_t[..., None] * k_t[:, :, None, :])
        return S, jnp.einsum("bhvk,bhk->bhv", S, r_t)
    xs = tuple(jnp.swapaxes(t, 0, 1) for t in (r, decay, kk, b, v, k))
    S, ys = lax.scan(step, state0, xs, reverse=reverse)
    return S, jnp.swapaxes(ys, 0, 1)


def rwkv_scans(r, k, v, w_lo, a_lo, lp, init):
    kk = to_heads(k * lp["k_k"])
    kk = kk * lax.rsqrt(jnp.maximum(jnp.sum(kk * kk, axis=-1, keepdims=True), 1e-24))
    rh, vh = to_heads(r), to_heads(v)
    y_sum = None
    finals = []
    for d in range(N_DIR):
        decay = jnp.exp(-DECAY_SCALE * jax.nn.sigmoid(lp["w0"][d] + jnp.tanh(w_lo) @ lp["w_up"][d]))
        a = jax.nn.sigmoid(lp["a0"][d] + a_lo @ lp["a_up"][d])
        k_d = k * (1 + (a - 1) * lp["k_a"])
        state, y = wkv_scan(init[d], rh, to_heads(decay), kk, kk * to_heads(a), vh, to_heads(k_d),
                            reverse=(d == 1))
        y_sum = y if y_sum is None else y_sum + y
        finals.append(state)
    return y_sum, finals


def rwkv_output(y, r, k, v, g_lo, lp):
    bsz, t = y.shape[:2]
    mu = jnp.mean(y, axis=-1, keepdims=True)
    var = jnp.mean(jnp.square(y - mu), axis=-1, keepdims=True)
    yn = ((y - mu) * lax.rsqrt(var + GN_EPS)).reshape(bsz, t, D_RWKV) * lp["ln_g"] + lp["ln_b"]
    bonus = jnp.sum(to_heads(r) * to_heads(k) * lp["r_k"].astype(jnp.float32), axis=-1, keepdims=True) * to_heads(v)
    gate = jax.nn.sigmoid(g_lo) @ lp["g_up"]
    return ((yn + bonus.reshape(bsz, t, D_RWKV)) * gate).astype(r.dtype)


def mixer_block(h, lp, init, need_output):
    cb, cc, cx, r, k, v, w_lo, a_lo, g_lo = split_proj(h @ lp["w_in"])
    y, finals = rwkv_scans(r, k, v, w_lo, a_lo, lp, init)
    if not need_output:
        return None, finals
    y_conv = cb * shortconv1d(cc * cx, lp["conv_a_w"])
    y_rwkv = rwkv_output(y, r, k, v, g_lo, lp)
    return jnp.concatenate([y_conv, y_rwkv], axis=-1) @ lp["w_out"], finals


def conv_ffn(h, lp, rows, cols):
    gate, val = jnp.split(h @ lp["ffn_w_up"], 2, axis=-1)
    gate = dwconv_grid(gate, lp["ffn_conv_w"], lp["ffn_conv_b"], rows, cols)
    return (jax.nn.silu(gate) * val) @ lp["ffn_w_down"]


def setup_inputs(seed: int = 0) -> dict:
    key = jax.random.key(seed)
    ks = jax.random.split(key, 32)
    L, D = DEPTH, D_MODEL

    def nrm(k, shape, s):
        return jax.random.normal(k, shape, jnp.float32) * s

    return {
        "x": nrm(ks[0], (BATCH, SEQ, D), 1.0),
        "c": nrm(ks[1], (BATCH, D), 1.0),
        "ctx": nrm(ks[2], (BATCH, CTX_LEN, D), 1.0),
        "c_ctx": nrm(ks[3], (D,), 1.0),
        "ada_w": nrm(ks[4], (L, D, 6 * D), 0.5 * D ** -0.5),
        "ada_b": nrm(ks[5], (L, 6 * D), 0.01),
        "norm1_g": 1.0 + nrm(ks[6], (L, D), 0.1),
        "norm2_g": 1.0 + nrm(ks[7], (L, D), 0.1),
        "w_in": nrm(ks[8], (L, D, PROJ), D ** -0.5),
        "conv_a_w": nrm(ks[9], (L, 3, D_CONV), 3 ** -0.5),
        "rw_w0": nrm(ks[10], (L, N_DIR, D_RWKV), 0.5),
        "rw_w_up": nrm(ks[11], (L, N_DIR, W_LORA, D_RWKV), 0.5 * W_LORA ** -0.5),
        "rw_a0": nrm(ks[12], (L, N_DIR, D_RWKV), 0.5),
        "rw_a_up": nrm(ks[13], (L, N_DIR, A_LORA, D_RWKV), 0.5 * A_LORA ** -0.5),
        "rw_k_k": 0.85 + nrm(ks[14], (L, D_RWKV), 0.05),
        "rw_k_a": 1.0 + nrm(ks[15], (L, D_RWKV), 0.05),
        "rw_r_k": nrm(ks[16], (L, RWKV_HEADS, RWKV_HEAD), 0.1),
        "rw_g_up": nrm(ks[17], (L, G_LORA, D_RWKV), G_LORA ** -0.5),
        "rw_ln_g": 1.0 + nrm(ks[18], (L, D_RWKV), 0.1),
        "rw_ln_b": nrm(ks[19], (L, D_RWKV), 0.01),
        "w_out": nrm(ks[20], (L, D_MIX, D), D_MIX ** -0.5),
        "ffn_w_up": nrm(ks[21], (L, D, 2 * D_FF), D ** -0.5),
        "ffn_conv_w": nrm(ks[22], (L, 3, 3, D_FF), 1.0 / 3.0),
        "ffn_conv_b": nrm(ks[23], (L, D_FF), 0.01),
        "ffn_w_down": nrm(ks[24], (L, D_FF, D), D_FF ** -0.5),
        "final_g": 1.0 + nrm(ks[25], (D,), 0.1),
    }


def reference(x, c, ctx, c_ctx, ada_w, ada_b, norm1_g, norm2_g, w_in, conv_a_w,
              rw_w0, rw_w_up, rw_a0, rw_a_up, rw_k_k, rw_k_a, rw_r_k, rw_g_up, rw_ln_g, rw_ln_b,
              w_out, ffn_w_up, ffn_conv_w, ffn_conv_b, ffn_w_down, final_g):
    bsz = x.shape[0]
    rows = x.shape[1] // GRID_W
    zero_state = jnp.zeros((bsz, RWKV_HEADS, RWKV_HEAD, RWKV_HEAD), jnp.float32)
    for l in range(DEPTH):
        last = l == DEPTH - 1
        lp = {
            "w_in": w_in[l], "conv_a_w": conv_a_w[l], "w0": rw_w0[l], "w_up": rw_w_up[l],
            "a0": rw_a0[l], "a_up": rw_a_up[l], "k_k": rw_k_k[l], "k_a": rw_k_a[l],
            "r_k": rw_r_k[l], "g_up": rw_g_up[l], "ln_g": rw_ln_g[l], "ln_b": rw_ln_b[l],
            "w_out": w_out[l], "ffn_w_up": ffn_w_up[l], "ffn_conv_w": ffn_conv_w[l],
            "ffn_conv_b": ffn_conv_b[l], "ffn_w_down": ffn_w_down[l],
        }
        mod = jax.nn.silu(c) @ ada_w[l] + ada_b[l]
        sh1, sc1, g1, sh2, sc2, g2 = [m[:, None, :] for m in jnp.split(mod, 6, axis=-1)]
        csh1, csc1, cg1, csh2, csc2, cg2 = jnp.split(jax.nn.silu(c_ctx) @ ada_w[l] + ada_b[l], 6, axis=-1)

        hc = modulate(rms_norm(ctx, norm1_g[l]), csh1, csc1)
        out_c, ctx_states = mixer_block(hc, lp, (zero_state, zero_state), not last)

        hx = modulate(rms_norm(x, norm1_g[l]), sh1, sc1)
        out_x, _ = mixer_block(hx, lp, ctx_states, True)
        x = x + g1 * out_x
        hx = modulate(rms_norm(x, norm2_g[l]), sh2, sc2)
        x = x + g2 * conv_ffn(hx, lp, rows, GRID_W)

        if not last:
            ctx = ctx + cg1 * out_c
            hc = modulate(rms_norm(ctx, norm2_g[l]), csh2, csc2)
            ctx = ctx + cg2 * conv_ffn(hc, lp, 1, ctx.shape[1])
    return rms_norm(x, final_g)
```

```python
import functools
import math

import jax
import jax.numpy as jnp
from jax import lax
from jax.experimental import pallas as pl
from jax.experimental.pallas import tpu as pltpu

F32 = jnp.float32
BF16 = jnp.bfloat16

HEAD = 64
D_CONV = 512
D_RWKV = 512
LORA_WA = 128
G_LORA = 128
GRID_W = 64
RMS_EPS = 1e-6
GN_EPS = 64e-5
DECAY_SCALE = math.exp(-0.5)

TM = 256
CHUNK = 64
QUAD = 4 * HEAD
VMEM_LIMIT = 56 * 1024 * 1024

NN = (((1,), (0,)), ((), ()))
NT = (((1,), (1,)), ((), ()))
TN = (((0,), (0,)), ((), ()))


def _silu(x):
    return x * jax.nn.sigmoid(x)


def _split(x):
    hi = x.astype(BF16)
    return hi, (x - hi.astype(F32)).astype(BF16)


def _dg(a, b, dims):
    return lax.dot_general(a, b, dims, preferred_element_type=F32)


def _mm3(a, b_hi, b_lo, dims):
    a_hi, a_lo = _split(a)
    return _dg(a_hi, b_hi, dims) + (_dg(a_hi, b_lo, dims) + _dg(a_lo, b_hi, dims))


def _rms(x):
    return x * lax.rsqrt(jnp.mean(x * x, axis=-1, keepdims=True) + RMS_EPS)


def _segsum(t, ones_bd):
    hi, lo = _split(t)
    return (jnp.dot(hi, ones_bd, preferred_element_type=F32)
            + jnp.dot(lo, ones_bd, preferred_element_type=F32))


def _ada_kernel(c_ref, w_ref, b_ref, o_ref):
    s = _silu(c_ref[...])
    w_hi, w_lo = _split(w_ref[0])
    o_ref[0] = _mm3(s, w_hi, w_lo, NN) + b_ref[0]


def _ada(c_rows, ada_w, ada_b):
    L, D, N = ada_w.shape
    R = c_rows.shape[0]
    NB = 1536
    return pl.pallas_call(
        _ada_kernel,
        grid=(L, N // NB),
        in_specs=[
            pl.BlockSpec((R, D), lambda l, n: (0, 0)),
            pl.BlockSpec((1, D, NB), lambda l, n: (l, 0, n)),
            pl.BlockSpec((1, 1, NB), lambda l, n: (l, 0, n)),
        ],
        out_specs=pl.BlockSpec((1, R, NB), lambda l, n: (l, 0, n)),
        out_shape=jax.ShapeDtypeStruct((L, R, N), F32),
        compiler_params=pltpu.CompilerParams(
            dimension_semantics=("arbitrary", "arbitrary"), vmem_limit_bytes=VMEM_LIMIT),
        name="ada",
    )(c_rows, ada_w, ada_b.reshape(L, 1, N))


def _proj_kernel(z_ref, m_ref, g_ref, win_ref, wup_ref, aup_ref, w0_ref, a0_ref, kk_ref, ka_ref,
                 rk_ref, gup_ref, ones_ref,
                 cb_ref, u_ref, r_ref, kkn_ref, v_ref, lw_ref, b_ref, kd_ref, bonus_ref, gate_ref):
    m = m_ref[0, 0]
    h = _rms(z_ref[0]) * g_ref[...]
    h = (h * (1.0 + m[1:2]) + m[0:1]).astype(BF16)

    def proj(lo, width):
        return jnp.dot(h, win_ref[:, lo:lo + width], preferred_element_type=F32)

    cb_ref[0] = proj(0, D_CONV)
    u_ref[0] = proj(D_CONV, D_CONV) * proj(2 * D_CONV, D_CONV)
    base = 3 * D_CONV
    r = proj(base, D_RWKV)
    k = proj(base + D_RWKV, D_RWKV)
    v = proj(base + 2 * D_RWKV, D_RWKV)
    lora = proj(base + 3 * D_RWKV, LORA_WA + G_LORA)
    wa = lora[:, :LORA_WA]
    gl = lora[:, LORA_WA:]
    ones_bd = ones_ref[...]

    kraw = k * kk_ref[...]
    kkn = kraw * lax.rsqrt(jnp.maximum(_segsum(kraw * kraw, ones_bd), 1e-24))
    r_ref[0] = r
    v_ref[0] = v
    kkn_ref[0] = kkn
    bonus_ref[0] = _segsum(r * k * rk_ref[...], ones_bd) * v
    gate_ref[0] = jnp.dot(jax.nn.sigmoid(gl).astype(BF16), gup_ref[...], preferred_element_type=F32)

    twa = jnp.tanh(wa).astype(BF16)
    wab = wa.astype(BF16)
    for d in range(2):
        lw_ref[d, 0] = -DECAY_SCALE * jax.nn.sigmoid(
            w0_ref[d:d + 1] + jnp.dot(twa, wup_ref[d], preferred_element_type=F32))
        a = jax.nn.sigmoid(a0_ref[d:d + 1] + jnp.dot(wab, aup_ref[d], preferred_element_type=F32))
        b_ref[d, 0] = kkn * a
        kd_ref[d, 0] = k * (1.0 + (a - 1.0) * ka_ref[...])


def _proj(z, modsel, g1, win, wup, aup, w0, a0, k_k, k_a, r_k, gup, ones_bd):
    B, T, D = z.shape
    NTL = T // TM
    P = win.shape[1]
    tok = pl.BlockSpec((1, TM, D_RWKV), lambda b, i: (b, i, 0))
    tok2 = pl.BlockSpec((2, 1, TM, D_RWKV), lambda b, i: (0, b, i, 0))
    const = lambda shape: pl.BlockSpec(shape, lambda b, i: (0,) * len(shape))
    one = jax.ShapeDtypeStruct((B, T, D_RWKV), F32)
    two = jax.ShapeDtypeStruct((2, B, T, D_RWKV), F32)
    return pl.pallas_call(
        _proj_kernel,
        grid=(B, NTL),
        in_specs=[
            pl.BlockSpec((1, TM, D), lambda b, i: (b, i, 0)),
            pl.BlockSpec((1, 1, 8, D), lambda b, i: (b, jnp.minimum(i, 1), 0, 0)),
            const((1, D)), const((D, P)), const((2, LORA_WA, D_RWKV)), const((2, LORA_WA, D_RWKV)),
            const((2, D_RWKV)), const((2, D_RWKV)), const((1, D_RWKV)), const((1, D_RWKV)),
            const((1, D_RWKV)), const((G_LORA, D_RWKV)), const((D_RWKV, D_RWKV)),
        ],
        out_specs=[tok, tok, tok, tok, tok, tok2, tok2, tok2, tok, tok],
        out_shape=[one, one, one, one, one, two, two, two, one, one],
        compiler_params=pltpu.CompilerParams(
            dimension_semantics=("parallel", "parallel"), vmem_limit_bytes=VMEM_LIMIT),
        name="proj",
    )(z, modsel, g1, win, wup, aup, w0, a0, k_k, k_a, r_k, gup, ones_bd)


def _scan_kernel(r_ref, kk_ref, v_ref, lw_ref, b_ref, kd_ref, y_ref, s_ref):
    C = CHUNK
    d = pl.program_id(1)

    @pl.when(pl.program_id(2) == 0)
    def _():
        s_ref[...] = jnp.zeros_like(s_ref)

    sgn = 1 - 2 * d
    row = lax.broadcasted_iota(jnp.int32, (C, QUAD), 0)
    col = lax.broadcasted_iota(jnp.int32, (C, QUAD), 1) & (HEAD - 1)
    tdiff = (row - col) * sgn
    strict = tdiff > 0
    incl = tdiff >= 0
    eye = (row == col).astype(F32)
    blk16 = (row >> 4) == (col >> 4)
    bdmask = (lax.broadcasted_iota(jnp.int32, (QUAD, QUAD), 0) >> 6) == \
             (lax.broadcasted_iota(jnp.int32, (QUAD, QUAD), 1) >> 6)
    rc = lax.broadcasted_iota(jnp.int32, (C, C), 0)
    cc = lax.broadcasted_iota(jnp.int32, (C, C), 1)
    tri = (((rc - cc) * sgn) >= 0).astype(BF16)
    eye_t = jnp.concatenate([(rc == cc).astype(BF16)] * 4, axis=1)

    def bd_split(x):
        xt = jnp.where(bdmask, jnp.concatenate([x] * 4, axis=0), 0.0)
        return _split(xt)

    def bdmm(p, x):
        return _mm3(p, *bd_split(x), NN)

    def ntmm(p, x):
        return _mm3(p, *bd_split(x), NT)

    def diag_blocks(full):
        lane_head = lax.broadcasted_iota(jnp.int32, (HEAD, QUAD), 1) >> 6
        out = jnp.zeros((HEAD, QUAD), F32)
        for hh in range(4):
            out = out + jnp.where(lane_head == hh, full[hh * HEAD:(hh + 1) * HEAD], 0.0)
        return out

    def tnmm(x, y_hi, y_lo):
        return diag_blocks(_mm3(x, y_hi, y_lo, TN))

    lw = lw_ref[0, 0]
    h1 = lw.astype(BF16)
    r1 = lw - h1.astype(F32)
    h2 = r1.astype(BF16)
    h3 = (r1 - h2.astype(F32)).astype(BF16)
    c = (jnp.dot(tri, h1, preferred_element_type=F32) + jnp.dot(tri, h2, preferred_element_type=F32)
         + jnp.dot(tri, h3, preferred_element_type=F32))
    c_last = jnp.where(d == 1, c[0:1], c[C - 1:C])

    for q in range(D_RWKV // QUAD):
        sl = slice(q * QUAD, (q + 1) * QUAD)
        cq, lwq, clq = c[:, sl], lw[:, sl], c_last[:, sl]
        r, kk, v = r_ref[0, :, sl], kk_ref[0, :, sl], v_ref[0, :, sl]
        b, kd = b_ref[0, 0, :, sl], kd_ref[0, 0, :, sl]
        en = jnp.exp(-cq)
        at = kk * jnp.exp(cq - lwq)
        rt = r * jnp.exp(cq)
        eh = jnp.exp(clq - cq)
        bh, kh = b * eh, kd * eh
        p_c = jnp.exp(clq)

        lhs = jnp.concatenate([at, rt], axis=0)
        ab = ntmm(lhs, b * en)
        ak = ntmm(lhs, kd * en)
        lmat = jnp.where(strict, ab[:C], 0.0)
        aak = jnp.where(strict, ak[:C], 0.0)
        arb = jnp.where(incl, ab[C:], 0.0)
        ark = jnp.where(incl, ak[C:], 0.0)

        ld = jnp.where(blk16, lmat, 0.0)
        lo = lmat - ld
        p1 = eye - ld
        l2 = bdmm(ld, ld)
        p2 = p1 + bdmm(p1, l2)
        l4 = bdmm(l2, l2)
        p3 = p2 + bdmm(p2, l4)
        l8 = bdmm(l4, l4)
        dinv = p3 + bdmm(p3, l8)
        mm = bdmm(dinv, lo)
        mm2 = bdmm(mm, mm)
        g = eye - mm + mm2 - bdmm(mm, mm2)
        tinv = bdmm(g, dinv)

        v_sp = bd_split(v)
        akv = _mm3(aak, *v_sp, NN)
        wm = bdmm(tinv, at)
        um = bdmm(tinv, akv)
        qh = rt - bdmm(arb, wm)
        yl = _mm3(ark, *v_sp, NN) - bdmm(arb, um)
        umt = tnmm(um, eye_t, jnp.zeros_like(eye_t))
        ds = tnmm(v, *_split(kh))

        s0 = s_ref[:, sl]
        ut = -(ntmm(s0, wm) + umt)
        y_ref[0, 0, :, sl] = ntmm(qh, s0) + yl
        s_ref[:, sl] = s0 * p_c + bdmm(ut, bh) + ds


def _scan(r, kk, v, lw, b, kd, ctx_len):
    B, T, W = r.shape
    C = CHUNK
    NC = T // C
    nctx = ctx_len // C

    def chunk(d, s):
        back = jnp.where(s < nctx, nctx - 1 - s, nctx + NC - 1 - s)
        return jnp.where(d == 0, s, back)

    shared = pl.BlockSpec((1, C, W), lambda bb, d, s: (bb, chunk(d, s), 0))
    perdir = pl.BlockSpec((1, 1, C, W), lambda bb, d, s: (d, bb, chunk(d, s), 0))
    return pl.pallas_call(
        _scan_kernel,
        grid=(B, 2, NC),
        in_specs=[shared, shared, shared, perdir, perdir, perdir],
        out_specs=perdir,
        out_shape=jax.ShapeDtypeStruct((2, B, T, W), F32),
        scratch_shapes=[pltpu.VMEM((HEAD, W), F32)],
        compiler_params=pltpu.CompilerParams(
            dimension_semantics=("parallel", "parallel", "arbitrary"), vmem_limit_bytes=VMEM_LIMIT),
        name="scan",
    )(r, kk, v, lw, b, kd)


def _mix_kernel(z_ref, m_ref, cb_ref, u_ref, up_ref, un_ref, y_ref, bonus_ref, gate_ref,
                cw_ref, lng_ref, lnb_ref, ones_ref, wout_ref, g2_ref, wup_ref,
                x1_ref, gp_ref, val_ref, *, n_tiles, d_ff):
    i = pl.program_id(1)
    m = m_ref[0, 0]
    prev_ok = (i >= 2).astype(F32)
    next_ok = jnp.logical_and(i >= 1, i <= n_tiles - 2).astype(F32)
    u = u_ref[0]
    rowi = lax.broadcasted_iota(jnp.int32, (TM, 1), 0)
    um1 = jnp.where(rowi == 0, up_ref[0][7:8] * prev_ok, pltpu.roll(u, 1, 0))
    up1 = jnp.where(rowi == TM - 1, un_ref[0][0:1] * next_ok, pltpu.roll(u, TM - 1, 0))
    cw = cw_ref[...]
    yconv = cb_ref[0] * (cw[0:1] * um1 + cw[1:2] * u + cw[2:3] * up1)

    ones_bd = ones_ref[...]
    y = y_ref[0, 0] + y_ref[1, 0]
    mu = _segsum(y, ones_bd) * (1.0 / HEAD)
    dlt = y - mu
    var = _segsum(dlt * dlt, ones_bd) * (1.0 / HEAD)
    yn = dlt * lax.rsqrt(var + GN_EPS) * lng_ref[...] + lnb_ref[...]
    yrw = (yn + bonus_ref[0]) * gate_ref[0]

    out = (jnp.dot(yconv.astype(BF16), wout_ref[0:D_CONV, :], preferred_element_type=F32)
           + jnp.dot(yrw.astype(BF16), wout_ref[D_CONV:D_CONV + D_RWKV, :], preferred_element_type=F32))
    x1 = z_ref[0] + m[2:3] * out
    x1_ref[0] = x1
    h2 = _rms(x1) * g2_ref[...]
    h2 = (h2 * (1.0 + m[4:5]) + m[3:4]).astype(BF16)
    NB = 256
    for n in range(0, d_ff, NB):
        gp_ref[0, :, n:n + NB] = jnp.dot(h2, wup_ref[:, n:n + NB], preferred_element_type=F32)
        val_ref[0, :, n:n + NB] = jnp.dot(h2, wup_ref[:, d_ff + n:d_ff + n + NB],
                                          preferred_element_type=F32)


def _mix(z, modsel, cb, u, y, bonus, gate, cw, lng, lnb, ones_bd, wout, g2, wup):
    B, T, D = z.shape
    NTL = T // TM
    d_ff = wup.shape[1] // 2
    tok = pl.BlockSpec((1, TM, D_RWKV), lambda b, i: (b, i, 0))
    const = lambda shape: pl.BlockSpec(shape, lambda b, i: (0,) * len(shape))
    R8 = TM // 8
    return pl.pallas_call(
        functools.partial(_mix_kernel, n_tiles=NTL, d_ff=d_ff),
        grid=(B, NTL),
        in_specs=[
            pl.BlockSpec((1, TM, D), lambda b, i: (b, i, 0)),
            pl.BlockSpec((1, 1, 8, D), lambda b, i: (b, jnp.minimum(i, 1), 0, 0)),
            tok, tok,
            pl.BlockSpec((1, 8, D_CONV), lambda b, i: (b, jnp.maximum(i * R8 - 1, 0), 0)),
            pl.BlockSpec((1, 8, D_CONV), lambda b, i: (b, jnp.minimum((i + 1) * R8, T // 8 - 1), 0)),
            pl.BlockSpec((2, 1, TM, D_RWKV), lambda b, i: (0, b, i, 0)),
            tok, tok,
            const((8, D_CONV)), const((1, D_RWKV)), const((1, D_RWKV)), const((D_RWKV, D_RWKV)),
            const((D_CONV + D_RWKV, D)), const((1, D)), const((D, 2 * d_ff)),
        ],
        out_specs=[
            pl.BlockSpec((1, TM, D), lambda b, i: (b, i, 0)),
            pl.BlockSpec((1, TM, d_ff), lambda b, i: (b, i, 0)),
            pl.BlockSpec((1, TM, d_ff), lambda b, i: (b, i, 0)),
        ],
        out_shape=[
            jax.ShapeDtypeStruct((B, T, D), F32),
            jax.ShapeDtypeStruct((B, T, d_ff), F32),
            jax.ShapeDtypeStruct((B, T, d_ff), F32),
        ],
        compiler_params=pltpu.CompilerParams(
            dimension_semantics=("parallel", "parallel"), vmem_limit_bytes=VMEM_LIMIT),
        name="mix",
    )(z, modsel, cb, u, u, u, y, bonus, gate, cw, lng, lnb, ones_bd, wout, g2, wup)


def _ffn_kernel(x1_ref, m_ref, gp_ref, gt_ref, gb_ref, val_ref, cw_ref, cbias_ref, wdn_ref, fg_ref,
                o_ref, *, n_tiles, d_ff, final):
    i = pl.program_id(1)
    m = m_ref[0, 0]
    lat = (i >= 1).astype(F32)
    top_ok = (i >= 2).astype(F32)
    bot_ok = jnp.logical_and(i >= 1, i <= n_tiles - 2).astype(F32)
    rowi = lax.broadcasted_iota(jnp.int32, (TM, 1), 0)
    colp = jnp.where(i == 0, rowi, rowi & (GRID_W - 1))
    width = jnp.where(i == 0, TM, GRID_W)
    lmask = (colp > 0).astype(F32)
    rmask = (colp < width - 1).astype(F32)

    NB = 256
    acc = jnp.zeros((TM, o_ref.shape[-1]), F32)
    for n in range(0, d_ff, NB):
        g = gp_ref[0, :, n:n + NB]
        gu = jnp.concatenate([gt_ref[0, :, n:n + NB] * top_ok, g[:TM - GRID_W]], axis=0) * lat
        gd = jnp.concatenate([g[GRID_W:], gb_ref[0, :, n:n + NB] * bot_ok], axis=0) * lat
        w = cw_ref[:, n:n + NB]
        h0 = w[0:1] * gu + w[3:4] * g + w[6:7] * gd
        h1 = w[1:2] * gu + w[4:5] * g + w[7:8] * gd
        h2 = w[2:3] * gu + w[5:6] * g + w[8:9] * gd
        conv = (lmask * pltpu.roll(h0, 1, 0) + h1 + rmask * pltpu.roll(h2, TM - 1, 0)
                + cbias_ref[:, n:n + NB])
        act = (_silu(conv) * val_ref[0, :, n:n + NB]).astype(BF16)
        acc = acc + jnp.dot(act, wdn_ref[n:n + NB, :], preferred_element_type=F32)
    x2 = x1_ref[0] + m[5:6] * acc
    if final:
        x2 = _rms(x2) * fg_ref[...]
    o_ref[0] = x2


def _ffn(x1, modsel, gp, val, cw, cbias, wdn, fg, final):
    B, T, D = x1.shape
    NTL = T // TM
    d_ff = gp.shape[-1]
    RW = TM // GRID_W
    const = lambda shape: pl.BlockSpec(shape, lambda b, i: (0,) * len(shape))
    big = pl.BlockSpec((1, TM, d_ff), lambda b, i: (b, i, 0))
    return pl.pallas_call(
        functools.partial(_ffn_kernel, n_tiles=NTL, d_ff=d_ff, final=final),
        grid=(B, NTL),
        in_specs=[
            pl.BlockSpec((1, TM, D), lambda b, i: (b, i, 0)),
            pl.BlockSpec((1, 1, 8, D), lambda b, i: (b, jnp.minimum(i, 1), 0, 0)),
            big,
            pl.BlockSpec((1, GRID_W, d_ff), lambda b, i: (b, jnp.maximum(i * RW - 1, 0), 0)),
            pl.BlockSpec((1, GRID_W, d_ff),
                         lambda b, i: (b, jnp.minimum((i + 1) * RW, T // GRID_W - 1), 0)),
            big,
            const((16, d_ff)), const((1, d_ff)), const((d_ff, D)), const((1, D)),
        ],
        out_specs=pl.BlockSpec((1, TM, D), lambda b, i: (b, i, 0)),
        out_shape=jax.ShapeDtypeStruct((B, T, D), F32),
        compiler_params=pltpu.CompilerParams(
            dimension_semantics=("parallel", "parallel"), vmem_limit_bytes=VMEM_LIMIT),
        name="ffn",
    )(x1, modsel, gp, gp, gp, val, cw, cbias, wdn, fg)


def kernel(x, c, ctx, c_ctx, ada_w, ada_b, norm1_g, norm2_g, w_in, conv_a_w, rw_w0, rw_w_up, rw_a0,
           rw_a_up, rw_k_k, rw_k_a, rw_r_k, rw_g_up, rw_ln_g, rw_ln_b, w_out, ffn_w_up, ffn_conv_w,
           ffn_conv_b, ffn_w_down, final_g):
    B, SEQ, D = x.shape
    CTX = ctx.shape[1]
    L = w_in.shape[0]
    d_ff = ffn_w_down.shape[1]
    assert CTX == TM and SEQ % TM == 0 and TM % GRID_W == 0 and CHUNK == HEAD
    assert w_in.shape[2] == 3 * D_CONV + 3 * D_RWKV + LORA_WA + G_LORA and d_ff % 256 == 0

    rows = -(-(B + 1) // 8) * 8
    c_rows = jnp.zeros((rows, D), F32).at[:B].set(c).at[B].set(c_ctx)
    mod = _ada(c_rows, ada_w, ada_b)

    hw = LORA_WA // 2
    zpad = jnp.zeros((L, 2, hw, D_RWKV), F32)
    wup_ext = jnp.concatenate([rw_w_up, zpad], axis=2).astype(BF16)
    aup_ext = jnp.concatenate([zpad, rw_a_up], axis=2).astype(BF16)
    head_of = jnp.arange(D_RWKV) // HEAD
    ones_bd = (head_of[:, None] == head_of[None, :]).astype(BF16)
    cw_a = jnp.zeros((L, 8, D_CONV), F32).at[:, :3].set(conv_a_w)
    cw_f = jnp.zeros((L, 16, d_ff), F32).at[:, :9].set(ffn_conv_w.reshape(L, 9, d_ff))

    z = jnp.concatenate([ctx, x], axis=1)
    for l in range(L):
        lat = mod[l, :B].reshape(B, 6, D)
        cm = jnp.broadcast_to(mod[l, B].reshape(1, 6, D), (B, 6, D))
        modsel = jnp.zeros((B, 2, 8, D), F32).at[:, 0, :6].set(cm).at[:, 1, :6].set(lat)

        cb, u, r, kkn, v, lw, b, kd, bonus, gate = _proj(
            z, modsel, norm1_g[l].reshape(1, D), w_in[l].astype(BF16), wup_ext[l], aup_ext[l],
            rw_w0[l], rw_a0[l], rw_k_k[l].reshape(1, -1), rw_k_a[l].reshape(1, -1),
            rw_r_k[l].reshape(1, -1), rw_g_up[l].astype(BF16), ones_bd)
        y = _scan(r, kkn, v, lw, b, kd, CTX)
        x1, gp, val = _mix(
            z, modsel, cb, u, y, bonus, gate, cw_a[l], rw_ln_g[l].reshape(1, -1),
            rw_ln_b[l].reshape(1, -1), ones_bd, w_out[l].astype(BF16), norm2_g[l].reshape(1, D),
            ffn_w_up[l].astype(BF16))
        z = _ffn(x1, modsel, gp, val, cw_f[l], ffn_conv_b[l].reshape(1, -1),
                 ffn_w_down[l].astype(BF16), final_g.reshape(1, D), final=(l == L - 1))
    return z[:, CTX:]
```

```python
import functools
import math

import jax
import jax.numpy as jnp
from jax import lax
from jax.experimental import pallas as pl
from jax.experimental.pallas import tpu as pltpu

F32 = jnp.float32
BF16 = jnp.bfloat16

HEAD = 64
D_CONV = 512
D_RWKV = 512
LORA_WA = 128
G_LORA = 128
GRID_W = 64
RMS_EPS = 1e-6
GN_EPS = 64e-5
DECAY_SCALE = math.exp(-0.5)

TM = 256
CHUNK = 64
QUAD = 4 * HEAD
VMEM_LIMIT = 56 * 1024 * 1024

NN = (((1,), (0,)), ((), ()))
NT = (((1,), (1,)), ((), ()))
TN = (((0,), (0,)), ((), ()))


def _silu(x):
    return x * jax.nn.sigmoid(x)


def _split(x):
    hi = x.astype(BF16)
    return hi, (x - hi.astype(F32)).astype(BF16)


def _dg(a, b, dims):
    return lax.dot_general(a, b, dims, preferred_element_type=F32)


def _mm3(a, b_hi, b_lo, dims):
    a_hi, a_lo = _split(a)
    return _dg(a_hi, b_hi, dims) + (_dg(a_hi, b_lo, dims) + _dg(a_lo, b_hi, dims))


def _rms(x):
    return x * lax.rsqrt(jnp.mean(x * x, axis=-1, keepdims=True) + RMS_EPS)


def _segsum(t, ones_bd):
    hi, lo = _split(t)
    return (jnp.dot(hi, ones_bd, preferred_element_type=F32)
            + jnp.dot(lo, ones_bd, preferred_element_type=F32))


def _ada_kernel(c_ref, w_ref, b_ref, o_ref):
    s = _silu(c_ref[...])
    w_hi, w_lo = _split(w_ref[0])
    o_ref[0] = _mm3(s, w_hi, w_lo, NN) + b_ref[0]


def _ada(c_rows, ada_w, ada_b):
    L, D, N = ada_w.shape
    R = c_rows.shape[0]
    NB = 1536
    return pl.pallas_call(
        _ada_kernel,
        grid=(L, N // NB),
        in_specs=[
            pl.BlockSpec((R, D), lambda l, n: (0, 0)),
            pl.BlockSpec((1, D, NB), lambda l, n: (l, 0, n)),
            pl.BlockSpec((1, 1, NB), lambda l, n: (l, 0, n)),
        ],
        out_specs=pl.BlockSpec((1, R, NB), lambda l, n: (l, 0, n)),
        out_shape=jax.ShapeDtypeStruct((L, R, N), F32),
        compiler_params=pltpu.CompilerParams(
            dimension_semantics=("arbitrary", "arbitrary"), vmem_limit_bytes=VMEM_LIMIT),
        name="ada",
    )(c_rows, ada_w, ada_b.reshape(L, 1, N))


def _proj_kernel(z_ref, m_ref, g_ref, win_ref, wup_ref, aup_ref, w0_ref, a0_ref, kk_ref, ka_ref,
                 rk_ref, gup_ref, ones_ref,
                 cb_ref, u_ref, r_ref, kkn_ref, v_ref, lw_ref, b_ref, kd_ref, bonus_ref, gate_ref):
    m = m_ref[0, 0]
    h = _rms(z_ref[0]) * g_ref[...]
    h = (h * (1.0 + m[1:2]) + m[0:1]).astype(BF16)

    def proj(lo, width):
        return jnp.dot(h, win_ref[:, lo:lo + width], preferred_element_type=F32)

    cb_ref[0] = proj(0, D_CONV)
    u_ref[0] = proj(D_CONV, D_CONV) * proj(2 * D_CONV, D_CONV)
    base = 3 * D_CONV
    r = proj(base, D_RWKV)
    k = proj(base + D_RWKV, D_RWKV)
    v = proj(base + 2 * D_RWKV, D_RWKV)
    lora = proj(base + 3 * D_RWKV, LORA_WA + G_LORA)
    wa = lora[:, :LORA_WA]
    gl = lora[:, LORA_WA:]
    ones_bd = ones_ref[...]

    kraw = k * kk_ref[...]
    kkn = kraw * lax.rsqrt(jnp.maximum(_segsum(kraw * kraw, ones_bd), 1e-24))
    r_ref[0] = r
    v_ref[0] = v
    kkn_ref[0] = kkn
    bonus_ref[0] = _segsum(r * k * rk_ref[...], ones_bd) * v
    gate_ref[0] = jnp.dot(jax.nn.sigmoid(gl).astype(BF16), gup_ref[...], preferred_element_type=F32)

    twa = jnp.tanh(wa).astype(BF16)
    wab = wa.astype(BF16)
    for d in range(2):
        lw_ref[d, 0] = -DECAY_SCALE * jax.nn.sigmoid(
            w0_ref[d:d + 1] + jnp.dot(twa, wup_ref[d], preferred_element_type=F32))
        a = jax.nn.sigmoid(a0_ref[d:d + 1] + jnp.dot(wab, aup_ref[d], preferred_element_type=F32))
        b_ref[d, 0] = kkn * a
        kd_ref[d, 0] = k * (1.0 + (a - 1.0) * ka_ref[...])


def _proj(z, modsel, g1, win, wup, aup, w0, a0, k_k, k_a, r_k, gup, ones_bd):
    B, T, D = z.shape
    NTL = T // TM
    P = win.shape[1]
    tok = pl.BlockSpec((1, TM, D_RWKV), lambda b, i: (b, i, 0))
    tok2 = pl.BlockSpec((2, 1, TM, D_RWKV), lambda b, i: (0, b, i, 0))
    const = lambda shape: pl.BlockSpec(shape, lambda b, i: (0,) * len(shape))
    one = jax.ShapeDtypeStruct((B, T, D_RWKV), F32)
    two = jax.ShapeDtypeStruct((2, B, T, D_RWKV), F32)
    return pl.pallas_call(
        _proj_kernel,
        grid=(B, NTL),
        in_specs=[
            pl.BlockSpec((1, TM, D), lambda b, i: (b, i, 0)),
            pl.BlockSpec((1, 1, 8, D), lambda b, i: (b, jnp.minimum(i, 1), 0, 0)),
            const((1, D)), const((D, P)), const((2, LORA_WA, D_RWKV)), const((2, LORA_WA, D_RWKV)),
            const((2, D_RWKV)), const((2, D_RWKV)), const((1, D_RWKV)), const((1, D_RWKV)),
            const((1, D_RWKV)), const((G_LORA, D_RWKV)), const((D_RWKV, D_RWKV)),
        ],
        out_specs=[tok, tok, tok, tok, tok, tok2, tok2, tok2, tok, tok],
        out_shape=[one, one, one, one, one, two, two, two, one, one],
        compiler_params=pltpu.CompilerParams(
            dimension_semantics=("parallel", "parallel"), vmem_limit_bytes=VMEM_LIMIT),
        name="proj",
    )(z, modsel, g1, win, wup, aup, w0, a0, k_k, k_a, r_k, gup, ones_bd)


P_NT, P_INV, P_LOC, P_ST = 3, 3, 3, 3


def _parts(x, passes):
    hi = x.astype(BF16)
    if passes == 1:
        return (hi,)
    return (hi, (x - hi.astype(F32)).astype(BF16))


def _mm(a, xb, dims, split_a=None):
    axis = 1 if dims == TN else 0
    m = a.shape[axis]
    a_hi = a.astype(BF16)
    if split_a is None:
        split_a = len(xb) == 2
    if not split_a:
        return _dg(a_hi, xb[0], dims)
    a_lo = (a - a_hi.astype(F32)).astype(BF16)
    top = _dg(jnp.concatenate([a_hi, a_lo], axis=axis), xb[0], dims)
    out = top[:m] + top[m:]
    if len(xb) == 2:
        out = out + _dg(a_hi, xb[1], dims)
    return out


def _scan_chunks(chains, bdmask):
    C = CHUNK
    row = lax.broadcasted_iota(jnp.int32, (C, QUAD), 0)
    col = lax.broadcasted_iota(jnp.int32, (C, QUAD), 1) & (HEAD - 1)
    eye = (row == col).astype(F32)
    blk16 = (row >> 4) == (col >> 4)
    lane_head = lax.broadcasted_iota(jnp.int32, (HEAD, QUAD), 1) >> 6
    zero_bf = jnp.zeros((), BF16)

    def each(f, *lists):
        return [f(*a) for a in zip(*lists)]

    def bd(x, passes):
        return tuple(jnp.where(bdmask, jnp.concatenate([p] * 4, axis=0), zero_bf)
                     for p in _parts(x, passes))

    def diag_blocks(full):
        out = jnp.where(lane_head == 0, full[0:HEAD], 0.0)
        for hh in range(1, 4):
            out = out + jnp.where(lane_head == hh, full[hh * HEAD:(hh + 1) * HEAD], 0.0)
        return out

    rev = [ch["rev"] for ch in chains]
    r, kk, v, lw, b, kd, s0, c = ([ch[n] for ch in chains]
                                  for n in ("r", "kk", "v", "lw", "b", "kd", "s0", "c"))
    strict = [(col > row) if rv else (col < row) for rv in rev]
    incl = [(col >= row) if rv else (col <= row) for rv in rev]
    c_last = [ci[0:1] if rv else ci[C - 1:C] for ci, rv in zip(c, rev)]
    en = each(lambda ci: jnp.exp(-ci), c)
    at = each(lambda k_, ci, lwi: k_ * jnp.exp(ci - lwi), kk, c, lw)
    rt = each(lambda r_, ci: r_ * jnp.exp(ci), r, c)
    eh = each(lambda cl, ci: jnp.exp(cl - ci), c_last, c)
    bh = each(lambda x, e: x * e, b, eh)
    kh = each(lambda x, e: x * e, kd, eh)
    p_c = each(jnp.exp, c_last)

    lhs = each(lambda a_, r_: jnp.concatenate([a_, r_], axis=0), at, rt)
    ab = each(lambda l_, x, e: _mm(l_, bd(x * e, P_NT), NT), lhs, b, en)
    ak = each(lambda l_, x, e: _mm(l_, bd(x * e, P_NT), NT), lhs, kd, en)
    lmat = each(lambda m_, x: jnp.where(m_, x[:C], 0.0), strict, ab)
    aak = each(lambda m_, x: jnp.where(m_, x[:C], 0.0), strict, ak)
    arb = each(lambda m_, x: jnp.where(m_, x[C:], 0.0), incl, ab)
    ark = each(lambda m_, x: jnp.where(m_, x[C:], 0.0), incl, ak)

    ld = each(lambda x: jnp.where(blk16, x, 0.0), lmat)
    lo = each(lambda x, y: x - y, lmat, ld)
    p1 = each(lambda x: eye - x, ld)
    l2 = each(lambda x: _mm(x, bd(x, P_INV), NN), ld)
    t = each(lambda p, x: _mm(jnp.concatenate([p, x], axis=0), bd(x, P_INV), NN), p1, l2)
    p2 = each(lambda p, x: p + x[:C], p1, t)
    l4 = each(lambda x: x[C:], t)
    t = each(lambda p, x: _mm(jnp.concatenate([p, x], axis=0), bd(x, P_INV), NN), p2, l4)
    p3 = each(lambda p, x: p + x[:C], p2, t)
    l8 = each(lambda x: x[C:], t)
    dinv = each(lambda p, x: p + _mm(p, bd(x, P_INV), NN), p3, l8)
    mm = each(lambda d_, x: _mm(d_, bd(x, P_INV), NN), dinv, lo)
    mm2 = each(lambda x: _mm(x, bd(x, P_INV), NN), mm)
    g = each(lambda x, x2: eye - x + x2 - _mm(x, bd(x2, P_INV), NN), mm, mm2)
    tinv = each(lambda g_, d_: _mm(g_, bd(d_, P_INV), NN), g, dinv)

    t = each(lambda x, y, v_: _mm(jnp.concatenate([x, y], axis=0), bd(v_, P_LOC), NN), aak, ark, v)
    akv = each(lambda x: x[:C], t)
    arkv = each(lambda x: x[C:], t)
    wm = each(lambda t_, x: _mm(t_, bd(x, P_LOC), NN), tinv, at)
    um = each(lambda t_, x: _mm(t_, bd(x, P_LOC), NN), tinv, akv)
    qh = each(lambda r_, a_, w_: r_ - _mm(a_, bd(w_, P_LOC), NN), rt, arb, wm)
    yl = each(lambda y_, a_, u_: y_ - _mm(a_, bd(u_, P_LOC), NN), arkv, arb, um)
    umt = each(lambda u_: diag_blocks(_mm(u_, (eye.astype(BF16),), TN, split_a=P_LOC == 3)), um)
    ds = each(lambda v_, k_: diag_blocks(_mm(v_, _parts(k_, P_LOC), TN)), v, kh)

    ut = each(lambda s_, w_, u_: -(_mm(s_, bd(w_, P_ST), NT) + u_), s0, wm, umt)
    y = each(lambda q_, s_, y_: _mm(q_, bd(s_, P_ST), NT) + y_, qh, s0, yl)
    s_new = each(lambda s_, p_, u_, b_, d_: s_ * p_ + _mm(u_, bd(b_, P_ST), NN) + d_,
                 s0, p_c, ut, bh, ds)
    return list(zip(y, s_new))


def _scan_kernel(rf_ref, kkf_ref, vf_ref, rb_ref, kkb_ref, vb_ref,
                 lwf_ref, bf_ref, kdf_ref, lwb_ref, bb_ref, kdb_ref, yf_ref, yb_ref, s_ref):
    C = CHUNK

    @pl.when(pl.program_id(1) == 0)
    def _():
        s_ref[...] = jnp.zeros_like(s_ref)

    bdmask = (lax.broadcasted_iota(jnp.int32, (QUAD, QUAD), 0) >> 6) == \
             (lax.broadcasted_iota(jnp.int32, (QUAD, QUAD), 1) >> 6)
    rc = lax.broadcasted_iota(jnp.int32, (C, C), 0)
    cc = lax.broadcasted_iota(jnp.int32, (C, C), 1)
    dirs = ((False, rf_ref, kkf_ref, vf_ref, lwf_ref, bf_ref, kdf_ref, yf_ref),
            (True, rb_ref, kkb_ref, vb_ref, lwb_ref, bb_ref, kdb_ref, yb_ref))
    chains, sinks = [], []
    for d, (rev, r_ref, kk_ref, v_ref, lw_ref, b_ref, kd_ref, y_ref) in enumerate(dirs):
        tri = ((cc >= rc) if rev else (cc <= rc)).astype(BF16)
        lw = lw_ref[0, 0]
        h1 = lw.astype(BF16)
        r1 = lw - h1.astype(F32)
        h2 = r1.astype(BF16)
        h3 = (r1 - h2.astype(F32)).astype(BF16)
        c = (jnp.dot(tri, h1, preferred_element_type=F32)
             + jnp.dot(tri, h2, preferred_element_type=F32)
             + jnp.dot(tri, h3, preferred_element_type=F32))
        for q in range(D_RWKV // QUAD):
            sl = slice(q * QUAD, (q + 1) * QUAD)
            chains.append(dict(rev=rev, r=r_ref[0, :, sl], kk=kk_ref[0, :, sl], v=v_ref[0, :, sl],
                               lw=lw[:, sl], b=b_ref[0, 0, :, sl], kd=kd_ref[0, 0, :, sl],
                               s0=s_ref[d, :, sl], c=c[:, sl]))
            sinks.append((y_ref, d, sl))
    for (y, s_new), (y_ref, d, sl) in zip(_scan_chunks(chains, bdmask), sinks):
        y_ref[0, :, sl] = y
        s_ref[d, :, sl] = s_new


def _scan(r, kk, v, lw, b, kd, ctx_len):
    B, T, W = r.shape
    C = CHUNK
    NC = T // C
    nctx = ctx_len // C

    def back(s):
        return jnp.where(s < nctx, nctx - 1 - s, nctx + NC - 1 - s)

    fwd = pl.BlockSpec((1, C, W), lambda bb, s: (bb, s, 0))
    bwd = pl.BlockSpec((1, C, W), lambda bb, s: (bb, back(s), 0))
    fwd_d = pl.BlockSpec((1, 1, C, W), lambda bb, s: (0, bb, s, 0))
    bwd_d = pl.BlockSpec((1, 1, C, W), lambda bb, s: (1, bb, back(s), 0))
    out = jax.ShapeDtypeStruct((B, T, W), F32)
    return pl.pallas_call(
        _scan_kernel,
        grid=(B, NC),
        in_specs=[fwd, fwd, fwd, bwd, bwd, bwd, fwd_d, fwd_d, fwd_d, bwd_d, bwd_d, bwd_d],
        out_specs=[fwd, bwd],
        out_shape=[out, out],
        scratch_shapes=[pltpu.VMEM((2, HEAD, W), F32)],
        compiler_params=pltpu.CompilerParams(
            dimension_semantics=("parallel", "arbitrary"), vmem_limit_bytes=VMEM_LIMIT),
        name="scan",
    )(r, kk, v, r, kk, v, lw, b, kd, lw, b, kd)


def _mix_kernel(z_ref, m_ref, cb_ref, u_ref, up_ref, un_ref, yf_ref, yb_ref, bonus_ref, gate_ref,
                cw_ref, lng_ref, lnb_ref, ones_ref, wout_ref, g2_ref, wup_ref,
                x1_ref, gp_ref, val_ref, *, n_tiles, d_ff):
    i = pl.program_id(1)
    m = m_ref[0, 0]
    prev_ok = (i >= 2).astype(F32)
    next_ok = jnp.logical_and(i >= 1, i <= n_tiles - 2).astype(F32)
    u = u_ref[0]
    rowi = lax.broadcasted_iota(jnp.int32, (TM, 1), 0)
    um1 = jnp.where(rowi == 0, up_ref[0][7:8] * prev_ok, pltpu.roll(u, 1, 0))
    up1 = jnp.where(rowi == TM - 1, un_ref[0][0:1] * next_ok, pltpu.roll(u, TM - 1, 0))
    cw = cw_ref[...]
    yconv = cb_ref[0] * (cw[0:1] * um1 + cw[1:2] * u + cw[2:3] * up1)

    ones_bd = ones_ref[...]
    y = yf_ref[0] + yb_ref[0]
    mu = _segsum(y, ones_bd) * (1.0 / HEAD)
    dlt = y - mu
    var = _segsum(dlt * dlt, ones_bd) * (1.0 / HEAD)
    yn = dlt * lax.rsqrt(var + GN_EPS) * lng_ref[...] + lnb_ref[...]
    yrw = (yn + bonus_ref[0]) * gate_ref[0]

    out = (jnp.dot(yconv.astype(BF16), wout_ref[0:D_CONV, :], preferred_element_type=F32)
           + jnp.dot(yrw.astype(BF16), wout_ref[D_CONV:D_CONV + D_RWKV, :], preferred_element_type=F32))
    x1 = z_ref[0] + m[2:3] * out
    x1_ref[0] = x1
    h2 = _rms(x1) * g2_ref[...]
    h2 = (h2 * (1.0 + m[4:5]) + m[3:4]).astype(BF16)
    NB = 256
    for n in range(0, d_ff, NB):
        gp_ref[0, :, n:n + NB] = jnp.dot(h2, wup_ref[:, n:n + NB], preferred_element_type=F32)
        val_ref[0, :, n:n + NB] = jnp.dot(h2, wup_ref[:, d_ff + n:d_ff + n + NB],
                                          preferred_element_type=F32)


def _mix(z, modsel, cb, u, yf, yb, bonus, gate, cw, lng, lnb, ones_bd, wout, g2, wup):
    B, T, D = z.shape
    NTL = T // TM
    d_ff = wup.shape[1] // 2
    tok = pl.BlockSpec((1, TM, D_RWKV), lambda b, i: (b, i, 0))
    const = lambda shape: pl.BlockSpec(shape, lambda b, i: (0,) * len(shape))
    R8 = TM // 8
    return pl.pallas_call(
        functools.partial(_mix_kernel, n_tiles=NTL, d_ff=d_ff),
        grid=(B, NTL),
        in_specs=[
            pl.BlockSpec((1, TM, D), lambda b, i: (b, i, 0)),
            pl.BlockSpec((1, 1, 8, D), lambda b, i: (b, jnp.minimum(i, 1), 0, 0)),
            tok, tok,
            pl.BlockSpec((1, 8, D_CONV), lambda b, i: (b, jnp.maximum(i * R8 - 1, 0), 0)),
            pl.BlockSpec((1, 8, D_CONV), lambda b, i: (b, jnp.minimum((i + 1) * R8, T // 8 - 1), 0)),
            tok, tok, tok, tok,
            const((8, D_CONV)), const((1, D_RWKV)), const((1, D_RWKV)), const((D_RWKV, D_RWKV)),
            const((D_CONV + D_RWKV, D)), const((1, D)), const((D, 2 * d_ff)),
        ],
        out_specs=[
            pl.BlockSpec((1, TM, D), lambda b, i: (b, i, 0)),
            pl.BlockSpec((1, TM, d_ff), lambda b, i: (b, i, 0)),
            pl.BlockSpec((1, TM, d_ff), lambda b, i: (b, i, 0)),
        ],
        out_shape=[
            jax.ShapeDtypeStruct((B, T, D), F32),
            jax.ShapeDtypeStruct((B, T, d_ff), F32),
            jax.ShapeDtypeStruct((B, T, d_ff), F32),
        ],
        compiler_params=pltpu.CompilerParams(
            dimension_semantics=("parallel", "parallel"), vmem_limit_bytes=VMEM_LIMIT),
        name="mix",
    )(z, modsel, cb, u, u, u, yf, yb, bonus, gate, cw, lng, lnb, ones_bd, wout, g2, wup)


def _ffn_kernel(x1_ref, m_ref, gp_ref, gt_ref, gb_ref, val_ref, cw_ref, cbias_ref, wdn_ref, fg_ref,
                o_ref, *, n_tiles, d_ff, final):
    i = pl.program_id(1)
    m = m_ref[0, 0]
    lat = (i >= 1).astype(F32)
    top_ok = (i >= 2).astype(F32)
    bot_ok = jnp.logical_and(i >= 1, i <= n_tiles - 2).astype(F32)
    rowi = lax.broadcasted_iota(jnp.int32, (TM, 1), 0)
    colp = jnp.where(i == 0, rowi, rowi & (GRID_W - 1))
    width = jnp.where(i == 0, TM, GRID_W)
    lmask = (colp > 0).astype(F32)
    rmask = (colp < width - 1).astype(F32)

    NB = 256
    acc = jnp.zeros((TM, o_ref.shape[-1]), F32)
    for n in range(0, d_ff, NB):
        g = gp_ref[0, :, n:n + NB]
        gu = jnp.concatenate([gt_ref[0, :, n:n + NB] * top_ok, g[:TM - GRID_W]], axis=0) * lat
        gd = jnp.concatenate([g[GRID_W:], gb_ref[0, :, n:n + NB] * bot_ok], axis=0) * lat
        w = cw_ref[:, n:n + NB]
        h0 = w[0:1] * gu + w[3:4] * g + w[6:7] * gd
        h1 = w[1:2] * gu + w[4:5] * g + w[7:8] * gd
        h2 = w[2:3] * gu + w[5:6] * g + w[8:9] * gd
        conv = (lmask * pltpu.roll(h0, 1, 0) + h1 + rmask * pltpu.roll(h2, TM - 1, 0)
                + cbias_ref[:, n:n + NB])
        act = (_silu(conv) * val_ref[0, :, n:n + NB]).astype(BF16)
        acc = acc + jnp.dot(act, wdn_ref[n:n + NB, :], preferred_element_type=F32)
    x2 = x1_ref[0] + m[5:6] * acc
    if final:
        x2 = _rms(x2) * fg_ref[...]
    o_ref[0] = x2


def _ffn(x1, modsel, gp, val, cw, cbias, wdn, fg, final):
    B, T, D = x1.shape
    NTL = T // TM
    d_ff = gp.shape[-1]
    RW = TM // GRID_W
    const = lambda shape: pl.BlockSpec(shape, lambda b, i: (0,) * len(shape))
    big = pl.BlockSpec((1, TM, d_ff), lambda b, i: (b, i, 0))
    return pl.pallas_call(
        functools.partial(_ffn_kernel, n_tiles=NTL, d_ff=d_ff, final=final),
        grid=(B, NTL),
        in_specs=[
            pl.BlockSpec((1, TM, D), lambda b, i: (b, i, 0)),
            pl.BlockSpec((1, 1, 8, D), lambda b, i: (b, jnp.minimum(i, 1), 0, 0)),
            big,
            pl.BlockSpec((1, GRID_W, d_ff), lambda b, i: (b, jnp.maximum(i * RW - 1, 0), 0)),
            pl.BlockSpec((1, GRID_W, d_ff),
                         lambda b, i: (b, jnp.minimum((i + 1) * RW, T // GRID_W - 1), 0)),
            big,
            const((16, d_ff)), const((1, d_ff)), const((d_ff, D)), const((1, D)),
        ],
        out_specs=pl.BlockSpec((1, TM, D), lambda b, i: (b, i, 0)),
        out_shape=jax.ShapeDtypeStruct((B, T, D), F32),
        compiler_params=pltpu.CompilerParams(
            dimension_semantics=("parallel", "parallel"), vmem_limit_bytes=VMEM_LIMIT),
        name="ffn",
    )(x1, modsel, gp, gp, gp, val, cw, cbias, wdn, fg)


def kernel(x, c, ctx, c_ctx, ada_w, ada_b, norm1_g, norm2_g, w_in, conv_a_w, rw_w0, rw_w_up, rw_a0,
           rw_a_up, rw_k_k, rw_k_a, rw_r_k, rw_g_up, rw_ln_g, rw_ln_b, w_out, ffn_w_up, ffn_conv_w,
           ffn_conv_b, ffn_w_down, final_g):
    B, SEQ, D = x.shape
    CTX = ctx.shape[1]
    L = w_in.shape[0]
    d_ff = ffn_w_down.shape[1]
    assert CTX == TM and SEQ % TM == 0 and TM % GRID_W == 0 and CHUNK == HEAD
    assert w_in.shape[2] == 3 * D_CONV + 3 * D_RWKV + LORA_WA + G_LORA and d_ff % 256 == 0

    rows = -(-(B + 1) // 8) * 8
    c_rows = jnp.zeros((rows, D), F32).at[:B].set(c).at[B].set(c_ctx)
    mod = _ada(c_rows, ada_w, ada_b)

    hw = LORA_WA // 2
    zpad = jnp.zeros((L, 2, hw, D_RWKV), F32)
    wup_ext = jnp.concatenate([rw_w_up, zpad], axis=2).astype(BF16)
    aup_ext = jnp.concatenate([zpad, rw_a_up], axis=2).astype(BF16)
    head_of = jnp.arange(D_RWKV) // HEAD
    ones_bd = (head_of[:, None] == head_of[None, :]).astype(BF16)
    cw_a = jnp.zeros((L, 8, D_CONV), F32).at[:, :3].set(conv_a_w)
    cw_f = jnp.zeros((L, 16, d_ff), F32).at[:, :9].set(ffn_conv_w.reshape(L, 9, d_ff))

    z = jnp.concatenate([ctx, x], axis=1)
    for l in range(L):
        lat = mod[l, :B].reshape(B, 6, D)
        cm = jnp.broadcast_to(mod[l, B].reshape(1, 6, D), (B, 6, D))
        modsel = jnp.zeros((B, 2, 8, D), F32).at[:, 0, :6].set(cm).at[:, 1, :6].set(lat)

        cb, u, r, kkn, v, lw, b, kd, bonus, gate = _proj(
            z, modsel, norm1_g[l].reshape(1, D), w_in[l].astype(BF16), wup_ext[l], aup_ext[l],
            rw_w0[l], rw_a0[l], rw_k_k[l].reshape(1, -1), rw_k_a[l].reshape(1, -1),
            rw_r_k[l].reshape(1, -1), rw_g_up[l].astype(BF16), ones_bd)
        yf, yb = _scan(r, kkn, v, lw, b, kd, CTX)
        x1, gp, val = _mix(
            z, modsel, cb, u, yf, yb, bonus, gate, cw_a[l], rw_ln_g[l].reshape(1, -1),
            rw_ln_b[l].reshape(1, -1), ones_bd, w_out[l].astype(BF16), norm2_g[l].reshape(1, D),
            ffn_w_up[l].astype(BF16))
        z = _ffn(x1, modsel, gp, val, cw_f[l], ffn_conv_b[l].reshape(1, -1),
                 ffn_w_down[l].astype(BF16), final_g.reshape(1, D), final=(l == L - 1))
    return z[:, CTX:]
```

```python
import functools
import math

import jax
import jax.numpy as jnp
from jax import lax
from jax.experimental import pallas as pl
from jax.experimental.pallas import tpu as pltpu

F32 = jnp.float32
BF16 = jnp.bfloat16

HEAD = 64
D_CONV = 512
D_RWKV = 512
LORA_WA = 128
G_LORA = 128
GRID_W = 64
RMS_EPS = 1e-6
GN_EPS = 64e-5
DECAY_SCALE = math.exp(-0.5)

TM = 256
CHUNK = 64
QUAD = 4 * HEAD
VMEM_LIMIT = 56 * 1024 * 1024

NN = (((1,), (0,)), ((), ()))
NT = (((1,), (1,)), ((), ()))
TN = (((0,), (0,)), ((), ()))


def _silu(x):
    return x * jax.nn.sigmoid(x)


def _split(x):
    hi = x.astype(BF16)
    return hi, (x - hi.astype(F32)).astype(BF16)


def _dg(a, b, dims):
    return lax.dot_general(a, b, dims, preferred_element_type=F32)


def _mm3(a, b_hi, b_lo, dims):
    a_hi, a_lo = _split(a)
    return _dg(a_hi, b_hi, dims) + (_dg(a_hi, b_lo, dims) + _dg(a_lo, b_hi, dims))


def _rms(x):
    return x * lax.rsqrt(jnp.mean(x * x, axis=-1, keepdims=True) + RMS_EPS)


def _segsum(t, ones_bd):
    hi, lo = _split(t)
    return (jnp.dot(hi, ones_bd, preferred_element_type=F32)
            + jnp.dot(lo, ones_bd, preferred_element_type=F32))


def _ada_kernel(c_ref, w_ref, b_ref, o_ref):
    s = _silu(c_ref[...])
    w_hi, w_lo = _split(w_ref[0])
    o_ref[0] = _mm3(s, w_hi, w_lo, NN) + b_ref[0]


def _ada(c_rows, ada_w, ada_b):
    L, D, N = ada_w.shape
    R = c_rows.shape[0]
    NB = 1536
    return pl.pallas_call(
        _ada_kernel,
        grid=(L, N // NB),
        in_specs=[
            pl.BlockSpec((R, D), lambda l, n: (0, 0)),
            pl.BlockSpec((1, D, NB), lambda l, n: (l, 0, n)),
            pl.BlockSpec((1, 1, NB), lambda l, n: (l, 0, n)),
        ],
        out_specs=pl.BlockSpec((1, R, NB), lambda l, n: (l, 0, n)),
        out_shape=jax.ShapeDtypeStruct((L, R, N), F32),
        compiler_params=pltpu.CompilerParams(
            dimension_semantics=("arbitrary", "arbitrary"), vmem_limit_bytes=VMEM_LIMIT),
        name="ada",
    )(c_rows, ada_w, ada_b.reshape(L, 1, N))


def _proj_kernel(z_ref, m_ref, g_ref, win_ref, wup_ref, aup_ref, w0_ref, a0_ref, kk_ref, ka_ref,
                 rk_ref, gup_ref, ones_ref,
                 cb_ref, u_ref, r_ref, kkn_ref, v_ref, lw_ref, b_ref, kd_ref, bonus_ref, gate_ref):
    m = m_ref[0, 0]
    h = _rms(z_ref[0]) * g_ref[...]
    h = (h * (1.0 + m[1:2]) + m[0:1]).astype(BF16)

    def proj(lo, width):
        return jnp.dot(h, win_ref[:, lo:lo + width], preferred_element_type=F32)

    cb_ref[0] = proj(0, D_CONV)
    u_ref[0] = proj(D_CONV, D_CONV) * proj(2 * D_CONV, D_CONV)
    base = 3 * D_CONV
    r = proj(base, D_RWKV)
    k = proj(base + D_RWKV, D_RWKV)
    v = proj(base + 2 * D_RWKV, D_RWKV)
    lora = proj(base + 3 * D_RWKV, LORA_WA + G_LORA)
    wa = lora[:, :LORA_WA]
    gl = lora[:, LORA_WA:]
    ones_bd = ones_ref[...]

    kraw = k * kk_ref[...]
    kkn = kraw * lax.rsqrt(jnp.maximum(_segsum(kraw * kraw, ones_bd), 1e-24))
    r_ref[0] = r
    v_ref[0] = v
    kkn_ref[0] = kkn
    bonus_ref[0] = _segsum(r * k * rk_ref[...], ones_bd) * v
    gate_ref[0] = jnp.dot(jax.nn.sigmoid(gl).astype(BF16), gup_ref[...], preferred_element_type=F32)

    twa = jnp.tanh(wa).astype(BF16)
    wab = wa.astype(BF16)
    for d in range(2):
        lw_ref[d, 0] = -DECAY_SCALE * jax.nn.sigmoid(
            w0_ref[d:d + 1] + jnp.dot(twa, wup_ref[d], preferred_element_type=F32))
        a = jax.nn.sigmoid(a0_ref[d:d + 1] + jnp.dot(wab, aup_ref[d], preferred_element_type=F32))
        b_ref[d, 0] = kkn * a
        kd_ref[d, 0] = k * (1.0 + (a - 1.0) * ka_ref[...])


def _proj(z, modsel, g1, win, wup, aup, w0, a0, k_k, k_a, r_k, gup, ones_bd):
    B, T, D = z.shape
    NTL = T // TM
    P = win.shape[1]
    tok = pl.BlockSpec((1, TM, D_RWKV), lambda b, i: (b, i, 0))
    tok2 = pl.BlockSpec((2, 1, TM, D_RWKV), lambda b, i: (0, b, i, 0))
    const = lambda shape: pl.BlockSpec(shape, lambda b, i: (0,) * len(shape))
    one = jax.ShapeDtypeStruct((B, T, D_RWKV), F32)
    two = jax.ShapeDtypeStruct((2, B, T, D_RWKV), F32)
    return pl.pallas_call(
        _proj_kernel,
        grid=(B, NTL),
        in_specs=[
            pl.BlockSpec((1, TM, D), lambda b, i: (b, i, 0)),
            pl.BlockSpec((1, 1, 8, D), lambda b, i: (b, jnp.minimum(i, 1), 0, 0)),
            const((1, D)), const((D, P)), const((2, LORA_WA, D_RWKV)), const((2, LORA_WA, D_RWKV)),
            const((2, D_RWKV)), const((2, D_RWKV)), const((1, D_RWKV)), const((1, D_RWKV)),
            const((1, D_RWKV)), const((G_LORA, D_RWKV)), const((D_RWKV, D_RWKV)),
        ],
        out_specs=[tok, tok, tok, tok, tok, tok2, tok2, tok2, tok, tok],
        out_shape=[one, one, one, one, one, two, two, two, one, one],
        compiler_params=pltpu.CompilerParams(
            dimension_semantics=("parallel", "parallel"), vmem_limit_bytes=VMEM_LIMIT),
        name="proj",
    )(z, modsel, g1, win, wup, aup, w0, a0, k_k, k_a, r_k, gup, ones_bd)


P_NT, P_INV, P_LOC, P_ST = 1, 1, 1, 1


def _parts(x, passes):
    hi = x.astype(BF16)
    if passes == 1:
        return (hi,)
    return (hi, (x - hi.astype(F32)).astype(BF16))


def _mm(a, xb, dims, split_a=None):
    axis = 1 if dims == TN else 0
    m = a.shape[axis]
    a_hi = a.astype(BF16)
    if split_a is None:
        split_a = len(xb) == 2
    if not split_a:
        return _dg(a_hi, xb[0], dims)
    a_lo = (a - a_hi.astype(F32)).astype(BF16)
    top = _dg(jnp.concatenate([a_hi, a_lo], axis=axis), xb[0], dims)
    out = top[:m] + top[m:]
    if len(xb) == 2:
        out = out + _dg(a_hi, xb[1], dims)
    return out


def _scan_chunks(chains, bdmask):
    C = CHUNK
    row = lax.broadcasted_iota(jnp.int32, (C, QUAD), 0)
    col = lax.broadcasted_iota(jnp.int32, (C, QUAD), 1) & (HEAD - 1)
    eye = (row == col).astype(F32)
    blk16 = (row >> 4) == (col >> 4)
    lane_head = lax.broadcasted_iota(jnp.int32, (HEAD, QUAD), 1) >> 6
    zero_bf = jnp.zeros((), BF16)

    def each(f, *lists):
        return [f(*a) for a in zip(*lists)]

    def bd(x, passes):
        return tuple(jnp.where(bdmask, jnp.concatenate([p] * 4, axis=0), zero_bf)
                     for p in _parts(x, passes))

    def diag_blocks(full):
        out = jnp.where(lane_head == 0, full[0:HEAD], 0.0)
        for hh in range(1, 4):
            out = out + jnp.where(lane_head == hh, full[hh * HEAD:(hh + 1) * HEAD], 0.0)
        return out

    rev = [ch["rev"] for ch in chains]
    r, kk, v, lw, b, kd, s0, c = ([ch[n] for ch in chains]
                                  for n in ("r", "kk", "v", "lw", "b", "kd", "s0", "c"))
    strict = [(col > row) if rv else (col < row) for rv in rev]
    incl = [(col >= row) if rv else (col <= row) for rv in rev]
    c_last = [ci[0:1] if rv else ci[C - 1:C] for ci, rv in zip(c, rev)]
    en = each(lambda ci: jnp.exp(-ci), c)
    at = each(lambda k_, ci, lwi: k_ * jnp.exp(ci - lwi), kk, c, lw)
    rt = each(lambda r_, ci: r_ * jnp.exp(ci), r, c)
    eh = each(lambda cl, ci: jnp.exp(cl - ci), c_last, c)
    bh = each(lambda x, e: x * e, b, eh)
    kh = each(lambda x, e: x * e, kd, eh)
    p_c = each(jnp.exp, c_last)

    lhs = each(lambda a_, r_: jnp.concatenate([a_, r_], axis=0), at, rt)
    ab = each(lambda l_, x, e: _mm(l_, bd(x * e, P_NT), NT), lhs, b, en)
    ak = each(lambda l_, x, e: _mm(l_, bd(x * e, P_NT), NT), lhs, kd, en)
    lmat = each(lambda m_, x: jnp.where(m_, x[:C], 0.0), strict, ab)
    aak = each(lambda m_, x: jnp.where(m_, x[:C], 0.0), strict, ak)
    arb = each(lambda m_, x: jnp.where(m_, x[C:], 0.0), incl, ab)
    ark = each(lambda m_, x: jnp.where(m_, x[C:], 0.0), incl, ak)

    ld = each(lambda x: jnp.where(blk16, x, 0.0), lmat)
    lo = each(lambda x, y: x - y, lmat, ld)
    p1 = each(lambda x: eye - x, ld)
    l2 = each(lambda x: _mm(x, bd(x, P_INV), NN), ld)
    t = each(lambda p, x: _mm(jnp.concatenate([p, x], axis=0), bd(x, P_INV), NN), p1, l2)
    p2 = each(lambda p, x: p + x[:C], p1, t)
    l4 = each(lambda x: x[C:], t)
    t = each(lambda p, x: _mm(jnp.concatenate([p, x], axis=0), bd(x, P_INV), NN), p2, l4)
    p3 = each(lambda p, x: p + x[:C], p2, t)
    l8 = each(lambda x: x[C:], t)
    dinv = each(lambda p, x: p + _mm(p, bd(x, P_INV), NN), p3, l8)
    mm = each(lambda d_, x: _mm(d_, bd(x, P_INV), NN), dinv, lo)
    mm2 = each(lambda x: _mm(x, bd(x, P_INV), NN), mm)
    g = each(lambda x, x2: eye - x + x2 - _mm(x, bd(x2, P_INV), NN), mm, mm2)
    tinv = each(lambda g_, d_: _mm(g_, bd(d_, P_INV), NN), g, dinv)

    t = each(lambda x, y, v_: _mm(jnp.concatenate([x, y], axis=0), bd(v_, P_LOC), NN), aak, ark, v)
    akv = each(lambda x: x[:C], t)
    arkv = each(lambda x: x[C:], t)
    wm = each(lambda t_, x: _mm(t_, bd(x, P_LOC), NN), tinv, at)
    um = each(lambda t_, x: _mm(t_, bd(x, P_LOC), NN), tinv, akv)
    qh = each(lambda r_, a_, w_: r_ - _mm(a_, bd(w_, P_LOC), NN), rt, arb, wm)
    yl = each(lambda y_, a_, u_: y_ - _mm(a_, bd(u_, P_LOC), NN), arkv, arb, um)
    umt = each(lambda u_: diag_blocks(_mm(u_, (eye.astype(BF16),), TN, split_a=P_LOC == 3)), um)
    ds = each(lambda v_, k_: diag_blocks(_mm(v_, _parts(k_, P_LOC), TN)), v, kh)

    ut = each(lambda s_, w_, u_: -(_mm(s_, bd(w_, P_ST), NT) + u_), s0, wm, umt)
    y = each(lambda q_, s_, y_: _mm(q_, bd(s_, P_ST), NT) + y_, qh, s0, yl)
    s_new = each(lambda s_, p_, u_, b_, d_: s_ * p_ + _mm(u_, bd(b_, P_ST), NN) + d_,
                 s0, p_c, ut, bh, ds)
    return list(zip(y, s_new))


def _scan_kernel(rf_ref, kkf_ref, vf_ref, rb_ref, kkb_ref, vb_ref,
                 lwf_ref, bf_ref, kdf_ref, lwb_ref, bb_ref, kdb_ref, yf_ref, yb_ref, s_ref):
    C = CHUNK

    @pl.when(pl.program_id(1) == 0)
    def _():
        s_ref[...] = jnp.zeros_like(s_ref)

    bdmask = (lax.broadcasted_iota(jnp.int32, (QUAD, QUAD), 0) >> 6) == \
             (lax.broadcasted_iota(jnp.int32, (QUAD, QUAD), 1) >> 6)
    rc = lax.broadcasted_iota(jnp.int32, (C, C), 0)
    cc = lax.broadcasted_iota(jnp.int32, (C, C), 1)
    dirs = ((False, rf_ref, kkf_ref, vf_ref, lwf_ref, bf_ref, kdf_ref, yf_ref),
            (True, rb_ref, kkb_ref, vb_ref, lwb_ref, bb_ref, kdb_ref, yb_ref))
    chains, sinks = [], []
    for bi in range(s_ref.shape[0]):
        for d, (rev, r_ref, kk_ref, v_ref, lw_ref, b_ref, kd_ref, y_ref) in enumerate(dirs):
            tri = ((cc >= rc) if rev else (cc <= rc)).astype(BF16)
            lw = lw_ref[0, bi]
            h1 = lw.astype(BF16)
            r1 = lw - h1.astype(F32)
            h2 = r1.astype(BF16)
            h3 = (r1 - h2.astype(F32)).astype(BF16)
            c = (jnp.dot(tri, h1, preferred_element_type=F32)
                 + jnp.dot(tri, h2, preferred_element_type=F32)
                 + jnp.dot(tri, h3, preferred_element_type=F32))
            for q in range(D_RWKV // QUAD):
                sl = slice(q * QUAD, (q + 1) * QUAD)
                chains.append(dict(rev=rev, r=r_ref[bi, :, sl], kk=kk_ref[bi, :, sl],
                                   v=v_ref[bi, :, sl], lw=lw[:, sl], b=b_ref[0, bi, :, sl],
                                   kd=kd_ref[0, bi, :, sl], s0=s_ref[bi, d, :, sl], c=c[:, sl]))
                sinks.append((y_ref, bi, d, sl))
    for (y, s_new), (y_ref, bi, d, sl) in zip(_scan_chunks(chains, bdmask), sinks):
        y_ref[bi, :, sl] = y
        s_ref[bi, d, :, sl] = s_new


def _scan(r, kk, v, lw, b, kd, ctx_len):
    B, T, W = r.shape
    C = CHUNK
    NC = T // C
    nctx = ctx_len // C

    def back(s):
        return jnp.where(s < nctx, nctx - 1 - s, nctx + NC - 1 - s)

    BB = 2 if B % 2 == 0 else 1
    fwd = pl.BlockSpec((BB, C, W), lambda bb, s: (bb, s, 0))
    bwd = pl.BlockSpec((BB, C, W), lambda bb, s: (bb, back(s), 0))
    fwd_d = pl.BlockSpec((1, BB, C, W), lambda bb, s: (0, bb, s, 0))
    bwd_d = pl.BlockSpec((1, BB, C, W), lambda bb, s: (1, bb, back(s), 0))
    out = jax.ShapeDtypeStruct((B, T, W), F32)
    return pl.pallas_call(
        _scan_kernel,
        grid=(B // BB, NC),
        in_specs=[fwd, fwd, fwd, bwd, bwd, bwd, fwd_d, fwd_d, fwd_d, bwd_d, bwd_d, bwd_d],
        out_specs=[fwd, bwd],
        out_shape=[out, out],
        scratch_shapes=[pltpu.VMEM((BB, 2, HEAD, W), F32)],
        compiler_params=pltpu.CompilerParams(
            dimension_semantics=("parallel", "arbitrary"), vmem_limit_bytes=VMEM_LIMIT),
        name="scan",
    )(r, kk, v, r, kk, v, lw, b, kd, lw, b, kd)


def _mix_kernel(z_ref, m_ref, cb_ref, u_ref, up_ref, un_ref, yf_ref, yb_ref, bonus_ref, gate_ref,
                cw_ref, lng_ref, lnb_ref, ones_ref, wout_ref, g2_ref, wup_ref,
                x1_ref, gp_ref, val_ref, *, n_tiles, d_ff):
    i = pl.program_id(1)
    m = m_ref[0, 0]
    prev_ok = (i >= 2).astype(F32)
    next_ok = jnp.logical_and(i >= 1, i <= n_tiles - 2).astype(F32)
    u = u_ref[0]
    rowi = lax.broadcasted_iota(jnp.int32, (TM, 1), 0)
    um1 = jnp.where(rowi == 0, up_ref[0][7:8] * prev_ok, pltpu.roll(u, 1, 0))
    up1 = jnp.where(rowi == TM - 1, un_ref[0][0:1] * next_ok, pltpu.roll(u, TM - 1, 0))
    cw = cw_ref[...]
    yconv = cb_ref[0] * (cw[0:1] * um1 + cw[1:2] * u + cw[2:3] * up1)

    ones_bd = ones_ref[...]
    y = yf_ref[0] + yb_ref[0]
    mu = _segsum(y, ones_bd) * (1.0 / HEAD)
    dlt = y - mu
    var = _segsum(dlt * dlt, ones_bd) * (1.0 / HEAD)
    yn = dlt * lax.rsqrt(var + GN_EPS) * lng_ref[...] + lnb_ref[...]
    yrw = (yn + bonus_ref[0]) * gate_ref[0]

    out = (jnp.dot(yconv.astype(BF16), wout_ref[0:D_CONV, :], preferred_element_type=F32)
           + jnp.dot(yrw.astype(BF16), wout_ref[D_CONV:D_CONV + D_RWKV, :], preferred_element_type=F32))
    x1 = z_ref[0] + m[2:3] * out
    x1_ref[0] = x1
    h2 = _rms(x1) * g2_ref[...]
    h2 = (h2 * (1.0 + m[4:5]) + m[3:4]).astype(BF16)
    NB = 256
    for n in range(0, d_ff, NB):
        gp_ref[0, :, n:n + NB] = jnp.dot(h2, wup_ref[:, n:n + NB], preferred_element_type=F32)
        val_ref[0, :, n:n + NB] = jnp.dot(h2, wup_ref[:, d_ff + n:d_ff + n + NB],
                                          preferred_element_type=F32)


def _mix(z, modsel, cb, u, yf, yb, bonus, gate, cw, lng, lnb, ones_bd, wout, g2, wup):
    B, T, D = z.shape
    NTL = T // TM
    d_ff = wup.shape[1] // 2
    tok = pl.BlockSpec((1, TM, D_RWKV), lambda b, i: (b, i, 0))
    const = lambda shape: pl.BlockSpec(shape, lambda b, i: (0,) * len(shape))
    R8 = TM // 8
    return pl.pallas_call(
        functools.partial(_mix_kernel, n_tiles=NTL, d_ff=d_ff),
        grid=(B, NTL),
        in_specs=[
            pl.BlockSpec((1, TM, D), lambda b, i: (b, i, 0)),
            pl.BlockSpec((1, 1, 8, D), lambda b, i: (b, jnp.minimum(i, 1), 0, 0)),
            tok, tok,
            pl.BlockSpec((1, 8, D_CONV), lambda b, i: (b, jnp.maximum(i * R8 - 1, 0), 0)),
            pl.BlockSpec((1, 8, D_CONV), lambda b, i: (b, jnp.minimum((i + 1) * R8, T // 8 - 1), 0)),
            tok, tok, tok, tok,
            const((8, D_CONV)), const((1, D_RWKV)), const((1, D_RWKV)), const((D_RWKV, D_RWKV)),
            const((D_CONV + D_RWKV, D)), const((1, D)), const((D, 2 * d_ff)),
        ],
        out_specs=[
            pl.BlockSpec((1, TM, D), lambda b, i: (b, i, 0)),
            pl.BlockSpec((1, TM, d_ff), lambda b, i: (b, i, 0)),
            pl.BlockSpec((1, TM, d_ff), lambda b, i: (b, i, 0)),
        ],
        out_shape=[
            jax.ShapeDtypeStruct((B, T, D), F32),
            jax.ShapeDtypeStruct((B, T, d_ff), F32),
            jax.ShapeDtypeStruct((B, T, d_ff), F32),
        ],
        compiler_params=pltpu.CompilerParams(
            dimension_semantics=("parallel", "parallel"), vmem_limit_bytes=VMEM_LIMIT),
        name="mix",
    )(z, modsel, cb, u, u, u, yf, yb, bonus, gate, cw, lng, lnb, ones_bd, wout, g2, wup)


def _ffn_kernel(x1_ref, m_ref, gp_ref, gt_ref, gb_ref, val_ref, cw_ref, cbias_ref, wdn_ref, fg_ref,
                o_ref, *, n_tiles, d_ff, final):
    i = pl.program_id(1)
    m = m_ref[0, 0]
    lat = (i >= 1).astype(F32)
    top_ok = (i >= 2).astype(F32)
    bot_ok = jnp.logical_and(i >= 1, i <= n_tiles - 2).astype(F32)
    rowi = lax.broadcasted_iota(jnp.int32, (TM, 1), 0)
    colp = jnp.where(i == 0, rowi, rowi & (GRID_W - 1))
    width = jnp.where(i == 0, TM, GRID_W)
    lmask = (colp > 0).astype(F32)
    rmask = (colp < width - 1).astype(F32)

    NB = 256
    acc = jnp.zeros((TM, o_ref.shape[-1]), F32)
    for n in range(0, d_ff, NB):
        g = gp_ref[0, :, n:n + NB]
        gu = jnp.concatenate([gt_ref[0, :, n:n + NB] * top_ok, g[:TM - GRID_W]], axis=0) * lat
        gd = jnp.concatenate([g[GRID_W:], gb_ref[0, :, n:n + NB] * bot_ok], axis=0) * lat
        w = cw_ref[:, n:n + NB]
        h0 = w[0:1] * gu + w[3:4] * g + w[6:7] * gd
        h1 = w[1:2] * gu + w[4:5] * g + w[7:8] * gd
        h2 = w[2:3] * gu + w[5:6] * g + w[8:9] * gd
        conv = (lmask * pltpu.roll(h0, 1, 0) + h1 + rmask * pltpu.roll(h2, TM - 1, 0)
                + cbias_ref[:, n:n + NB])
        act = (_silu(conv) * val_ref[0, :, n:n + NB]).astype(BF16)
        acc = acc + jnp.dot(act, wdn_ref[n:n + NB, :], preferred_element_type=F32)
    x2 = x1_ref[0] + m[5:6] * acc
    if final:
        x2 = _rms(x2) * fg_ref[...]
    o_ref[0] = x2


def _ffn(x1, modsel, gp, val, cw, cbias, wdn, fg, final):
    B, T, D = x1.shape
    NTL = T // TM
    d_ff = gp.shape[-1]
    RW = TM // GRID_W
    const = lambda shape: pl.BlockSpec(shape, lambda b, i: (0,) * len(shape))
    big = pl.BlockSpec((1, TM, d_ff), lambda b, i: (b, i, 0))
    return pl.pallas_call(
        functools.partial(_ffn_kernel, n_tiles=NTL, d_ff=d_ff, final=final),
        grid=(B, NTL),
        in_specs=[
            pl.BlockSpec((1, TM, D), lambda b, i: (b, i, 0)),
            pl.BlockSpec((1, 1, 8, D), lambda b, i: (b, jnp.minimum(i, 1), 0, 0)),
            big,
            pl.BlockSpec((1, GRID_W, d_ff), lambda b, i: (b, jnp.maximum(i * RW - 1, 0), 0)),
            pl.BlockSpec((1, GRID_W, d_ff),
                         lambda b, i: (b, jnp.minimum((i + 1) * RW, T // GRID_W - 1), 0)),
            big,
            const((16, d_ff)), const((1, d_ff)), const((d_ff, D)), const((1, D)),
        ],
        out_specs=pl.BlockSpec((1, TM, D), lambda b, i: (b, i, 0)),
        out_shape=jax.ShapeDtypeStruct((B, T, D), F32),
        compiler_params=pltpu.CompilerParams(
            dimension_semantics=("parallel", "parallel"), vmem_limit_bytes=VMEM_LIMIT),
        name="ffn",
    )(x1, modsel, gp, gp, gp, val, cw, cbias, wdn, fg)


def kernel(x, c, ctx, c_ctx, ada_w, ada_b, norm1_g, norm2_g, w_in, conv_a_w, rw_w0, rw_w_up, rw_a0,
           rw_a_up, rw_k_k, rw_k_a, rw_r_k, rw_g_up, rw_ln_g, rw_ln_b, w_out, ffn_w_up, ffn_conv_w,
           ffn_conv_b, ffn_w_down, final_g):
    B, SEQ, D = x.shape
    CTX = ctx.shape[1]
    L = w_in.shape[0]
    d_ff = ffn_w_down.shape[1]
    assert CTX == TM and SEQ % TM == 0 and TM % GRID_W == 0 and CHUNK == HEAD
    assert w_in.shape[2] == 3 * D_CONV + 3 * D_RWKV + LORA_WA + G_LORA and d_ff % 256 == 0

    rows = -(-(B + 1) // 8) * 8
    c_rows = jnp.zeros((rows, D), F32).at[:B].set(c).at[B].set(c_ctx)
    mod = _ada(c_rows, ada_w, ada_b)

    hw = LORA_WA // 2
    zpad = jnp.zeros((L, 2, hw, D_RWKV), F32)
    wup_ext = jnp.concatenate([rw_w_up, zpad], axis=2).astype(BF16)
    aup_ext = jnp.concatenate([zpad, rw_a_up], axis=2).astype(BF16)
    head_of = jnp.arange(D_RWKV) // HEAD
    ones_bd = (head_of[:, None] == head_of[None, :]).astype(BF16)
    cw_a = jnp.zeros((L, 8, D_CONV), F32).at[:, :3].set(conv_a_w)
    cw_f = jnp.zeros((L, 16, d_ff), F32).at[:, :9].set(ffn_conv_w.reshape(L, 9, d_ff))

    z = jnp.concatenate([ctx, x], axis=1)
    for l in range(L):
        lat = mod[l, :B].reshape(B, 6, D)
        cm = jnp.broadcast_to(mod[l, B].reshape(1, 6, D), (B, 6, D))
        modsel = jnp.zeros((B, 2, 8, D), F32).at[:, 0, :6].set(cm).at[:, 1, :6].set(lat)

        cb, u, r, kkn, v, lw, b, kd, bonus, gate = _proj(
            z, modsel, norm1_g[l].reshape(1, D), w_in[l].astype(BF16), wup_ext[l], aup_ext[l],
            rw_w0[l], rw_a0[l], rw_k_k[l].reshape(1, -1), rw_k_a[l].reshape(1, -1),
            rw_r_k[l].reshape(1, -1), rw_g_up[l].astype(BF16), ones_bd)
        yf, yb = _scan(r, kkn, v, lw, b, kd, CTX)
        x1, gp, val = _mix(
            z, modsel, cb, u, yf, yb, bonus, gate, cw_a[l], rw_ln_g[l].reshape(1, -1),
            rw_ln_b[l].reshape(1, -1), ones_bd, w_out[l].astype(BF16), norm2_g[l].reshape(1, D),
            ffn_w_up[l].astype(BF16))
        z = _ffn(x1, modsel, gp, val, cw_f[l], ffn_conv_b[l].reshape(1, -1),
                 ffn_w_down[l].astype(BF16), final_g.reshape(1, D), final=(l == L - 1))
    return z[:, CTX:]
```

```python
import functools
import math

import jax
import jax.numpy as jnp
from jax import lax
from jax.experimental import pallas as pl
from jax.experimental.pallas import tpu as pltpu

F32 = jnp.float32
BF16 = jnp.bfloat16

HEAD = 64
D_CONV = 512
D_RWKV = 512
LORA_WA = 128
G_LORA = 128
GRID_W = 64
RMS_EPS = 1e-6
GN_EPS = 64e-5
DECAY_SCALE = math.exp(-0.5)

TM = 256
CHUNK = 64
HPG = 2
GROUP = HPG * HEAD
VMEM_LIMIT = 56 * 1024 * 1024

NN = (((1,), (0,)), ((), ()))
NT = (((1,), (1,)), ((), ()))
TN = (((0,), (0,)), ((), ()))


def _silu(x):
    return x * jax.nn.sigmoid(x)


def _split(x):
    hi = x.astype(BF16)
    return hi, (x - hi.astype(F32)).astype(BF16)


def _dg(a, b, dims):
    return lax.dot_general(a, b, dims, preferred_element_type=F32)


def _mm3(a, b_hi, b_lo, dims):
    a_hi, a_lo = _split(a)
    return _dg(a_hi, b_hi, dims) + (_dg(a_hi, b_lo, dims) + _dg(a_lo, b_hi, dims))


def _rms(x):
    return x * lax.rsqrt(jnp.mean(x * x, axis=-1, keepdims=True) + RMS_EPS)


def _segsum(t, ones_bd):
    hi, lo = _split(t)
    return (jnp.dot(hi, ones_bd, preferred_element_type=F32)
            + jnp.dot(lo, ones_bd, preferred_element_type=F32))


def _ada_kernel(c_ref, w_ref, b_ref, o_ref):
    s = _silu(c_ref[...])
    w_hi, w_lo = _split(w_ref[0])
    o_ref[0] = _mm3(s, w_hi, w_lo, NN) + b_ref[0]


def _ada(c_rows, ada_w, ada_b):
    L, D, N = ada_w.shape
    R = c_rows.shape[0]
    NB = 1536
    return pl.pallas_call(
        _ada_kernel,
        grid=(L, N // NB),
        in_specs=[
            pl.BlockSpec((R, D), lambda l, n: (0, 0)),
            pl.BlockSpec((1, D, NB), lambda l, n: (l, 0, n)),
            pl.BlockSpec((1, 1, NB), lambda l, n: (l, 0, n)),
        ],
        out_specs=pl.BlockSpec((1, R, NB), lambda l, n: (l, 0, n)),
        out_shape=jax.ShapeDtypeStruct((L, R, N), F32),
        compiler_params=pltpu.CompilerParams(
            dimension_semantics=("arbitrary", "arbitrary"), vmem_limit_bytes=VMEM_LIMIT),
        name="ada",
    )(c_rows, ada_w, ada_b.reshape(L, 1, N))


def _proj_kernel(z_ref, m_ref, g_ref, win_ref, wup_ref, aup_ref, w0_ref, a0_ref, kk_ref, ka_ref,
                 rk_ref, gup_ref, ones_ref,
                 cb_ref, u_ref, r_ref, kkn_ref, v_ref, lw_ref, b_ref, kd_ref, bonus_ref, gate_ref):
    m = m_ref[0, 0]
    h = _rms(z_ref[0]) * g_ref[...]
    h = (h * (1.0 + m[1:2]) + m[0:1]).astype(BF16)

    def proj(lo, width):
        return jnp.dot(h, win_ref[:, lo:lo + width], preferred_element_type=F32)

    cb_ref[0] = proj(0, D_CONV)
    u_ref[0] = proj(D_CONV, D_CONV) * proj(2 * D_CONV, D_CONV)
    base = 3 * D_CONV
    r = proj(base, D_RWKV)
    k = proj(base + D_RWKV, D_RWKV)
    v = proj(base + 2 * D_RWKV, D_RWKV)
    lora = proj(base + 3 * D_RWKV, LORA_WA + G_LORA)
    wa = lora[:, :LORA_WA]
    gl = lora[:, LORA_WA:]
    ones_bd = ones_ref[...]

    kraw = k * kk_ref[...]
    kkn = kraw * lax.rsqrt(jnp.maximum(_segsum(kraw * kraw, ones_bd), 1e-24))
    r_ref[0] = r
    v_ref[0] = v
    kkn_ref[0] = kkn
    bonus_ref[0] = _segsum(r * k * rk_ref[...], ones_bd) * v
    gate_ref[0] = jnp.dot(jax.nn.sigmoid(gl).astype(BF16), gup_ref[...], preferred_element_type=F32)

    twa = jnp.tanh(wa).astype(BF16)
    wab = wa.astype(BF16)
    for d in range(2):
        lw_ref[d, 0] = -DECAY_SCALE * jax.nn.sigmoid(
            w0_ref[d:d + 1] + jnp.dot(twa, wup_ref[d], preferred_element_type=F32))
        a = jax.nn.sigmoid(a0_ref[d:d + 1] + jnp.dot(wab, aup_ref[d], preferred_element_type=F32))
        b_ref[d, 0] = kkn * a
        kd_ref[d, 0] = k * (1.0 + (a - 1.0) * ka_ref[...])


def _proj(z, modsel, g1, win, wup, aup, w0, a0, k_k, k_a, r_k, gup, ones_bd):
    B, T, D = z.shape
    NTL = T // TM
    P = win.shape[1]
    tok = pl.BlockSpec((1, TM, D_RWKV), lambda b, i: (b, i, 0))
    tok2 = pl.BlockSpec((2, 1, TM, D_RWKV), lambda b, i: (0, b, i, 0))
    const = lambda shape: pl.BlockSpec(shape, lambda b, i: (0,) * len(shape))
    one = jax.ShapeDtypeStruct((B, T, D_RWKV), F32)
    two = jax.ShapeDtypeStruct((2, B, T, D_RWKV), F32)
    return pl.pallas_call(
        _proj_kernel,
        grid=(B, NTL),
        in_specs=[
            pl.BlockSpec((1, TM, D), lambda b, i: (b, i, 0)),
            pl.BlockSpec((1, 1, 8, D), lambda b, i: (b, jnp.minimum(i, 1), 0, 0)),
            const((1, D)), const((D, P)), const((2, LORA_WA, D_RWKV)), const((2, LORA_WA, D_RWKV)),
            const((2, D_RWKV)), const((2, D_RWKV)), const((1, D_RWKV)), const((1, D_RWKV)),
            const((1, D_RWKV)), const((G_LORA, D_RWKV)), const((D_RWKV, D_RWKV)),
        ],
        out_specs=[tok, tok, tok, tok, tok, tok2, tok2, tok2, tok, tok],
        out_shape=[one, one, one, one, one, two, two, two, one, one],
        compiler_params=pltpu.CompilerParams(
            dimension_semantics=("parallel", "parallel"), vmem_limit_bytes=VMEM_LIMIT),
        name="proj",
    )(z, modsel, g1, win, wup, aup, w0, a0, k_k, k_a, r_k, gup, ones_bd)


P_NT, P_INV, P_LOC, P_ST = 1, 1, 1, 1


def _parts(x, passes):
    hi = x.astype(BF16)
    if passes == 1:
        return (hi,)
    return (hi, (x - hi.astype(F32)).astype(BF16))


def _mm(a, xb, dims, split_a=None):
    axis = 1 if dims == TN else 0
    m = a.shape[axis]
    a_hi = a.astype(BF16)
    if split_a is None:
        split_a = len(xb) == 2
    if not split_a:
        return _dg(a_hi, xb[0], dims)
    a_lo = (a - a_hi.astype(F32)).astype(BF16)
    top = _dg(jnp.concatenate([a_hi, a_lo], axis=axis), xb[0], dims)
    out = top[:m] + top[m:]
    if len(xb) == 2:
        out = out + _dg(a_hi, xb[1], dims)
    return out


def _scan_chunks(chains, bdmask):
    C = CHUNK
    row = lax.broadcasted_iota(jnp.int32, (C, GROUP), 0)
    col = lax.broadcasted_iota(jnp.int32, (C, GROUP), 1) & (HEAD - 1)
    eye = (row == col).astype(F32)
    blk16 = (row >> 4) == (col >> 4)
    lane_head = lax.broadcasted_iota(jnp.int32, (HEAD, GROUP), 1) >> 6
    zero_bf = jnp.zeros((), BF16)

    def each(f, *lists):
        return [f(*a) for a in zip(*lists)]

    def bd(x, passes):
        return tuple(jnp.where(bdmask, jnp.concatenate([p] * HPG, axis=0), zero_bf)
                     for p in _parts(x, passes))

    def diag_blocks(full):
        out = jnp.where(lane_head == 0, full[0:HEAD], 0.0)
        for hh in range(1, HPG):
            out = out + jnp.where(lane_head == hh, full[hh * HEAD:(hh + 1) * HEAD], 0.0)
        return out

    rev = [ch["rev"] for ch in chains]
    r, kk, v, lw, b, kd, s0, c = ([ch[n] for ch in chains]
                                  for n in ("r", "kk", "v", "lw", "b", "kd", "s0", "c"))
    strict = [(col > row) if rv else (col < row) for rv in rev]
    incl = [(col >= row) if rv else (col <= row) for rv in rev]
    c_last = [ci[0:1] if rv else ci[C - 1:C] for ci, rv in zip(c, rev)]
    en = each(lambda ci: jnp.exp(-ci), c)
    at = each(lambda k_, ci, lwi: k_ * jnp.exp(ci - lwi), kk, c, lw)
    rt = each(lambda r_, ci: r_ * jnp.exp(ci), r, c)
    eh = each(lambda cl, ci: jnp.exp(cl - ci), c_last, c)
    bh = each(lambda x, e: x * e, b, eh)
    kh = each(lambda x, e: x * e, kd, eh)
    p_c = each(jnp.exp, c_last)

    lhs = each(lambda a_, r_: jnp.concatenate([a_, r_], axis=0), at, rt)
    ab = each(lambda l_, x, e: _mm(l_, bd(x * e, P_NT), NT), lhs, b, en)
    ak = each(lambda l_, x, e: _mm(l_, bd(x * e, P_NT), NT), lhs, kd, en)
    lmat = each(lambda m_, x: jnp.where(m_, x[:C], 0.0), strict, ab)
    aak = each(lambda m_, x: jnp.where(m_, x[:C], 0.0), strict, ak)
    arb = each(lambda m_, x: jnp.where(m_, x[C:], 0.0), incl, ab)
    ark = each(lambda m_, x: jnp.where(m_, x[C:], 0.0), incl, ak)

    ld = each(lambda x: jnp.where(blk16, x, 0.0), lmat)
    lo = each(lambda x, y: x - y, lmat, ld)
    p1 = each(lambda x: eye - x, ld)
    l2 = each(lambda x: _mm(x, bd(x, P_INV), NN), ld)
    t = each(lambda p, x: _mm(jnp.concatenate([p, x], axis=0), bd(x, P_INV), NN), p1, l2)
    p2 = each(lambda p, x: p + x[:C], p1, t)
    l4 = each(lambda x: x[C:], t)
    t = each(lambda p, x: _mm(jnp.concatenate([p, x], axis=0), bd(x, P_INV), NN), p2, l4)
    p3 = each(lambda p, x: p + x[:C], p2, t)
    l8 = each(lambda x: x[C:], t)
    dinv = each(lambda p, x: p + _mm(p, bd(x, P_INV), NN), p3, l8)
    mm = each(lambda d_, x: _mm(d_, bd(x, P_INV), NN), dinv, lo)
    mm2 = each(lambda x: _mm(x, bd(x, P_INV), NN), mm)
    g = each(lambda x, x2: eye - x + x2 - _mm(x, bd(x2, P_INV), NN), mm, mm2)
    tinv = each(lambda g_, d_: _mm(g_, bd(d_, P_INV), NN), g, dinv)

    t = each(lambda x, y, v_: _mm(jnp.concatenate([x, y], axis=0), bd(v_, P_LOC), NN), aak, ark, v)
    akv = each(lambda x: x[:C], t)
    arkv = each(lambda x: x[C:], t)
    wm = each(lambda t_, x: _mm(t_, bd(x, P_LOC), NN), tinv, at)
    um = each(lambda t_, x: _mm(t_, bd(x, P_LOC), NN), tinv, akv)
    qh = each(lambda r_, a_, w_: r_ - _mm(a_, bd(w_, P_LOC), NN), rt, arb, wm)
    yl = each(lambda y_, a_, u_: y_ - _mm(a_, bd(u_, P_LOC), NN), arkv, arb, um)
    ds = each(lambda v_, u_, k_, b_: diag_blocks(_mm(
        jnp.concatenate([v_, u_], axis=0), _parts(jnp.concatenate([k_, -b_], axis=0), P_LOC), TN)),
        v, um, kh, bh)

    wst = each(lambda s_, w_: _mm(s_, bd(w_, P_ST), NT), s0, wm)
    y = each(lambda q_, s_, y_: _mm(q_, bd(s_, P_ST), NT) + y_, qh, s0, yl)
    s_new = each(lambda s_, p_, w_, b_, d_: s_ * p_ - _mm(w_, bd(b_, P_ST), NN) + d_,
                 s0, p_c, wst, bh, ds)
    return list(zip(y, s_new))


def _scan_kernel(rf_ref, kkf_ref, vf_ref, rb_ref, kkb_ref, vb_ref,
                 lwf_ref, bf_ref, kdf_ref, lwb_ref, bb_ref, kdb_ref, yf_ref, yb_ref, s_ref):
    C = CHUNK

    @pl.when(pl.program_id(1) == 0)
    def _():
        s_ref[...] = jnp.zeros_like(s_ref)

    bdmask = (lax.broadcasted_iota(jnp.int32, (GROUP, GROUP), 0) >> 6) == \
             (lax.broadcasted_iota(jnp.int32, (GROUP, GROUP), 1) >> 6)
    rc = lax.broadcasted_iota(jnp.int32, (C, C), 0)
    cc = lax.broadcasted_iota(jnp.int32, (C, C), 1)
    dirs = ((False, rf_ref, kkf_ref, vf_ref, lwf_ref, bf_ref, kdf_ref, yf_ref),
            (True, rb_ref, kkb_ref, vb_ref, lwb_ref, bb_ref, kdb_ref, yb_ref))
    chains, sinks = [], []
    for bi in range(s_ref.shape[0]):
        for d, (rev, r_ref, kk_ref, v_ref, lw_ref, b_ref, kd_ref, y_ref) in enumerate(dirs):
            tri = ((cc >= rc) if rev else (cc <= rc)).astype(BF16)
            lw = lw_ref[0, bi]
            h1 = lw.astype(BF16)
            r1 = lw - h1.astype(F32)
            h2 = r1.astype(BF16)
            h3 = (r1 - h2.astype(F32)).astype(BF16)
            c = (jnp.dot(tri, h1, preferred_element_type=F32)
                 + jnp.dot(tri, h2, preferred_element_type=F32)
                 + jnp.dot(tri, h3, preferred_element_type=F32))
            for q in range(D_RWKV // GROUP):
                sl = slice(q * GROUP, (q + 1) * GROUP)
                chains.append(dict(rev=rev, r=r_ref[bi, :, sl], kk=kk_ref[bi, :, sl],
                                   v=v_ref[bi, :, sl], lw=lw[:, sl], b=b_ref[0, bi, :, sl],
                                   kd=kd_ref[0, bi, :, sl], s0=s_ref[bi, d, :, sl], c=c[:, sl]))
                sinks.append((y_ref, bi, d, sl))
    for (y, s_new), (y_ref, bi, d, sl) in zip(_scan_chunks(chains, bdmask), sinks):
        y_ref[bi, :, sl] = y
        s_ref[bi, d, :, sl] = s_new


def _scan(r, kk, v, lw, b, kd, ctx_len):
    B, T, W = r.shape
    C = CHUNK
    NC = T // C
    nctx = ctx_len // C

    def back(s):
        return jnp.where(s < nctx, nctx - 1 - s, nctx + NC - 1 - s)

    BB = 2 if B % 2 == 0 else 1
    fwd = pl.BlockSpec((BB, C, W), lambda bb, s: (bb, s, 0))
    bwd = pl.BlockSpec((BB, C, W), lambda bb, s: (bb, back(s), 0))
    fwd_d = pl.BlockSpec((1, BB, C, W), lambda bb, s: (0, bb, s, 0))
    bwd_d = pl.BlockSpec((1, BB, C, W), lambda bb, s: (1, bb, back(s), 0))
    out = jax.ShapeDtypeStruct((B, T, W), F32)
    return pl.pallas_call(
        _scan_kernel,
        grid=(B // BB, NC),
        in_specs=[fwd, fwd, fwd, bwd, bwd, bwd, fwd_d, fwd_d, fwd_d, bwd_d, bwd_d, bwd_d],
        out_specs=[fwd, bwd],
        out_shape=[out, out],
        scratch_shapes=[pltpu.VMEM((BB, 2, HEAD, W), F32)],
        compiler_params=pltpu.CompilerParams(
            dimension_semantics=("parallel", "arbitrary"), vmem_limit_bytes=VMEM_LIMIT),
        name="scan",
    )(r, kk, v, r, kk, v, lw, b, kd, lw, b, kd)


def _mix_kernel(z_ref, m_ref, cb_ref, u_ref, up_ref, un_ref, yf_ref, yb_ref, bonus_ref, gate_ref,
                cw_ref, lng_ref, lnb_ref, ones_ref, wout_ref, g2_ref, wup_ref,
                x1_ref, gp_ref, val_ref, *, n_tiles, d_ff):
    i = pl.program_id(1)
    m = m_ref[0, 0]
    prev_ok = (i >= 2).astype(F32)
    next_ok = jnp.logical_and(i >= 1, i <= n_tiles - 2).astype(F32)
    u = u_ref[0]
    rowi = lax.broadcasted_iota(jnp.int32, (TM, 1), 0)
    um1 = jnp.where(rowi == 0, up_ref[0][7:8] * prev_ok, pltpu.roll(u, 1, 0))
    up1 = jnp.where(rowi == TM - 1, un_ref[0][0:1] * next_ok, pltpu.roll(u, TM - 1, 0))
    cw = cw_ref[...]
    yconv = cb_ref[0] * (cw[0:1] * um1 + cw[1:2] * u + cw[2:3] * up1)

    ones_bd = ones_ref[...]
    y = yf_ref[0] + yb_ref[0]
    mu = _segsum(y, ones_bd) * (1.0 / HEAD)
    dlt = y - mu
    var = _segsum(dlt * dlt, ones_bd) * (1.0 / HEAD)
    yn = dlt * lax.rsqrt(var + GN_EPS) * lng_ref[...] + lnb_ref[...]
    yrw = (yn + bonus_ref[0]) * gate_ref[0]

    out = (jnp.dot(yconv.astype(BF16), wout_ref[0:D_CONV, :], preferred_element_type=F32)
           + jnp.dot(yrw.astype(BF16), wout_ref[D_CONV:D_CONV + D_RWKV, :], preferred_element_type=F32))
    x1 = z_ref[0] + m[2:3] * out
    x1_ref[0] = x1
    h2 = _rms(x1) * g2_ref[...]
    h2 = (h2 * (1.0 + m[4:5]) + m[3:4]).astype(BF16)
    NB = 256
    for n in range(0, d_ff, NB):
        gp_ref[0, :, n:n + NB] = jnp.dot(h2, wup_ref[:, n:n + NB], preferred_element_type=F32)
        val_ref[0, :, n:n + NB] = jnp.dot(h2, wup_ref[:, d_ff + n:d_ff + n + NB],
                                          preferred_element_type=F32)


def _mix(z, modsel, cb, u, yf, yb, bonus, gate, cw, lng, lnb, ones_bd, wout, g2, wup):
    B, T, D = z.shape
    NTL = T // TM
    d_ff = wup.shape[1] // 2
    tok = pl.BlockSpec((1, TM, D_RWKV), lambda b, i: (b, i, 0))
    const = lambda shape: pl.BlockSpec(shape, lambda b, i: (0,) * len(shape))
    R8 = TM // 8
    return pl.pallas_call(
        functools.partial(_mix_kernel, n_tiles=NTL, d_ff=d_ff),
        grid=(B, NTL),
        in_specs=[
            pl.BlockSpec((1, TM, D), lambda b, i: (b, i, 0)),
            pl.BlockSpec((1, 1, 8, D), lambda b, i: (b, jnp.minimum(i, 1), 0, 0)),
            tok, tok,
            pl.BlockSpec((1, 8, D_CONV), lambda b, i: (b, jnp.maximum(i * R8 - 1, 0), 0)),
            pl.BlockSpec((1, 8, D_CONV), lambda b, i: (b, jnp.minimum((i + 1) * R8, T // 8 - 1), 0)),
            tok, tok, tok, tok,
            const((8, D_CONV)), const((1, D_RWKV)), const((1, D_RWKV)), const((D_RWKV, D_RWKV)),
            const((D_CONV + D_RWKV, D)), const((1, D)), const((D, 2 * d_ff)),
        ],
        out_specs=[
            pl.BlockSpec((1, TM, D), lambda b, i: (b, i, 0)),
            pl.BlockSpec((1, TM, d_ff), lambda b, i: (b, i, 0)),
            pl.BlockSpec((1, TM, d_ff), lambda b, i: (b, i, 0)),
        ],
        out_shape=[
            jax.ShapeDtypeStruct((B, T, D), F32),
            jax.ShapeDtypeStruct((B, T, d_ff), F32),
            jax.ShapeDtypeStruct((B, T, d_ff), F32),
        ],
        compiler_params=pltpu.CompilerParams(
            dimension_semantics=("parallel", "parallel"), vmem_limit_bytes=VMEM_LIMIT),
        name="mix",
    )(z, modsel, cb, u, u, u, yf, yb, bonus, gate, cw, lng, lnb, ones_bd, wout, g2, wup)


def _ffn_kernel(x1_ref, m_ref, gp_ref, gt_ref, gb_ref, val_ref, cw_ref, cbias_ref, wdn_ref, fg_ref,
                o_ref, *, n_tiles, d_ff, final):
    i = pl.program_id(1)
    m = m_ref[0, 0]
    lat = (i >= 1).astype(F32)
    top_ok = (i >= 2).astype(F32)
    bot_ok = jnp.logical_and(i >= 1, i <= n_tiles - 2).astype(F32)
    rowi = lax.broadcasted_iota(jnp.int32, (TM, 1), 0)
    colp = jnp.where(i == 0, rowi, rowi & (GRID_W - 1))
    width = jnp.where(i == 0, TM, GRID_W)
    lmask = (colp > 0).astype(F32)
    rmask = (colp < width - 1).astype(F32)

    NB = 256
    acc = jnp.zeros((TM, o_ref.shape[-1]), F32)
    for n in range(0, d_ff, NB):
        g = gp_ref[0, :, n:n + NB]
        gu = jnp.concatenate([gt_ref[0, :, n:n + NB] * top_ok, g[:TM - GRID_W]], axis=0) * lat
        gd = jnp.concatenate([g[GRID_W:], gb_ref[0, :, n:n + NB] * bot_ok], axis=0) * lat
        w = cw_ref[:, n:n + NB]
        h0 = w[0:1] * gu + w[3:4] * g + w[6:7] * gd
        h1 = w[1:2] * gu + w[4:5] * g + w[7:8] * gd
        h2 = w[2:3] * gu + w[5:6] * g + w[8:9] * gd
        conv = (lmask * pltpu.roll(h0, 1, 0) + h1 + rmask * pltpu.roll(h2, TM - 1, 0)
                + cbias_ref[:, n:n + NB])
        act = (_silu(conv) * val_ref[0, :, n:n + NB]).astype(BF16)
        acc = acc + jnp.dot(act, wdn_ref[n:n + NB, :], preferred_element_type=F32)
    x2 = x1_ref[0] + m[5:6] * acc
    if final:
        x2 = _rms(x2) * fg_ref[...]
    o_ref[0] = x2


def _ffn(x1, modsel, gp, val, cw, cbias, wdn, fg, final):
    B, T, D = x1.shape
    NTL = T // TM
    d_ff = gp.shape[-1]
    RW = TM // GRID_W
    const = lambda shape: pl.BlockSpec(shape, lambda b, i: (0,) * len(shape))
    big = pl.BlockSpec((1, TM, d_ff), lambda b, i: (b, i, 0))
    return pl.pallas_call(
        functools.partial(_ffn_kernel, n_tiles=NTL, d_ff=d_ff, final=final),
        grid=(B, NTL),
        in_specs=[
            pl.BlockSpec((1, TM, D), lambda b, i: (b, i, 0)),
            pl.BlockSpec((1, 1, 8, D), lambda b, i: (b, jnp.minimum(i, 1), 0, 0)),
            big,
            pl.BlockSpec((1, GRID_W, d_ff), lambda b, i: (b, jnp.maximum(i * RW - 1, 0), 0)),
            pl.BlockSpec((1, GRID_W, d_ff),
                         lambda b, i: (b, jnp.minimum((i + 1) * RW, T // GRID_W - 1), 0)),
            big,
            const((16, d_ff)), const((1, d_ff)), const((d_ff, D)), const((1, D)),
        ],
        out_specs=pl.BlockSpec((1, TM, D), lambda b, i: (b, i, 0)),
        out_shape=jax.ShapeDtypeStruct((B, T, D), F32),
        compiler_params=pltpu.CompilerParams(
            dimension_semantics=("parallel", "parallel"), vmem_limit_bytes=VMEM_LIMIT),
        name="ffn",
    )(x1, modsel, gp, gp, gp, val, cw, cbias, wdn, fg)


def kernel(x, c, ctx, c_ctx, ada_w, ada_b, norm1_g, norm2_g, w_in, conv_a_w, rw_w0, rw_w_up, rw_a0,
           rw_a_up, rw_k_k, rw_k_a, rw_r_k, rw_g_up, rw_ln_g, rw_ln_b, w_out, ffn_w_up, ffn_conv_w,
           ffn_conv_b, ffn_w_down, final_g):
    B, SEQ, D = x.shape
    CTX = ctx.shape[1]
    L = w_in.shape[0]
    d_ff = ffn_w_down.shape[1]
    assert CTX == TM and SEQ % TM == 0 and TM % GRID_W == 0 and CHUNK == HEAD
    assert w_in.shape[2] == 3 * D_CONV + 3 * D_RWKV + LORA_WA + G_LORA and d_ff % 256 == 0

    rows = -(-(B + 1) // 8) * 8
    c_rows = jnp.zeros((rows, D), F32).at[:B].set(c).at[B].set(c_ctx)
    mod = _ada(c_rows, ada_w, ada_b)

    hw = LORA_WA // 2
    zpad = jnp.zeros((L, 2, hw, D_RWKV), F32)
    wup_ext = jnp.concatenate([rw_w_up, zpad], axis=2).astype(BF16)
    aup_ext = jnp.concatenate([zpad, rw_a_up], axis=2).astype(BF16)
    head_of = jnp.arange(D_RWKV) // HEAD
    ones_bd = (head_of[:, None] == head_of[None, :]).astype(BF16)
    cw_a = jnp.zeros((L, 8, D_CONV), F32).at[:, :3].set(conv_a_w)
    cw_f = jnp.zeros((L, 16, d_ff), F32).at[:, :9].set(ffn_conv_w.reshape(L, 9, d_ff))

    z = jnp.concatenate([ctx, x], axis=1)
    for l in range(L):
        lat = mod[l, :B].reshape(B, 6, D)
        cm = jnp.broadcast_to(mod[l, B].reshape(1, 6, D), (B, 6, D))
        modsel = jnp.zeros((B, 2, 8, D), F32).at[:, 0, :6].set(cm).at[:, 1, :6].set(lat)

        cb, u, r, kkn, v, lw, b, kd, bonus, gate = _proj(
            z, modsel, norm1_g[l].reshape(1, D), w_in[l].astype(BF16), wup_ext[l], aup_ext[l],
            rw_w0[l], rw_a0[l], rw_k_k[l].reshape(1, -1), rw_k_a[l].reshape(1, -1),
            rw_r_k[l].reshape(1, -1), rw_g_up[l].astype(BF16), ones_bd)
        yf, yb = _scan(r, kkn, v, lw, b, kd, CTX)
        x1, gp, val = _mix(
            z, modsel, cb, u, yf, yb, bonus, gate, cw_a[l], rw_ln_g[l].reshape(1, -1),
            rw_ln_b[l].reshape(1, -1), ones_bd, w_out[l].astype(BF16), norm2_g[l].reshape(1, D),
            ffn_w_up[l].astype(BF16))
        z = _ffn(x1, modsel, gp, val, cw_f[l], ffn_conv_b[l].reshape(1, -1),
                 ffn_w_down[l].astype(BF16), final_g.reshape(1, D), final=(l == L - 1))
    return z[:, CTX:]
```

```python
import functools
import math

import jax
import jax.numpy as jnp
from jax import lax
from jax.experimental import pallas as pl
from jax.experimental.pallas import tpu as pltpu

F32 = jnp.float32
BF16 = jnp.bfloat16

HEAD = 64
D_CONV = 512
D_RWKV = 512
LORA_WA = 128
G_LORA = 128
GRID_W = 64
RMS_EPS = 1e-6
GN_EPS = 64e-5
DECAY_SCALE = math.exp(-0.5)

TM = 256
CHUNK = 64
HPG = 2
GROUP = HPG * HEAD
VMEM_LIMIT = 56 * 1024 * 1024

NN = (((1,), (0,)), ((), ()))
NT = (((1,), (1,)), ((), ()))
TN = (((0,), (0,)), ((), ()))


def _silu(x):
    return x * jax.nn.sigmoid(x)


def _split(x):
    hi = x.astype(BF16)
    return hi, (x - hi.astype(F32)).astype(BF16)


def _dg(a, b, dims):
    return lax.dot_general(a, b, dims, preferred_element_type=F32)


def _mm3(a, b_hi, b_lo, dims):
    a_hi, a_lo = _split(a)
    return _dg(a_hi, b_hi, dims) + (_dg(a_hi, b_lo, dims) + _dg(a_lo, b_hi, dims))


def _rms(x):
    return x * lax.rsqrt(jnp.mean(x * x, axis=-1, keepdims=True) + RMS_EPS)


def _segsum(t, ones_bd):
    hi, lo = _split(t)
    return (jnp.dot(hi, ones_bd, preferred_element_type=F32)
            + jnp.dot(lo, ones_bd, preferred_element_type=F32))


def _ada_kernel(c_ref, w_ref, b_ref, o_ref):
    s = _silu(c_ref[...])
    w_hi, w_lo = _split(w_ref[0])
    o_ref[0] = _mm3(s, w_hi, w_lo, NN) + b_ref[0]


def _ada(c_rows, ada_w, ada_b):
    L, D, N = ada_w.shape
    R = c_rows.shape[0]
    NB = 1536
    return pl.pallas_call(
        _ada_kernel,
        grid=(L, N // NB),
        in_specs=[
            pl.BlockSpec((R, D), lambda l, n: (0, 0)),
            pl.BlockSpec((1, D, NB), lambda l, n: (l, 0, n)),
            pl.BlockSpec((1, 1, NB), lambda l, n: (l, 0, n)),
        ],
        out_specs=pl.BlockSpec((1, R, NB), lambda l, n: (l, 0, n)),
        out_shape=jax.ShapeDtypeStruct((L, R, N), F32),
        compiler_params=pltpu.CompilerParams(
            dimension_semantics=("arbitrary", "arbitrary"), vmem_limit_bytes=VMEM_LIMIT),
        name="ada",
    )(c_rows, ada_w, ada_b.reshape(L, 1, N))


def _proj_kernel(z_ref, m_ref, g_ref, win_ref, wup_ref, aup_ref, w0_ref, a0_ref, kk_ref, ka_ref,
                 rk_ref, gup_ref, ones_ref,
                 cb_ref, u_ref, r_ref, kkn_ref, v_ref, lw_ref, b_ref, kd_ref, bonus_ref, gate_ref):
    m = m_ref[0, 0]
    h = _rms(z_ref[0]) * g_ref[...]
    h = (h * (1.0 + m[1:2]) + m[0:1]).astype(BF16)

    def proj(lo, width):
        return jnp.dot(h, win_ref[:, lo:lo + width], preferred_element_type=F32)

    cb_ref[0] = proj(0, D_CONV)
    u_ref[0] = proj(D_CONV, D_CONV) * proj(2 * D_CONV, D_CONV)
    base = 3 * D_CONV
    r = proj(base, D_RWKV)
    k = proj(base + D_RWKV, D_RWKV)
    v = proj(base + 2 * D_RWKV, D_RWKV)
    lora = proj(base + 3 * D_RWKV, LORA_WA + G_LORA)
    wa = lora[:, :LORA_WA]
    gl = lora[:, LORA_WA:]
    ones_bd = ones_ref[...]

    kraw = k * kk_ref[...]
    kkn = kraw * lax.rsqrt(jnp.maximum(_segsum(kraw * kraw, ones_bd), 1e-24))
    r_ref[0] = r
    v_ref[0] = v
    kkn_ref[0] = kkn
    bonus_ref[0] = _segsum(r * k * rk_ref[...], ones_bd) * v
    gate_ref[0] = jnp.dot(jax.nn.sigmoid(gl).astype(BF16), gup_ref[...], preferred_element_type=F32)

    twa = jnp.tanh(wa).astype(BF16)
    wab = wa.astype(BF16)
    for d in range(2):
        lw_ref[d, 0] = -DECAY_SCALE * jax.nn.sigmoid(
            w0_ref[d:d + 1] + jnp.dot(twa, wup_ref[d], preferred_element_type=F32))
        a = jax.nn.sigmoid(a0_ref[d:d + 1] + jnp.dot(wab, aup_ref[d], preferred_element_type=F32))
        b_ref[d, 0] = kkn * a
        kd_ref[d, 0] = k * (1.0 + (a - 1.0) * ka_ref[...])


def _proj(z, modsel, g1, win, wup, aup, w0, a0, k_k, k_a, r_k, gup, ones_bd):
    B, T, D = z.shape
    NTL = T // TM
    P = win.shape[1]
    tok = pl.BlockSpec((1, TM, D_RWKV), lambda b, i: (b, i, 0))
    tok2 = pl.BlockSpec((2, 1, TM, D_RWKV), lambda b, i: (0, b, i, 0))
    const = lambda shape: pl.BlockSpec(shape, lambda b, i: (0,) * len(shape))
    one = jax.ShapeDtypeStruct((B, T, D_RWKV), F32)
    two = jax.ShapeDtypeStruct((2, B, T, D_RWKV), F32)
    return pl.pallas_call(
        _proj_kernel,
        grid=(B, NTL),
        in_specs=[
            pl.BlockSpec((1, TM, D), lambda b, i: (b, i, 0)),
            pl.BlockSpec((1, 1, 8, D), lambda b, i: (b, jnp.minimum(i, 1), 0, 0)),
            const((1, D)), const((D, P)), const((2, LORA_WA, D_RWKV)), const((2, LORA_WA, D_RWKV)),
            const((2, D_RWKV)), const((2, D_RWKV)), const((1, D_RWKV)), const((1, D_RWKV)),
            const((1, D_RWKV)), const((G_LORA, D_RWKV)), const((D_RWKV, D_RWKV)),
        ],
        out_specs=[tok, tok, tok, tok, tok, tok2, tok2, tok2, tok, tok],
        out_shape=[one, one, one, one, one, two, two, two, one, one],
        compiler_params=pltpu.CompilerParams(
            dimension_semantics=("parallel", "parallel"), vmem_limit_bytes=VMEM_LIMIT),
        name="proj",
    )(z, modsel, g1, win, wup, aup, w0, a0, k_k, k_a, r_k, gup, ones_bd)


P_NT, P_INV, P_LOC, P_ST = 1, 1, 1, 1


def _parts(x, passes):
    hi = x.astype(BF16)
    if passes == 1:
        return (hi,)
    return (hi, (x - hi.astype(F32)).astype(BF16))


def _mm(a, xb, dims, split_a=None):
    axis = 1 if dims == TN else 0
    m = a.shape[axis]
    a_hi = a.astype(BF16)
    if split_a is None:
        split_a = len(xb) == 2
    if not split_a:
        return _dg(a_hi, xb[0], dims)
    a_lo = (a - a_hi.astype(F32)).astype(BF16)
    top = _dg(jnp.concatenate([a_hi, a_lo], axis=axis), xb[0], dims)
    out = top[:m] + top[m:]
    if len(xb) == 2:
        out = out + _dg(a_hi, xb[1], dims)
    return out


def _scan_chunks(chains, bdmask):
    C = CHUNK
    row = lax.broadcasted_iota(jnp.int32, (C, GROUP), 0)
    col = lax.broadcasted_iota(jnp.int32, (C, GROUP), 1) & (HEAD - 1)
    eye = (row == col).astype(F32)
    blk16 = (row >> 4) == (col >> 4)
    lane_head = lax.broadcasted_iota(jnp.int32, (HEAD, GROUP), 1) >> 6
    zero_bf = jnp.zeros((), BF16)

    def each(f, *lists):
        return [f(*a) for a in zip(*lists)]

    def bd(x, passes):
        return tuple(jnp.where(bdmask, jnp.concatenate([p] * HPG, axis=0), zero_bf)
                     for p in _parts(x, passes))

    def diag_blocks(full):
        out = jnp.where(lane_head == 0, full[0:HEAD], 0.0)
        for hh in range(1, HPG):
            out = out + jnp.where(lane_head == hh, full[hh * HEAD:(hh + 1) * HEAD], 0.0)
        return out

    rev = [ch["rev"] for ch in chains]
    r, kk, v, lw, b, kd, s0, c = ([ch[n] for ch in chains]
                                  for n in ("r", "kk", "v", "lw", "b", "kd", "s0", "c"))
    strict = [(col > row) if rv else (col < row) for rv in rev]
    incl = [(col >= row) if rv else (col <= row) for rv in rev]
    c_last = [ci[0:1] if rv else ci[C - 1:C] for ci, rv in zip(c, rev)]
    en = each(lambda ci: jnp.exp(-ci), c)
    at = each(lambda k_, ci, lwi: k_ * jnp.exp(ci - lwi), kk, c, lw)
    rt = each(lambda r_, ci: r_ * jnp.exp(ci), r, c)
    eh = each(lambda cl, ci: jnp.exp(cl - ci), c_last, c)
    bh = each(lambda x, e: x * e, b, eh)
    kh = each(lambda x, e: x * e, kd, eh)
    p_c = each(jnp.exp, c_last)

    lhs = each(lambda a_, r_: jnp.concatenate([a_, r_], axis=0), at, rt)
    ab = each(lambda l_, x, e: _mm(l_, bd(x * e, P_NT), NT), lhs, b, en)
    ak = each(lambda l_, x, e: _mm(l_, bd(x * e, P_NT), NT), lhs, kd, en)
    lmat = each(lambda m_, x: jnp.where(m_, x[:C], 0.0), strict, ab)
    aak = each(lambda m_, x: jnp.where(m_, x[:C], 0.0), strict, ak)
    arb = each(lambda m_, x: jnp.where(m_, x[C:], 0.0), incl, ab)
    ark = each(lambda m_, x: jnp.where(m_, x[C:], 0.0), incl, ak)

    ld = each(lambda x: jnp.where(blk16, x, 0.0), lmat)
    lo = each(lambda x, y: x - y, lmat, ld)
    p1 = each(lambda x: eye - x, ld)
    l2 = each(lambda x: _mm(x, bd(x, P_INV), NN), ld)
    t = each(lambda p, x: _mm(jnp.concatenate([p, x], axis=0), bd(x, P_INV), NN), p1, l2)
    p2 = each(lambda p, x: p + x[:C], p1, t)
    l4 = each(lambda x: x[C:], t)
    t = each(lambda p, x: _mm(jnp.concatenate([p, x], axis=0), bd(x, P_INV), NN), p2, l4)
    p3 = each(lambda p, x: p + x[:C], p2, t)
    l8 = each(lambda x: x[C:], t)
    dinv = each(lambda p, x: p + _mm(p, bd(x, P_INV), NN), p3, l8)
    mm = each(lambda d_, x: _mm(d_, bd(x, P_INV), NN), dinv, lo)
    mm2 = each(lambda x: _mm(x, bd(x, P_INV), NN), mm)
    g = each(lambda x, x2: eye - x + x2 - _mm(x, bd(x2, P_INV), NN), mm, mm2)
    tinv = each(lambda g_, d_: _mm(g_, bd(d_, P_INV), NN), g, dinv)

    t = each(lambda x, y, v_: _mm(jnp.concatenate([x, y], axis=0), bd(v_, P_LOC), NN), aak, ark, v)
    akv = each(lambda x: x[:C], t)
    arkv = each(lambda x: x[C:], t)
    wm = each(lambda t_, x: _mm(t_, bd(x, P_LOC), NN), tinv, at)
    um = each(lambda t_, x: _mm(t_, bd(x, P_LOC), NN), tinv, akv)
    qh = each(lambda r_, a_, w_: r_ - _mm(a_, bd(w_, P_LOC), NN), rt, arb, wm)
    yl = each(lambda y_, a_, u_: y_ - _mm(a_, bd(u_, P_LOC), NN), arkv, arb, um)
    ds = each(lambda v_, u_, k_, b_: diag_blocks(_mm(
        jnp.concatenate([v_, u_], axis=0), _parts(jnp.concatenate([k_, -b_], axis=0), P_LOC), TN)),
        v, um, kh, bh)

    wst = each(lambda s_, w_: _mm(s_, bd(w_, P_ST), NT), s0, wm)
    y = each(lambda q_, s_, y_: _mm(q_, bd(s_, P_ST), NT) + y_, qh, s0, yl)
    s_new = each(lambda s_, p_, w_, b_, d_: s_ * p_ - _mm(w_, bd(b_, P_ST), NN) + d_,
                 s0, p_c, wst, bh, ds)
    return list(zip(y, s_new))


def _scan_kernel(rf_ref, kkf_ref, vf_ref, rb_ref, kkb_ref, vb_ref,
                 lwf_ref, bf_ref, kdf_ref, lwb_ref, bb_ref, kdb_ref, yf_ref, yb_ref, s_ref):
    C = CHUNK

    @pl.when(pl.program_id(1) == 0)
    def _():
        s_ref[...] = jnp.zeros_like(s_ref)

    bdmask = (lax.broadcasted_iota(jnp.int32, (GROUP, GROUP), 0) >> 6) == \
             (lax.broadcasted_iota(jnp.int32, (GROUP, GROUP), 1) >> 6)
    rc = lax.broadcasted_iota(jnp.int32, (C, C), 0)
    cc = lax.broadcasted_iota(jnp.int32, (C, C), 1)
    dirs = ((False, rf_ref, kkf_ref, vf_ref, lwf_ref, bf_ref, kdf_ref, yf_ref),
            (True, rb_ref, kkb_ref, vb_ref, lwb_ref, bb_ref, kdb_ref, yb_ref))
    chains, sinks = [], []
    for bi in range(s_ref.shape[0]):
        for d, (rev, r_ref, kk_ref, v_ref, lw_ref, b_ref, kd_ref, y_ref) in enumerate(dirs):
            tri = ((cc >= rc) if rev else (cc <= rc)).astype(BF16)
            lw = lw_ref[0, bi]
            h1 = lw.astype(BF16)
            r1 = lw - h1.astype(F32)
            h2 = r1.astype(BF16)
            h3 = (r1 - h2.astype(F32)).astype(BF16)
            c = (jnp.dot(tri, h1, preferred_element_type=F32)
                 + jnp.dot(tri, h2, preferred_element_type=F32)
                 + jnp.dot(tri, h3, preferred_element_type=F32))
            for q in range(D_RWKV // GROUP):
                sl = slice(q * GROUP, (q + 1) * GROUP)
                chains.append(dict(rev=rev, r=r_ref[bi, :, sl], kk=kk_ref[bi, :, sl],
                                   v=v_ref[bi, :, sl], lw=lw[:, sl], b=b_ref[0, bi, :, sl],
                                   kd=kd_ref[0, bi, :, sl], s0=s_ref[bi, d, :, sl], c=c[:, sl]))
                sinks.append((y_ref, bi, d, sl))
    for (y, s_new), (y_ref, bi, d, sl) in zip(_scan_chunks(chains, bdmask), sinks):
        y_ref[bi, :, sl] = y
        s_ref[bi, d, :, sl] = s_new


def _scan(r, kk, v, lw, b, kd, ctx_len):
    B, T, W = r.shape
    C = CHUNK
    NC = T // C
    nctx = ctx_len // C

    def back(s):
        return jnp.where(s < nctx, nctx - 1 - s, nctx + NC - 1 - s)

    BB = 2 if B % 2 == 0 else 1
    fwd = pl.BlockSpec((BB, C, W), lambda bb, s: (bb, s, 0))
    bwd = pl.BlockSpec((BB, C, W), lambda bb, s: (bb, back(s), 0))
    fwd_d = pl.BlockSpec((1, BB, C, W), lambda bb, s: (0, bb, s, 0))
    bwd_d = pl.BlockSpec((1, BB, C, W), lambda bb, s: (1, bb, back(s), 0))
    out = jax.ShapeDtypeStruct((B, T, W), F32)
    return pl.pallas_call(
        _scan_kernel,
        grid=(B // BB, NC),
        in_specs=[fwd, fwd, fwd, bwd, bwd, bwd, fwd_d, fwd_d, fwd_d, bwd_d, bwd_d, bwd_d],
        out_specs=[fwd, bwd],
        out_shape=[out, out],
        scratch_shapes=[pltpu.VMEM((BB, 2, HEAD, W), F32)],
        compiler_params=pltpu.CompilerParams(
            dimension_semantics=("parallel", "arbitrary"), vmem_limit_bytes=VMEM_LIMIT),
        name="scan",
    )(r, kk, v, r, kk, v, lw, b, kd, lw, b, kd)


def _mix_kernel(z_ref, m_ref, cb_ref, u_ref, up_ref, un_ref, yf_ref, yb_ref, bonus_ref, gate_ref,
                cw_ref, lng_ref, lnb_ref, ones_ref, wout_ref, x1_ref, *, n_tiles):
    i = pl.program_id(1)
    m = m_ref[0, 0]
    prev_ok = (i >= 2).astype(F32)
    next_ok = jnp.logical_and(i >= 1, i <= n_tiles - 2).astype(F32)
    u = u_ref[0]
    rowi = lax.broadcasted_iota(jnp.int32, (TM, 1), 0)
    um1 = jnp.where(rowi == 0, up_ref[0][7:8] * prev_ok, pltpu.roll(u, 1, 0))
    up1 = jnp.where(rowi == TM - 1, un_ref[0][0:1] * next_ok, pltpu.roll(u, TM - 1, 0))
    cw = cw_ref[...]
    yconv = cb_ref[0] * (cw[0:1] * um1 + cw[1:2] * u + cw[2:3] * up1)

    ones_bd = ones_ref[...]
    y = yf_ref[0] + yb_ref[0]
    mu = _segsum(y, ones_bd) * (1.0 / HEAD)
    dlt = y - mu
    var = _segsum(dlt * dlt, ones_bd) * (1.0 / HEAD)
    yn = dlt * lax.rsqrt(var + GN_EPS) * lng_ref[...] + lnb_ref[...]
    yrw = (yn + bonus_ref[0]) * gate_ref[0]

    out = (jnp.dot(yconv.astype(BF16), wout_ref[0:D_CONV, :], preferred_element_type=F32)
           + jnp.dot(yrw.astype(BF16), wout_ref[D_CONV:D_CONV + D_RWKV, :], preferred_element_type=F32))
    x1_ref[0] = z_ref[0] + m[2:3] * out


def _mix(z, modsel, cb, u, yf, yb, bonus, gate, cw, lng, lnb, ones_bd, wout):
    B, T, D = z.shape
    NTL = T // TM
    tok = pl.BlockSpec((1, TM, D_RWKV), lambda b, i: (b, i, 0))
    const = lambda shape: pl.BlockSpec(shape, lambda b, i: (0,) * len(shape))
    R8 = TM // 8
    return pl.pallas_call(
        functools.partial(_mix_kernel, n_tiles=NTL),
        grid=(B, NTL),
        in_specs=[
            pl.BlockSpec((1, TM, D), lambda b, i: (b, i, 0)),
            pl.BlockSpec((1, 1, 8, D), lambda b, i: (b, jnp.minimum(i, 1), 0, 0)),
            tok, tok,
            pl.BlockSpec((1, 8, D_CONV), lambda b, i: (b, jnp.maximum(i * R8 - 1, 0), 0)),
            pl.BlockSpec((1, 8, D_CONV), lambda b, i: (b, jnp.minimum((i + 1) * R8, T // 8 - 1), 0)),
            tok, tok, tok, tok,
            const((8, D_CONV)), const((1, D_RWKV)), const((1, D_RWKV)), const((D_RWKV, D_RWKV)),
            const((D_CONV + D_RWKV, D)),
        ],
        out_specs=pl.BlockSpec((1, TM, D), lambda b, i: (b, i, 0)),
        out_shape=jax.ShapeDtypeStruct((B, T, D), F32),
        compiler_params=pltpu.CompilerParams(
            dimension_semantics=("parallel", "parallel"), vmem_limit_bytes=VMEM_LIMIT),
        name="mix",
    )(z, modsel, cb, u, u, u, yf, yb, bonus, gate, cw, lng, lnb, ones_bd, wout)


def _ffn_kernel(x1_ref, xt_ref, xb_ref, m_ref, g2_ref, wup_ref, cw_ref, cbias_ref, wdn_ref, fg_ref,
                o_ref, *, n_tiles, d_ff, final):
    i = pl.program_id(1)
    m = m_ref[0, 0]
    lat = (i >= 1).astype(F32)
    top_ok = (i >= 2).astype(F32)
    bot_ok = jnp.logical_and(i >= 1, i <= n_tiles - 2).astype(F32)
    rowi = lax.broadcasted_iota(jnp.int32, (TM, 1), 0)
    colp = jnp.where(i == 0, rowi, rowi & (GRID_W - 1))
    width = jnp.where(i == 0, TM, GRID_W)
    lmask = (colp > 0).astype(F32)
    rmask = (colp < width - 1).astype(F32)

    def norm2(x, keep):
        h = _rms(x) * g2_ref[...]
        return ((h * (1.0 + m[4:5]) + m[3:4]) * keep).astype(BF16)

    x1 = x1_ref[0]
    hx = jnp.concatenate([norm2(xt_ref[0], top_ok), norm2(x1, 1.0), norm2(xb_ref[0], bot_ok)], axis=0)
    hc = hx[GRID_W:GRID_W + TM]

    NB = 256

    def up(n):
        return (jnp.dot(hx, wup_ref[:, n:n + NB], preferred_element_type=F32),
                jnp.dot(hc, wup_ref[:, d_ff + n:d_ff + n + NB], preferred_element_type=F32))

    acc = jnp.zeros((TM, o_ref.shape[-1]), F32)
    nxt = up(0)
    prev = None
    for n in range(0, d_ff, NB):
        gfull, val = nxt
        if n + NB < d_ff:
            nxt = up(n + NB)
        if prev is not None:
            acc = acc + jnp.dot(prev, wdn_ref[n - NB:n, :], preferred_element_type=F32)
        gu, g, gd = gfull[0:TM], gfull[GRID_W:GRID_W + TM], gfull[2 * GRID_W:2 * GRID_W + TM]
        w = cw_ref[:, n:n + NB]
        wv = w * lat
        h0 = wv[0:1] * gu + w[3:4] * g + wv[6:7] * gd
        h1 = wv[1:2] * gu + w[4:5] * g + wv[7:8] * gd
        h2 = wv[2:3] * gu + w[5:6] * g + wv[8:9] * gd
        conv = (lmask * pltpu.roll(h0, 1, 0) + h1 + rmask * pltpu.roll(h2, TM - 1, 0)
                + cbias_ref[:, n:n + NB])
        prev = (_silu(conv) * val).astype(BF16)
    acc = acc + jnp.dot(prev, wdn_ref[d_ff - NB:d_ff, :], preferred_element_type=F32)
    x2 = x1 + m[5:6] * acc
    if final:
        x2 = _rms(x2) * fg_ref[...]
    o_ref[0] = x2


def _ffn(x1, modsel, g2, wup, cw, cbias, wdn, fg, final):
    B, T, D = x1.shape
    NTL = T // TM
    d_ff = wdn.shape[0]
    RW = TM // GRID_W
    const = lambda shape: pl.BlockSpec(shape, lambda b, i: (0,) * len(shape))
    if final:
        out_spec = pl.BlockSpec((1, TM, D), lambda b, i: (b, jnp.maximum(i - 1, 0), 0))
        out_shape = jax.ShapeDtypeStruct((B, T - TM, D), F32)
    else:
        out_spec = pl.BlockSpec((1, TM, D), lambda b, i: (b, i, 0))
        out_shape = jax.ShapeDtypeStruct((B, T, D), F32)
    return pl.pallas_call(
        functools.partial(_ffn_kernel, n_tiles=NTL, d_ff=d_ff, final=final),
        grid=(B, NTL),
        in_specs=[
            pl.BlockSpec((1, TM, D), lambda b, i: (b, i, 0)),
            pl.BlockSpec((1, GRID_W, D), lambda b, i: (b, jnp.maximum(i * RW - 1, 0), 0)),
            pl.BlockSpec((1, GRID_W, D),
                         lambda b, i: (b, jnp.minimum((i + 1) * RW, T // GRID_W - 1), 0)),
            pl.BlockSpec((1, 1, 8, D), lambda b, i: (b, jnp.minimum(i, 1), 0, 0)),
            const((1, D)), const((D, 2 * d_ff)),
            const((16, d_ff)), const((1, d_ff)), const((d_ff, D)), const((1, D)),
        ],
        out_specs=out_spec,
        out_shape=out_shape,
        compiler_params=pltpu.CompilerParams(
            dimension_semantics=("parallel", "arbitrary"), vmem_limit_bytes=VMEM_LIMIT),
        name="ffn",
    )(x1, x1, x1, modsel, g2, wup, cw, cbias, wdn, fg)


def kernel(x, c, ctx, c_ctx, ada_w, ada_b, norm1_g, norm2_g, w_in, conv_a_w, rw_w0, rw_w_up, rw_a0,
           rw_a_up, rw_k_k, rw_k_a, rw_r_k, rw_g_up, rw_ln_g, rw_ln_b, w_out, ffn_w_up, ffn_conv_w,
           ffn_conv_b, ffn_w_down, final_g):
    B, SEQ, D = x.shape
    CTX = ctx.shape[1]
    L = w_in.shape[0]
    d_ff = ffn_w_down.shape[1]
    assert CTX == TM and SEQ % TM == 0 and TM % GRID_W == 0 and CHUNK == HEAD
    assert w_in.shape[2] == 3 * D_CONV + 3 * D_RWKV + LORA_WA + G_LORA and d_ff % 256 == 0

    rows = -(-(B + 1) // 8) * 8
    c_rows = jnp.zeros((rows, D), F32).at[:B].set(c).at[B].set(c_ctx)
    mod = _ada(c_rows, ada_w, ada_b)

    hw = LORA_WA // 2
    zpad = jnp.zeros((L, 2, hw, D_RWKV), F32)
    wup_ext = jnp.concatenate([rw_w_up, zpad], axis=2).astype(BF16)
    aup_ext = jnp.concatenate([zpad, rw_a_up], axis=2).astype(BF16)
    head_of = jnp.arange(D_RWKV) // HEAD
    ones_bd = (head_of[:, None] == head_of[None, :]).astype(BF16)
    cw_a = jnp.zeros((L, 8, D_CONV), F32).at[:, :3].set(conv_a_w)
    cw_f = jnp.zeros((L, 16, d_ff), F32).at[:, :9].set(ffn_conv_w.reshape(L, 9, d_ff))

    z = jnp.concatenate([ctx, x], axis=1)
    for l in range(L):
        lat = mod[l, :B].reshape(B, 6, D)
        cm = jnp.broadcast_to(mod[l, B].reshape(1, 6, D), (B, 6, D))
        modsel = jnp.zeros((B, 2, 8, D), F32).at[:, 0, :6].set(cm).at[:, 1, :6].set(lat)

        cb, u, r, kkn, v, lw, b, kd, bonus, gate = _proj(
            z, modsel, norm1_g[l].reshape(1, D), w_in[l].astype(BF16), wup_ext[l], aup_ext[l],
            rw_w0[l], rw_a0[l], rw_k_k[l].reshape(1, -1), rw_k_a[l].reshape(1, -1),
            rw_r_k[l].reshape(1, -1), rw_g_up[l].astype(BF16), ones_bd)
        yf, yb = _scan(r, kkn, v, lw, b, kd, CTX)
        x1 = _mix(
            z, modsel, cb, u, yf, yb, bonus, gate, cw_a[l], rw_ln_g[l].reshape(1, -1),
            rw_ln_b[l].reshape(1, -1), ones_bd, w_out[l].astype(BF16))
        z = _ffn(x1, modsel, norm2_g[l].reshape(1, D), ffn_w_up[l].astype(BF16), cw_f[l],
                 ffn_conv_b[l].reshape(1, -1), ffn_w_down[l].astype(BF16), final_g.reshape(1, D),
                 final=(l == L - 1))
    return z
```

```python
import functools
import math

import jax
import jax.numpy as jnp
from jax import lax
from jax.experimental import pallas as pl
from jax.experimental.pallas import tpu as pltpu

F32 = jnp.float32
BF16 = jnp.bfloat16

HEAD = 64
D_CONV = 512
D_RWKV = 512
LORA_WA = 128
G_LORA = 128
GRID_W = 64
RMS_EPS = 1e-6
GN_EPS = 64e-5
DECAY_SCALE = math.exp(-0.5)

TM = 256
CHUNK = 64
HPG = 2
GROUP = HPG * HEAD
VMEM_LIMIT = 56 * 1024 * 1024

NN = (((1,), (0,)), ((), ()))
NT = (((1,), (1,)), ((), ()))
TN = (((0,), (0,)), ((), ()))


def _silu(x):
    return x * jax.nn.sigmoid(x)


def _split(x):
    hi = x.astype(BF16)
    return hi, (x - hi.astype(F32)).astype(BF16)


def _dg(a, b, dims):
    return lax.dot_general(a, b, dims, preferred_element_type=F32)


def _mm3(a, b_hi, b_lo, dims):
    a_hi, a_lo = _split(a)
    return _dg(a_hi, b_hi, dims) + (_dg(a_hi, b_lo, dims) + _dg(a_lo, b_hi, dims))


def _rms(x):
    return x * lax.rsqrt(jnp.mean(x * x, axis=-1, keepdims=True) + RMS_EPS)


def _segsum(t, ones_bd):
    hi, lo = _split(t)
    return (jnp.dot(hi, ones_bd, preferred_element_type=F32)
            + jnp.dot(lo, ones_bd, preferred_element_type=F32))


def _ada_kernel(c_ref, w_ref, b_ref, o_ref):
    s = _silu(c_ref[...])
    w_hi, w_lo = _split(w_ref[0])
    o_ref[0] = _mm3(s, w_hi, w_lo, NN) + b_ref[0]


def _ada(c_rows, ada_w, ada_b):
    L, D, N = ada_w.shape
    R = c_rows.shape[0]
    NB = 1536
    return pl.pallas_call(
        _ada_kernel,
        grid=(L, N // NB),
        in_specs=[
            pl.BlockSpec((R, D), lambda l, n: (0, 0)),
            pl.BlockSpec((1, D, NB), lambda l, n: (l, 0, n)),
            pl.BlockSpec((1, 1, NB), lambda l, n: (l, 0, n)),
        ],
        out_specs=pl.BlockSpec((1, R, NB), lambda l, n: (l, 0, n)),
        out_shape=jax.ShapeDtypeStruct((L, R, N), F32),
        compiler_params=pltpu.CompilerParams(
            dimension_semantics=("arbitrary", "arbitrary"), vmem_limit_bytes=VMEM_LIMIT),
        name="ada",
    )(c_rows, ada_w, ada_b.reshape(L, 1, N))


def _proj_kernel(z_ref, m_ref, g_ref, win_ref, wup_ref, aup_ref, w0_ref, a0_ref, kk_ref, ka_ref,
                 rk_ref, gup_ref, ones_ref,
                 cb_ref, u_ref, r_ref, kkn_ref, v_ref, lw_ref, b_ref, kd_ref, bonus_ref, gate_ref):
    m = m_ref[0, 0]
    h = _rms(z_ref[0]) * g_ref[...]
    h = (h * (1.0 + m[1:2]) + m[0:1]).astype(BF16)

    def proj(lo, width):
        return jnp.dot(h, win_ref[:, lo:lo + width], preferred_element_type=F32)

    cb_ref[0] = proj(0, D_CONV)
    u_ref[0] = proj(D_CONV, D_CONV) * proj(2 * D_CONV, D_CONV)
    base = 3 * D_CONV
    r = proj(base, D_RWKV)
    k = proj(base + D_RWKV, D_RWKV)
    v = proj(base + 2 * D_RWKV, D_RWKV)
    lora = proj(base + 3 * D_RWKV, LORA_WA + G_LORA)
    wa = lora[:, :LORA_WA]
    gl = lora[:, LORA_WA:]
    ones_bd = ones_ref[...]

    kraw = k * kk_ref[...]
    kkn = kraw * lax.rsqrt(jnp.maximum(_segsum(kraw * kraw, ones_bd), 1e-24))
    r_ref[0] = r
    v_ref[0] = v
    kkn_ref[0] = kkn
    bonus_ref[0] = _segsum(r * k * rk_ref[...], ones_bd) * v
    gate_ref[0] = jnp.dot(jax.nn.sigmoid(gl).astype(BF16), gup_ref[...], preferred_element_type=F32)

    twa = jnp.tanh(wa).astype(BF16)
    wab = wa.astype(BF16)
    for d in range(2):
        lw_ref[d, 0] = -DECAY_SCALE * jax.nn.sigmoid(
            w0_ref[d:d + 1] + jnp.dot(twa, wup_ref[d], preferred_element_type=F32))
        a = jax.nn.sigmoid(a0_ref[d:d + 1] + jnp.dot(wab, aup_ref[d], preferred_element_type=F32))
        b_ref[d, 0] = kkn * a
        kd_ref[d, 0] = k * (1.0 + (a - 1.0) * ka_ref[...])


def _proj(z, modsel, g1, win, wup, aup, w0, a0, k_k, k_a, r_k, gup, ones_bd):
    B, T, D = z.shape
    NTL = T // TM
    P = win.shape[1]
    tok = pl.BlockSpec((1, TM, D_RWKV), lambda b, i: (b, i, 0))
    tok2 = pl.BlockSpec((2, 1, TM, D_RWKV), lambda b, i: (0, b, i, 0))
    const = lambda shape: pl.BlockSpec(shape, lambda b, i: (0,) * len(shape))
    one = jax.ShapeDtypeStruct((B, T, D_RWKV), F32)
    two = jax.ShapeDtypeStruct((2, B, T, D_RWKV), F32)
    return pl.pallas_call(
        _proj_kernel,
        grid=(B, NTL),
        in_specs=[
            pl.BlockSpec((1, TM, D), lambda b, i: (b, i, 0)),
            pl.BlockSpec((1, 1, 8, D), lambda b, i: (b, jnp.minimum(i, 1), 0, 0)),
            const((1, D)), const((D, P)), const((2, LORA_WA, D_RWKV)), const((2, LORA_WA, D_RWKV)),
            const((2, D_RWKV)), const((2, D_RWKV)), const((1, D_RWKV)), const((1, D_RWKV)),
            const((1, D_RWKV)), const((G_LORA, D_RWKV)), const((D_RWKV, D_RWKV)),
        ],
        out_specs=[tok, tok, tok, tok, tok, tok2, tok2, tok2, tok, tok],
        out_shape=[one, one, one, one, one, two, two, two, one, one],
        compiler_params=pltpu.CompilerParams(
            dimension_semantics=("parallel", "parallel"), vmem_limit_bytes=VMEM_LIMIT),
        name="proj",
    )(z, modsel, g1, win, wup, aup, w0, a0, k_k, k_a, r_k, gup, ones_bd)


P_NT, P_INV, P_LOC, P_ST = 1, 1, 1, 1


def _parts(x, passes):
    hi = x.astype(BF16)
    if passes == 1:
        return (hi,)
    return (hi, (x - hi.astype(F32)).astype(BF16))


def _mm(a, xb, dims, split_a=None):
    axis = 1 if dims == TN else 0
    m = a.shape[axis]
    a_hi = a.astype(BF16)
    if split_a is None:
        split_a = len(xb) == 2
    if not split_a:
        return _dg(a_hi, xb[0], dims)
    a_lo = (a - a_hi.astype(F32)).astype(BF16)
    top = _dg(jnp.concatenate([a_hi, a_lo], axis=axis), xb[0], dims)
    out = top[:m] + top[m:]
    if len(xb) == 2:
        out = out + _dg(a_hi, xb[1], dims)
    return out


def _scan_chunks(chains, bdmask):
    C = CHUNK
    row = lax.broadcasted_iota(jnp.int32, (C, GROUP), 0)
    col = lax.broadcasted_iota(jnp.int32, (C, GROUP), 1) & (HEAD - 1)
    eye = (row == col).astype(F32)
    blk16 = (row >> 4) == (col >> 4)
    lane_head = lax.broadcasted_iota(jnp.int32, (HEAD, GROUP), 1) >> 6
    zero_bf = jnp.zeros((), BF16)

    def each(f, *lists):
        return [f(*a) for a in zip(*lists)]

    def bd(x, passes):
        return tuple(jnp.where(bdmask, jnp.concatenate([p] * HPG, axis=0), zero_bf)
                     for p in _parts(x, passes))

    def bd2(x, y, passes, axis):
        return tuple(jnp.concatenate([p, q], axis=axis) for p, q in zip(bd(x, passes), bd(y, passes)))

    def diag_blocks(full):
        out = jnp.where(lane_head == 0, full[0:HEAD], 0.0)
        for hh in range(1, HPG):
            out = out + jnp.where(lane_head == hh, full[hh * HEAD:(hh + 1) * HEAD], 0.0)
        return out

    rev = [ch["rev"] for ch in chains]
    r, kk, v, lw, b, kd, s0, c = ([ch[n] for ch in chains]
                                  for n in ("r", "kk", "v", "lw", "b", "kd", "s0", "c"))
    strict = [(col > row) if rv else (col < row) for rv in rev]
    incl = [(col >= row) if rv else (col <= row) for rv in rev]
    c_last = [ci[0:1] if rv else ci[C - 1:C] for ci, rv in zip(c, rev)]
    en = each(lambda ci: jnp.exp(-ci), c)
    at = each(lambda k_, ci, lwi: k_ * jnp.exp(ci - lwi), kk, c, lw)
    rt = each(lambda r_, ci: r_ * jnp.exp(ci), r, c)
    eh = each(lambda cl, ci: jnp.exp(cl - ci), c_last, c)
    bh = each(lambda x, e: x * e, b, eh)
    kh = each(lambda x, e: x * e, kd, eh)
    p_c = each(jnp.exp, c_last)

    lhs = each(lambda a_, r_: jnp.concatenate([a_, r_], axis=0), at, rt)
    abk = each(lambda l_, x, y, e: _mm(l_, bd2(x * e, y * e, P_NT, 0), NT), lhs, b, kd, en)
    lmat = each(lambda m_, x: jnp.where(m_, x[:C, :GROUP], 0.0), strict, abk)
    aak = each(lambda m_, x: jnp.where(m_, x[:C, GROUP:], 0.0), strict, abk)
    arb = each(lambda m_, x: jnp.where(m_, x[C:, :GROUP], 0.0), incl, abk)
    ark = each(lambda m_, x: jnp.where(m_, x[C:, GROUP:], 0.0), incl, abk)

    ld = each(lambda x: jnp.where(blk16, x, 0.0), lmat)
    lo = each(lambda x, y: x - y, lmat, ld)
    p1 = each(lambda x: eye - x, ld)
    l2 = each(lambda x: _mm(x, bd(x, P_INV), NN), ld)
    t = each(lambda p, x: _mm(jnp.concatenate([p, x], axis=0), bd(x, P_INV), NN), p1, l2)
    p2 = each(lambda p, x: p + x[:C], p1, t)
    l4 = each(lambda x: x[C:], t)
    t = each(lambda p, x: _mm(jnp.concatenate([p, x], axis=0), bd(x, P_INV), NN), p2, l4)
    p3 = each(lambda p, x: p + x[:C], p2, t)
    l8 = each(lambda x: x[C:], t)
    dinv = each(lambda p, x: p + _mm(p, bd(x, P_INV), NN), p3, l8)
    mm = each(lambda d_, x: _mm(d_, bd(x, P_INV), NN), dinv, lo)
    mm2 = each(lambda x: _mm(x, bd(x, P_INV), NN), mm)
    g = each(lambda x, x2: eye - x + x2 - _mm(x, bd(x2, P_INV), NN), mm, mm2)
    tinv = each(lambda g_, d_: _mm(g_, bd(d_, P_INV), NN), g, dinv)

    t = each(lambda x, y, v_: _mm(jnp.concatenate([x, y], axis=0), bd(v_, P_LOC), NN), aak, ark, v)
    akv = each(lambda x: x[:C], t)
    arkv = each(lambda x: x[C:], t)
    t = each(lambda t_, x, y: _mm(t_, bd2(x, y, P_LOC, 1), NN), tinv, at, akv)
    wm = each(lambda x: x[:, :GROUP], t)
    um = each(lambda x: x[:, GROUP:], t)
    t = each(lambda a_, x, y: _mm(a_, bd2(x, y, P_LOC, 1), NN), arb, wm, um)
    qh = each(lambda r_, x: r_ - x[:, :GROUP], rt, t)
    yl = each(lambda y_, x: y_ - x[:, GROUP:], arkv, t)
    t = each(lambda q_, w_, s_: _mm(jnp.concatenate([q_, w_], axis=0), bd(s_, P_ST), NT), qh, wm, s0)
    y = each(lambda x, y_: x[:C] + y_, t, yl)
    ds = each(lambda v_, u_, x, k_, b_: diag_blocks(_mm(
        jnp.concatenate([v_, u_, x[C:]], axis=0),
        _parts(jnp.concatenate([k_, -b_, -b_], axis=0), P_ST), TN)), v, um, t, kh, bh)
    s_new = each(lambda s_, p_, d_: s_ * p_ + d_, s0, p_c, ds)
    return list(zip(y, s_new))


def _scan_kernel(rf_ref, kkf_ref, vf_ref, rb_ref, kkb_ref, vb_ref,
                 lwf_ref, bf_ref, kdf_ref, lwb_ref, bb_ref, kdb_ref, yf_ref, yb_ref, s_ref):
    C = CHUNK

    @pl.when(pl.program_id(1) == 0)
    def _():
        s_ref[...] = jnp.zeros_like(s_ref)

    bdmask = (lax.broadcasted_iota(jnp.int32, (GROUP, GROUP), 0) >> 6) == \
             (lax.broadcasted_iota(jnp.int32, (GROUP, GROUP), 1) >> 6)
    rc = lax.broadcasted_iota(jnp.int32, (C, C), 0)
    cc = lax.broadcasted_iota(jnp.int32, (C, C), 1)
    dirs = ((False, rf_ref, kkf_ref, vf_ref, lwf_ref, bf_ref, kdf_ref, yf_ref),
            (True, rb_ref, kkb_ref, vb_ref, lwb_ref, bb_ref, kdb_ref, yb_ref))
    chains, sinks = [], []
    for bi in range(s_ref.shape[0]):
        for d, (rev, r_ref, kk_ref, v_ref, lw_ref, b_ref, kd_ref, y_ref) in enumerate(dirs):
            tri = ((cc >= rc) if rev else (cc <= rc)).astype(BF16)
            lw = lw_ref[0, bi]
            h1 = lw.astype(BF16)
            r1 = lw - h1.astype(F32)
            h2 = r1.astype(BF16)
            h3 = (r1 - h2.astype(F32)).astype(BF16)
            c = (jnp.dot(tri, h1, preferred_element_type=F32)
                 + jnp.dot(tri, h2, preferred_element_type=F32)
                 + jnp.dot(tri, h3, preferred_element_type=F32))
            for q in range(D_RWKV // GROUP):
                sl = slice(q * GROUP, (q + 1) * GROUP)
                chains.append(dict(rev=rev, r=r_ref[bi, :, sl], kk=kk_ref[bi, :, sl],
                                   v=v_ref[bi, :, sl], lw=lw[:, sl], b=b_ref[0, bi, :, sl],
                                   kd=kd_ref[0, bi, :, sl], s0=s_ref[bi, d, :, sl], c=c[:, sl]))
                sinks.append((y_ref, bi, d, sl))
    for (y, s_new), (y_ref, bi, d, sl) in zip(_scan_chunks(chains, bdmask), sinks):
        y_ref[bi, :, sl] = y
        s_ref[bi, d, :, sl] = s_new


def _scan(r, kk, v, lw, b, kd, ctx_len):
    B, T, W = r.shape
    C = CHUNK
    NC = T // C
    nctx = ctx_len // C

    def back(s):
        return jnp.where(s < nctx, nctx - 1 - s, nctx + NC - 1 - s)

    BB = 2 if B % 2 == 0 else 1
    fwd = pl.BlockSpec((BB, C, W), lambda bb, s: (bb, s, 0))
    bwd = pl.BlockSpec((BB, C, W), lambda bb, s: (bb, back(s), 0))
    fwd_d = pl.BlockSpec((1, BB, C, W), lambda bb, s: (0, bb, s, 0))
    bwd_d = pl.BlockSpec((1, BB, C, W), lambda bb, s: (1, bb, back(s), 0))
    out = jax.ShapeDtypeStruct((B, T, W), F32)
    return pl.pallas_call(
        _scan_kernel,
        grid=(B // BB, NC),
        in_specs=[fwd, fwd, fwd, bwd, bwd, bwd, fwd_d, fwd_d, fwd_d, bwd_d, bwd_d, bwd_d],
        out_specs=[fwd, bwd],
        out_shape=[out, out],
        scratch_shapes=[pltpu.VMEM((BB, 2, HEAD, W), F32)],
        compiler_params=pltpu.CompilerParams(
            dimension_semantics=("parallel", "arbitrary"), vmem_limit_bytes=VMEM_LIMIT),
        name="scan",
    )(r, kk, v, r, kk, v, lw, b, kd, lw, b, kd)


def _mix_kernel(z_ref, m_ref, cb_ref, u_ref, up_ref, un_ref, yf_ref, yb_ref, bonus_ref, gate_ref,
                cw_ref, lng_ref, lnb_ref, ones_ref, wout_ref, x1_ref, *, n_tiles):
    i = pl.program_id(1)
    m = m_ref[0, 0]
    prev_ok = (i >= 2).astype(F32)
    next_ok = jnp.logical_and(i >= 1, i <= n_tiles - 2).astype(F32)
    u = u_ref[0]
    rowi = lax.broadcasted_iota(jnp.int32, (TM, 1), 0)
    um1 = jnp.where(rowi == 0, up_ref[0][7:8] * prev_ok, pltpu.roll(u, 1, 0))
    up1 = jnp.where(rowi == TM - 1, un_ref[0][0:1] * next_ok, pltpu.roll(u, TM - 1, 0))
    cw = cw_ref[...]
    yconv = cb_ref[0] * (cw[0:1] * um1 + cw[1:2] * u + cw[2:3] * up1)

    ones_bd = ones_ref[...]
    y = yf_ref[0] + yb_ref[0]
    mu = _segsum(y, ones_bd) * (1.0 / HEAD)
    dlt = y - mu
    var = _segsum(dlt * dlt, ones_bd) * (1.0 / HEAD)
    yn = dlt * lax.rsqrt(var + GN_EPS) * lng_ref[...] + lnb_ref[...]
    yrw = (yn + bonus_ref[0]) * gate_ref[0]

    out = (jnp.dot(yconv.astype(BF16), wout_ref[0:D_CONV, :], preferred_element_type=F32)
           + jnp.dot(yrw.astype(BF16), wout_ref[D_CONV:D_CONV + D_RWKV, :], preferred_element_type=F32))
    x1_ref[0] = z_ref[0] + m[2:3] * out


def _mix(z, modsel, cb, u, yf, yb, bonus, gate, cw, lng, lnb, ones_bd, wout):
    B, T, D = z.shape
    NTL = T // TM
    tok = pl.BlockSpec((1, TM, D_RWKV), lambda b, i: (b, i, 0))
    const = lambda shape: pl.BlockSpec(shape, lambda b, i: (0,) * len(shape))
    R8 = TM // 8
    return pl.pallas_call(
        functools.partial(_mix_kernel, n_tiles=NTL),
        grid=(B, NTL),
        in_specs=[
            pl.BlockSpec((1, TM, D), lambda b, i: (b, i, 0)),
            pl.BlockSpec((1, 1, 8, D), lambda b, i: (b, jnp.minimum(i, 1), 0, 0)),
            tok, tok,
            pl.BlockSpec((1, 8, D_CONV), lambda b, i: (b, jnp.maximum(i * R8 - 1, 0), 0)),
            pl.BlockSpec((1, 8, D_CONV), lambda b, i: (b, jnp.minimum((i + 1) * R8, T // 8 - 1), 0)),
            tok, tok, tok, tok,
            const((8, D_CONV)), const((1, D_RWKV)), const((1, D_RWKV)), const((D_RWKV, D_RWKV)),
            const((D_CONV + D_RWKV, D)),
        ],
        out_specs=pl.BlockSpec((1, TM, D), lambda b, i: (b, i, 0)),
        out_shape=jax.ShapeDtypeStruct((B, T, D), F32),
        compiler_params=pltpu.CompilerParams(
            dimension_semantics=("parallel", "parallel"), vmem_limit_bytes=VMEM_LIMIT),
        name="mix",
    )(z, modsel, cb, u, u, u, yf, yb, bonus, gate, cw, lng, lnb, ones_bd, wout)


def _ffn_kernel(x1_ref, xt_ref, xb_ref, m_ref, g2_ref, wup_ref, cw_ref, cbias_ref, wdn_ref, fg_ref,
                o_ref, *, n_tiles, d_ff, final):
    i = pl.program_id(1)
    m = m_ref[0, 0]
    lat = (i >= 1).astype(F32)
    top_ok = (i >= 2).astype(F32)
    bot_ok = jnp.logical_and(i >= 1, i <= n_tiles - 2).astype(F32)
    rowi = lax.broadcasted_iota(jnp.int32, (TM, 1), 0)
    colp = jnp.where(i == 0, rowi, rowi & (GRID_W - 1))
    width = jnp.where(i == 0, TM, GRID_W)
    lmask = (colp > 0).astype(F32)
    rmask = (colp < width - 1).astype(F32)

    def norm2(x, keep):
        h = _rms(x) * g2_ref[...]
        return ((h * (1.0 + m[4:5]) + m[3:4]) * keep).astype(BF16)

    x1 = x1_ref[0]
    hx = jnp.concatenate([norm2(xt_ref[0], top_ok), norm2(x1, 1.0), norm2(xb_ref[0], bot_ok)], axis=0)
    hc = hx[GRID_W:GRID_W + TM]

    NB = 256

    def up(n):
        return (jnp.dot(hx, wup_ref[:, n:n + NB], preferred_element_type=F32),
                jnp.dot(hc, wup_ref[:, d_ff + n:d_ff + n + NB], preferred_element_type=F32))

    acc = jnp.zeros((TM, o_ref.shape[-1]), F32)
    nxt = up(0)
    prev = None
    for n in range(0, d_ff, NB):
        gfull, val = nxt
        if n + NB < d_ff:
            nxt = up(n + NB)
        if prev is not None:
            acc = acc + jnp.dot(prev, wdn_ref[n - NB:n, :], preferred_element_type=F32)
        gu, g, gd = gfull[0:TM], gfull[GRID_W:GRID_W + TM], gfull[2 * GRID_W:2 * GRID_W + TM]
        w = cw_ref[:, n:n + NB]
        wv = w * lat
        h0 = wv[0:1] * gu + w[3:4] * g + wv[6:7] * gd
        h1 = wv[1:2] * gu + w[4:5] * g + wv[7:8] * gd
        h2 = wv[2:3] * gu + w[5:6] * g + wv[8:9] * gd
        conv = (lmask * pltpu.roll(h0, 1, 0) + h1 + rmask * pltpu.roll(h2, TM - 1, 0)
                + cbias_ref[:, n:n + NB])
        prev = (_silu(conv) * val).astype(BF16)
    acc = acc + jnp.dot(prev, wdn_ref[d_ff - NB:d_ff, :], preferred_element_type=F32)
    x2 = x1 + m[5:6] * acc
    if final:
        x2 = _rms(x2) * fg_ref[...]
    o_ref[0] = x2


def _ffn(x1, modsel, g2, wup, cw, cbias, wdn, fg, final):
    B, T, D = x1.shape
    NTL = T // TM
    d_ff = wdn.shape[0]
    RW = TM // GRID_W
    const = lambda shape: pl.BlockSpec(shape, lambda b, i: (0,) * len(shape))
    if final:
        out_spec = pl.BlockSpec((1, TM, D), lambda b, i: (b, jnp.maximum(i - 1, 0), 0))
        out_shape = jax.ShapeDtypeStruct((B, T - TM, D), F32)
    else:
        out_spec = pl.BlockSpec((1, TM, D), lambda b, i: (b, i, 0))
        out_shape = jax.ShapeDtypeStruct((B, T, D), F32)
    return pl.pallas_call(
        functools.partial(_ffn_kernel, n_tiles=NTL, d_ff=d_ff, final=final),
        grid=(B, NTL),
        in_specs=[
            pl.BlockSpec((1, TM, D), lambda b, i: (b, i, 0)),
            pl.BlockSpec((1, GRID_W, D), lambda b, i: (b, jnp.maximum(i * RW - 1, 0), 0)),
            pl.BlockSpec((1, GRID_W, D),
                         lambda b, i: (b, jnp.minimum((i + 1) * RW, T // GRID_W - 1), 0)),
            pl.BlockSpec((1, 1, 8, D), lambda b, i: (b, jnp.minimum(i, 1), 0, 0)),
            const((1, D)), const((D, 2 * d_ff)),
            const((16, d_ff)), const((1, d_ff)), const((d_ff, D)), const((1, D)),
        ],
        out_specs=out_spec,
        out_shape=out_shape,
        compiler_params=pltpu.CompilerParams(
            dimension_semantics=("parallel", "arbitrary"), vmem_limit_bytes=VMEM_LIMIT),
        name="ffn",
    )(x1, x1, x1, modsel, g2, wup, cw, cbias, wdn, fg)


def kernel(x, c, ctx, c_ctx, ada_w, ada_b, norm1_g, norm2_g, w_in, conv_a_w, rw_w0, rw_w_up, rw_a0,
           rw_a_up, rw_k_k, rw_k_a, rw_r_k, rw_g_up, rw_ln_g, rw_ln_b, w_out, ffn_w_up, ffn_conv_w,
           ffn_conv_b, ffn_w_down, final_g):
    B, SEQ, D = x.shape
    CTX = ctx.shape[1]
    L = w_in.shape[0]
    d_ff = ffn_w_down.shape[1]
    assert CTX == TM and SEQ % TM == 0 and TM % GRID_W == 0 and CHUNK == HEAD
    assert w_in.shape[2] == 3 * D_CONV + 3 * D_RWKV + LORA_WA + G_LORA and d_ff % 256 == 0

    rows = -(-(B + 1) // 8) * 8
    c_rows = jnp.zeros((rows, D), F32).at[:B].set(c).at[B].set(c_ctx)
    mod = _ada(c_rows, ada_w, ada_b)

    hw = LORA_WA // 2
    zpad = jnp.zeros((L, 2, hw, D_RWKV), F32)
    wup_ext = jnp.concatenate([rw_w_up, zpad], axis=2).astype(BF16)
    aup_ext = jnp.concatenate([zpad, rw_a_up], axis=2).astype(BF16)
    head_of = jnp.arange(D_RWKV) // HEAD
    ones_bd = (head_of[:, None] == head_of[None, :]).astype(BF16)
    cw_a = jnp.zeros((L, 8, D_CONV), F32).at[:, :3].set(conv_a_w)
    cw_f = jnp.zeros((L, 16, d_ff), F32).at[:, :9].set(ffn_conv_w.reshape(L, 9, d_ff))

    z = jnp.concatenate([ctx, x], axis=1)
    for l in range(L):
        lat = mod[l, :B].reshape(B, 6, D)
        cm = jnp.broadcast_to(mod[l, B].reshape(1, 6, D), (B, 6, D))
        modsel = jnp.zeros((B, 2, 8, D), F32).at[:, 0, :6].set(cm).at[:, 1, :6].set(lat)

        cb, u, r, kkn, v, lw, b, kd, bonus, gate = _proj(
            z, modsel, norm1_g[l].reshape(1, D), w_in[l].astype(BF16), wup_ext[l], aup_ext[l],
            rw_w0[l], rw_a0[l], rw_k_k[l].reshape(1, -1), rw_k_a[l].reshape(1, -1),
            rw_r_k[l].reshape(1, -1), rw_g_up[l].astype(BF16), ones_bd)
        yf, yb = _scan(r, kkn, v, lw, b, kd, CTX)
        x1 = _mix(
            z, modsel, cb, u, yf, yb, bonus, gate, cw_a[l], rw_ln_g[l].reshape(1, -1),
            rw_ln_b[l].reshape(1, -1), ones_bd, w_out[l].astype(BF16))
        z = _ffn(x1, modsel, norm2_g[l].reshape(1, D), ffn_w_up[l].astype(BF16), cw_f[l],
                 ffn_conv_b[l].reshape(1, -1), ffn_w_down[l].astype(BF16), final_g.reshape(1, D),
                 final=(l == L - 1))
    return z
```

```python
import functools
import math

import jax
import jax.numpy as jnp
from jax import lax
from jax.experimental import pallas as pl
from jax.experimental.pallas import tpu as pltpu

F32 = jnp.float32
BF16 = jnp.bfloat16

HEAD = 64
D_CONV = 512
D_RWKV = 512
LORA_WA = 128
G_LORA = 128
GRID_W = 64
RMS_EPS = 1e-6
GN_EPS = 64e-5
DECAY_SCALE = math.exp(-0.5)

TM = 256
CHUNK = 64
HPG = 2
GROUP = HPG * HEAD
VMEM_LIMIT = 56 * 1024 * 1024

NN = (((1,), (0,)), ((), ()))
NT = (((1,), (1,)), ((), ()))
TN = (((0,), (0,)), ((), ()))


def _silu(x):
    return x * jax.nn.sigmoid(x)


def _split(x):
    hi = x.astype(BF16)
    return hi, (x - hi.astype(F32)).astype(BF16)


def _dg(a, b, dims):
    return lax.dot_general(a, b, dims, preferred_element_type=F32)


def _mm3(a, b_hi, b_lo, dims):
    a_hi, a_lo = _split(a)
    return _dg(a_hi, b_hi, dims) + (_dg(a_hi, b_lo, dims) + _dg(a_lo, b_hi, dims))


def _rms(x):
    return x * lax.rsqrt(jnp.mean(x * x, axis=-1, keepdims=True) + RMS_EPS)


def _segsum(t, ones_bd):
    hi, lo = _split(t)
    return (jnp.dot(hi, ones_bd, preferred_element_type=F32)
            + jnp.dot(lo, ones_bd, preferred_element_type=F32))


def _z_tile(za_ref, zb_ref, split):
    if not split:
        return za_ref[0]
    return jnp.where(pl.program_id(1) == 0, za_ref[0], zb_ref[0])


def _z_specs(split, D):
    if split:
        return [pl.BlockSpec((1, TM, D), lambda b, i: (b, 0, 0)),
                pl.BlockSpec((1, TM, D), lambda b, i: (b, jnp.maximum(i - 1, 0), 0))]
    return [pl.BlockSpec((1, TM, D), lambda b, i: (b, i, 0)),
            pl.BlockSpec((1, 8, D), lambda b, i: (0, 0, 0))]


def _ada_kernel(c_ref, w_ref, b_ref, o_ref):
    s = _silu(c_ref[...])
    w_hi, w_lo = _split(w_ref[0])
    o_ref[0] = _mm3(s, w_hi, w_lo, NN) + b_ref[0]


def _ada(c_rows, ada_w, ada_b):
    L, D, N = ada_w.shape
    R = c_rows.shape[0]
    NB = 1536
    return pl.pallas_call(
        _ada_kernel,
        grid=(L, N // NB),
        in_specs=[
            pl.BlockSpec((R, D), lambda l, n: (0, 0)),
            pl.BlockSpec((1, D, NB), lambda l, n: (l, 0, n)),
            pl.BlockSpec((1, 1, NB), lambda l, n: (l, 0, n)),
        ],
        out_specs=pl.BlockSpec((1, R, NB), lambda l, n: (l, 0, n)),
        out_shape=jax.ShapeDtypeStruct((L, R, N), F32),
        compiler_params=pltpu.CompilerParams(
            dimension_semantics=("arbitrary", "arbitrary"), vmem_limit_bytes=VMEM_LIMIT),
        name="ada",
    )(c_rows, ada_w, ada_b.reshape(L, 1, N))


def _proj_kernel(za_ref, zb_ref, m_ref, g_ref, win_ref, wup_ref, aup_ref, w0_ref, a0_ref, kk_ref, ka_ref,
                 rk_ref, gup_ref, ones_ref,
                 cb_ref, u_ref, r_ref, kkn_ref, v_ref, lw_ref, b_ref, kd_ref, bonus_ref, gate_ref,
                 *, split):
    m = m_ref[0, 0]
    h = _rms(_z_tile(za_ref, zb_ref, split)) * g_ref[...]
    h = (h * (1.0 + m[1:2]) + m[0:1]).astype(BF16)

    def proj(lo, width):
        return jnp.dot(h, win_ref[:, lo:lo + width], preferred_element_type=F32)

    cb_ref[0] = proj(0, D_CONV)
    u_ref[0] = proj(D_CONV, D_CONV) * proj(2 * D_CONV, D_CONV)
    base = 3 * D_CONV
    r = proj(base, D_RWKV)
    k = proj(base + D_RWKV, D_RWKV)
    v = proj(base + 2 * D_RWKV, D_RWKV)
    lora = proj(base + 3 * D_RWKV, LORA_WA + G_LORA)
    wa = lora[:, :LORA_WA]
    gl = lora[:, LORA_WA:]
    ones_bd = ones_ref[...]

    kraw = k * kk_ref[...]
    kkn = kraw * lax.rsqrt(jnp.maximum(_segsum(kraw * kraw, ones_bd), 1e-24))
    r_ref[0] = r
    v_ref[0] = v
    kkn_ref[0] = kkn
    bonus_ref[0] = _segsum(r * k * rk_ref[...], ones_bd) * v
    gate_ref[0] = jnp.dot(jax.nn.sigmoid(gl).astype(BF16), gup_ref[...], preferred_element_type=F32)

    twa = jnp.tanh(wa).astype(BF16)
    wab = wa.astype(BF16)
    for d in range(2):
        lw_ref[d, 0] = -DECAY_SCALE * jax.nn.sigmoid(
            w0_ref[d:d + 1] + jnp.dot(twa, wup_ref[d], preferred_element_type=F32))
        a = jax.nn.sigmoid(a0_ref[d:d + 1] + jnp.dot(wab, aup_ref[d], preferred_element_type=F32))
        b_ref[d, 0] = kkn * a
        kd_ref[d, 0] = k * (1.0 + (a - 1.0) * ka_ref[...])


def _proj(za, zb, split, modsel, g1, win, wup, aup, w0, a0, k_k, k_a, r_k, gup, ones_bd):
    B, D = za.shape[0], za.shape[2]
    T = za.shape[1] + zb.shape[1] if split else za.shape[1]
    NTL = T // TM
    P = win.shape[1]
    tok = pl.BlockSpec((1, TM, D_RWKV), lambda b, i: (b, i, 0))
    tok2 = pl.BlockSpec((2, 1, TM, D_RWKV), lambda b, i: (0, b, i, 0))
    const = lambda shape: pl.BlockSpec(shape, lambda b, i: (0,) * len(shape))
    one = jax.ShapeDtypeStruct((B, T, D_RWKV), F32)
    two = jax.ShapeDtypeStruct((2, B, T, D_RWKV), F32)
    return pl.pallas_call(
        functools.partial(_proj_kernel, split=split),
        grid=(B, NTL),
        in_specs=_z_specs(split, D) + [
            pl.BlockSpec((1, 1, 8, D), lambda b, i: (b, jnp.minimum(i, 1), 0, 0)),
            const((1, D)), const((D, P)), const((2, LORA_WA, D_RWKV)), const((2, LORA_WA, D_RWKV)),
            const((2, D_RWKV)), const((2, D_RWKV)), const((1, D_RWKV)), const((1, D_RWKV)),
            const((1, D_RWKV)), const((G_LORA, D_RWKV)), const((D_RWKV, D_RWKV)),
        ],
        out_specs=[tok, tok, tok, tok, tok, tok2, tok2, tok2, tok, tok],
        out_shape=[one, one, one, one, one, two, two, two, one, one],
        compiler_params=pltpu.CompilerParams(
            dimension_semantics=("parallel", "parallel"), vmem_limit_bytes=VMEM_LIMIT),
        name="proj",
    )(za, zb, modsel, g1, win, wup, aup, w0, a0, k_k, k_a, r_k, gup, ones_bd)


P_NT, P_INV, P_LOC, P_ST = 1, 1, 1, 1


def _parts(x, passes):
    hi = x.astype(BF16)
    if passes == 1:
        return (hi,)
    return (hi, (x - hi.astype(F32)).astype(BF16))


def _mm(a, xb, dims, split_a=None):
    axis = 1 if dims == TN else 0
    m = a.shape[axis]
    a_hi = a.astype(BF16)
    if split_a is None:
        split_a = len(xb) == 2
    if not split_a:
        return _dg(a_hi, xb[0], dims)
    a_lo = (a - a_hi.astype(F32)).astype(BF16)
    top = _dg(jnp.concatenate([a_hi, a_lo], axis=axis), xb[0], dims)
    out = top[:m] + top[m:]
    if len(xb) == 2:
        out = out + _dg(a_hi, xb[1], dims)
    return out


def _scan_chunks(chains, bdmask):
    C = CHUNK
    row = lax.broadcasted_iota(jnp.int32, (C, GROUP), 0)
    col = lax.broadcasted_iota(jnp.int32, (C, GROUP), 1) & (HEAD - 1)
    eye = (row == col).astype(F32)
    blk16 = (row >> 4) == (col >> 4)
    lane_head = lax.broadcasted_iota(jnp.int32, (HEAD, GROUP), 1) >> 6
    zero_bf = jnp.zeros((), BF16)

    def each(f, *lists):
        return [f(*a) for a in zip(*lists)]

    def bd(x, passes):
        return tuple(jnp.where(bdmask, jnp.concatenate([p] * HPG, axis=0), zero_bf)
                     for p in _parts(x, passes))

    def bd2(x, y, passes, axis):
        return tuple(jnp.concatenate([p, q], axis=axis) for p, q in zip(bd(x, passes), bd(y, passes)))

    def diag_blocks(full):
        out = jnp.where(lane_head == 0, full[0:HEAD], 0.0)
        for hh in range(1, HPG):
            out = out + jnp.where(lane_head == hh, full[hh * HEAD:(hh + 1) * HEAD], 0.0)
        return out

    rev = [ch["rev"] for ch in chains]
    r, kk, v, lw, b, kd, s0, c = ([ch[n] for ch in chains]
                                  for n in ("r", "kk", "v", "lw", "b", "kd", "s0", "c"))
    strict = [(col > row) if rv else (col < row) for rv in rev]
    incl = [(col >= row) if rv else (col <= row) for rv in rev]
    c_last = [ci[0:1] if rv else ci[C - 1:C] for ci, rv in zip(c, rev)]
    en = each(lambda ci: jnp.exp(-ci), c)
    at = each(lambda k_, ci, lwi: k_ * jnp.exp(ci - lwi), kk, c, lw)
    rt = each(lambda r_, ci: r_ * jnp.exp(ci), r, c)
    p_c = each(jnp.exp, c_last)
    bt = each(lambda x, e: x * e, b, en)
    kt = each(lambda x, e: x * e, kd, en)
    bh = each(lambda x, p: x * p, bt, p_c)
    kh = each(lambda x, p: x * p, kt, p_c)

    lhs = each(lambda a_, r_: jnp.concatenate([a_, r_], axis=0), at, rt)
    abk = each(lambda l_, x, y: _mm(l_, bd2(x, y, P_NT, 0), NT), lhs, bt, kt)
    lmat = each(lambda m_, x: jnp.where(m_, x[:C, :GROUP], 0.0), strict, abk)
    aak = each(lambda m_, x: jnp.where(m_, x[:C, GROUP:], 0.0), strict, abk)
    arb = each(lambda m_, x: jnp.where(m_, x[C:, :GROUP], 0.0), incl, abk)
    ark = each(lambda m_, x: jnp.where(m_, x[C:, GROUP:], 0.0), incl, abk)

    ld = each(lambda x: jnp.where(blk16, x, 0.0), lmat)
    lo = each(lambda x, y: x - y, lmat, ld)
    p1 = each(lambda x: eye - x, ld)
    l2 = each(lambda x: _mm(x, bd(x, P_INV), NN), ld)
    t = each(lambda p, x: _mm(jnp.concatenate([p, x], axis=0), bd(x, P_INV), NN), p1, l2)
    p2 = each(lambda p, x: p + x[:C], p1, t)
    l4 = each(lambda x: x[C:], t)
    t = each(lambda p, x: _mm(jnp.concatenate([p, x], axis=0), bd(x, P_INV), NN), p2, l4)
    p3 = each(lambda p, x: p + x[:C], p2, t)
    l8 = each(lambda x: x[C:], t)
    dinv = each(lambda p, x: p + _mm(p, bd(x, P_INV), NN), p3, l8)
    mm = each(lambda d_, x: _mm(d_, bd(x, P_INV), NN), dinv, lo)
    mm2 = each(lambda x: _mm(x, bd(x, P_INV), NN), mm)
    g = each(lambda x, x2: eye - x + x2 - _mm(x, bd(x2, P_INV), NN), mm, mm2)
    tinv = each(lambda g_, d_: _mm(g_, bd(d_, P_INV), NN), g, dinv)

    t = each(lambda x, y, v_: _mm(jnp.concatenate([x, y], axis=0), bd(v_, P_LOC), NN), aak, ark, v)
    akv = each(lambda x: x[:C], t)
    arkv = each(lambda x: x[C:], t)
    t = each(lambda t_, x, y: _mm(t_, bd2(x, y, P_LOC, 1), NN), tinv, at, akv)
    wm = each(lambda x: x[:, :GROUP], t)
    um = each(lambda x: x[:, GROUP:], t)
    t = each(lambda a_, x, y: _mm(a_, bd2(x, y, P_LOC, 1), NN), arb, wm, um)
    qh = each(lambda r_, x: r_ - x[:, :GROUP], rt, t)
    yl = each(lambda y_, x: y_ - x[:, GROUP:], arkv, t)
    t = each(lambda q_, w_, s_: _mm(jnp.concatenate([q_, w_], axis=0), bd(s_, P_ST), NT), qh, wm, s0)
    y = each(lambda x, y_: x[:C] + y_, t, yl)
    ds = each(lambda v_, u_, x, k_, b_: diag_blocks(_mm(
        jnp.concatenate([v_, u_, x[C:]], axis=0),
        _parts(jnp.concatenate([k_, -b_, -b_], axis=0), P_ST), TN)), v, um, t, kh, bh)
    s_new = each(lambda s_, p_, d_: s_ * p_ + d_, s0, p_c, ds)
    return list(zip(y, s_new))


def _scan_kernel(rf_ref, kkf_ref, vf_ref, rb_ref, kkb_ref, vb_ref,
                 lwf_ref, bf_ref, kdf_ref, lwb_ref, bb_ref, kdb_ref, yf_ref, yb_ref, s_ref):
    C = CHUNK

    @pl.when(pl.program_id(1) == 0)
    def _():
        s_ref[...] = jnp.zeros_like(s_ref)

    bdmask = (lax.broadcasted_iota(jnp.int32, (GROUP, GROUP), 0) >> 6) == \
             (lax.broadcasted_iota(jnp.int32, (GROUP, GROUP), 1) >> 6)
    rc = lax.broadcasted_iota(jnp.int32, (C, C), 0)
    cc = lax.broadcasted_iota(jnp.int32, (C, C), 1)
    dirs = ((False, rf_ref, kkf_ref, vf_ref, lwf_ref, bf_ref, kdf_ref, yf_ref),
            (True, rb_ref, kkb_ref, vb_ref, lwb_ref, bb_ref, kdb_ref, yb_ref))
    chains, sinks = [], []
    for bi in range(s_ref.shape[0]):
        for d, (rev, r_ref, kk_ref, v_ref, lw_ref, b_ref, kd_ref, y_ref) in enumerate(dirs):
            tri = ((cc >= rc) if rev else (cc <= rc)).astype(BF16)
            lw = lw_ref[0, bi]
            h1 = lw.astype(BF16)
            r1 = lw - h1.astype(F32)
            h2 = r1.astype(BF16)
            h3 = (r1 - h2.astype(F32)).astype(BF16)
            c = (jnp.dot(tri, h1, preferred_element_type=F32)
                 + jnp.dot(tri, h2, preferred_element_type=F32)
                 + jnp.dot(tri, h3, preferred_element_type=F32))
            for q in range(D_RWKV // GROUP):
                sl = slice(q * GROUP, (q + 1) * GROUP)
                chains.append(dict(rev=rev, r=r_ref[bi, :, sl], kk=kk_ref[bi, :, sl],
                                   v=v_ref[bi, :, sl], lw=lw[:, sl], b=b_ref[0, bi, :, sl],
                                   kd=kd_ref[0, bi, :, sl], s0=s_ref[bi, d, :, sl], c=c[:, sl]))
                sinks.append((y_ref, bi, d, sl))
    for (y, s_new), (y_ref, bi, d, sl) in zip(_scan_chunks(chains, bdmask), sinks):
        y_ref[bi, :, sl] = y
        s_ref[bi, d, :, sl] = s_new


def _scan(r, kk, v, lw, b, kd, ctx_len):
    B, T, W = r.shape
    C = CHUNK
    NC = T // C
    nctx = ctx_len // C

    def back(s):
        return jnp.where(s < nctx, nctx - 1 - s, nctx + NC - 1 - s)

    BB = 2 if B % 2 == 0 else 1
    fwd = pl.BlockSpec((BB, C, W), lambda bb, s: (bb, s, 0))
    bwd = pl.BlockSpec((BB, C, W), lambda bb, s: (bb, back(s), 0))
    fwd_d = pl.BlockSpec((1, BB, C, W), lambda bb, s: (0, bb, s, 0))
    bwd_d = pl.BlockSpec((1, BB, C, W), lambda bb, s: (1, bb, back(s), 0))
    out = jax.ShapeDtypeStruct((B, T, W), F32)
    return pl.pallas_call(
        _scan_kernel,
        grid=(B // BB, NC),
        in_specs=[fwd, fwd, fwd, bwd, bwd, bwd, fwd_d, fwd_d, fwd_d, bwd_d, bwd_d, bwd_d],
        out_specs=[fwd, bwd],
        out_shape=[out, out],
        scratch_shapes=[pltpu.VMEM((BB, 2, HEAD, W), F32)],
        compiler_params=pltpu.CompilerParams(
            dimension_semantics=("parallel", "arbitrary"), vmem_limit_bytes=VMEM_LIMIT),
        name="scan",
    )(r, kk, v, r, kk, v, lw, b, kd, lw, b, kd)


def _mix_kernel(za_ref, zb_ref, m_ref, cb_ref, u_ref, up_ref, un_ref, yf_ref, yb_ref, bonus_ref, gate_ref,
                cw_ref, lng_ref, lnb_ref, ones_ref, wout_ref, x1_ref, *, n_tiles, split):
    i = pl.program_id(1)
    m = m_ref[0, 0]
    prev_ok = (i >= 2).astype(F32)
    next_ok = jnp.logical_and(i >= 1, i <= n_tiles - 2).astype(F32)
    u = u_ref[0]
    rowi = lax.broadcasted_iota(jnp.int32, (TM, 1), 0)
    um1 = jnp.where(rowi == 0, up_ref[0][7:8] * prev_ok, pltpu.roll(u, 1, 0))
    up1 = jnp.where(rowi == TM - 1, un_ref[0][0:1] * next_ok, pltpu.roll(u, TM - 1, 0))
    cw = cw_ref[...]
    yconv = cb_ref[0] * (cw[0:1] * um1 + cw[1:2] * u + cw[2:3] * up1)

    ones_bd = ones_ref[...]
    y = yf_ref[0] + yb_ref[0]
    mu = _segsum(y, ones_bd) * (1.0 / HEAD)
    dlt = y - mu
    var = _segsum(dlt * dlt, ones_bd) * (1.0 / HEAD)
    yn = dlt * lax.rsqrt(var + GN_EPS) * lng_ref[...] + lnb_ref[...]
    yrw = (yn + bonus_ref[0]) * gate_ref[0]

    out = (jnp.dot(yconv.astype(BF16), wout_ref[0:D_CONV, :], preferred_element_type=F32)
           + jnp.dot(yrw.astype(BF16), wout_ref[D_CONV:D_CONV + D_RWKV, :], preferred_element_type=F32))
    x1_ref[0] = _z_tile(za_ref, zb_ref, split) + m[2:3] * out


def _mix(za, zb, split, modsel, cb, u, yf, yb, bonus, gate, cw, lng, lnb, ones_bd, wout):
    B, T, D = cb.shape[0], cb.shape[1], za.shape[2]
    NTL = T // TM
    tok = pl.BlockSpec((1, TM, D_RWKV), lambda b, i: (b, i, 0))
    const = lambda shape: pl.BlockSpec(shape, lambda b, i: (0,) * len(shape))
    R8 = TM // 8
    return pl.pallas_call(
        functools.partial(_mix_kernel, n_tiles=NTL, split=split),
        grid=(B, NTL),
        in_specs=_z_specs(split, D) + [
            pl.BlockSpec((1, 1, 8, D), lambda b, i: (b, jnp.minimum(i, 1), 0, 0)),
            tok, tok,
            pl.BlockSpec((1, 8, D_CONV), lambda b, i: (b, jnp.maximum(i * R8 - 1, 0), 0)),
            pl.BlockSpec((1, 8, D_CONV), lambda b, i: (b, jnp.minimum((i + 1) * R8, T // 8 - 1), 0)),
            tok, tok, tok, tok,
            const((8, D_CONV)), const((1, D_RWKV)), const((1, D_RWKV)), const((D_RWKV, D_RWKV)),
            const((D_CONV + D_RWKV, D)),
        ],
        out_specs=pl.BlockSpec((1, TM, D), lambda b, i: (b, i, 0)),
        out_shape=jax.ShapeDtypeStruct((B, T, D), F32),
        compiler_params=pltpu.CompilerParams(
            dimension_semantics=("parallel", "parallel"), vmem_limit_bytes=VMEM_LIMIT),
        name="mix",
    )(za, zb, modsel, cb, u, u, u, yf, yb, bonus, gate, cw, lng, lnb, ones_bd, wout)


def _ffn_kernel(x1_ref, xb_ref, m_ref, g2_ref, wup_ref, cw_ref, cbias_ref, wdn_ref, fg_ref,
                o_ref, gtop_ref, *, n_tiles, d_ff, final):
    i = pl.program_id(1)
    m = m_ref[0, 0]
    lat = (i >= 1).astype(F32)
    top_ok = i >= 2
    bot_ok = jnp.logical_and(i >= 1, i <= n_tiles - 2).astype(F32)
    rowi = lax.broadcasted_iota(jnp.int32, (TM, 1), 0)
    colp = jnp.where(i == 0, rowi, rowi & (GRID_W - 1))
    width = jnp.where(i == 0, TM, GRID_W)
    lmask = (colp > 0).astype(F32)
    rmask = (colp < width - 1).astype(F32)

    def norm2(x, keep):
        h = _rms(x) * g2_ref[...]
        return ((h * (1.0 + m[4:5]) + m[3:4]) * keep).astype(BF16)

    @pl.when(i == 0)
    def _():
        gtop_ref[...] = jnp.zeros_like(gtop_ref)

    x1 = x1_ref[0]
    hx = jnp.concatenate([norm2(x1, 1.0), norm2(xb_ref[0], bot_ok)], axis=0)
    hc = hx[0:TM]

    NB = 256

    def up(n):
        return (jnp.dot(hx, wup_ref[:, n:n + NB], preferred_element_type=F32),
                jnp.dot(hc, wup_ref[:, d_ff + n:d_ff + n + NB], preferred_element_type=F32))

    acc = jnp.zeros((TM, o_ref.shape[-1]), F32)
    nxt = up(0)
    prev = None
    for n in range(0, d_ff, NB):
        gfull, val = nxt
        if n + NB < d_ff:
            nxt = up(n + NB)
        if prev is not None:
            acc = acc + jnp.dot(prev, wdn_ref[n - NB:n, :], preferred_element_type=F32)
        g, gd = gfull[0:TM], gfull[GRID_W:GRID_W + TM]
        top = jnp.where(top_ok, gtop_ref[:, n:n + NB], 0.0)
        gu = jnp.concatenate([top, g[0:TM - GRID_W]], axis=0)
        gtop_ref[:, n:n + NB] = g[TM - GRID_W:TM]
        w = cw_ref[:, n:n + NB]
        wv = w * lat
        h0 = wv[0:1] * gu + w[3:4] * g + wv[6:7] * gd
        h1 = wv[1:2] * gu + w[4:5] * g + wv[7:8] * gd
        h2 = wv[2:3] * gu + w[5:6] * g + wv[8:9] * gd
        conv = (lmask * pltpu.roll(h0, 1, 0) + h1 + rmask * pltpu.roll(h2, TM - 1, 0)
                + cbias_ref[:, n:n + NB])
        prev = (_silu(conv) * val).astype(BF16)
    acc = acc + jnp.dot(prev, wdn_ref[d_ff - NB:d_ff, :], preferred_element_type=F32)
    x2 = x1 + m[5:6] * acc
    if final:
        x2 = _rms(x2) * fg_ref[...]
    o_ref[0] = x2


def _ffn(x1, modsel, g2, wup, cw, cbias, wdn, fg, final):
    B, T, D = x1.shape
    NTL = T // TM
    d_ff = wdn.shape[0]
    RW = TM // GRID_W
    const = lambda shape: pl.BlockSpec(shape, lambda b, i: (0,) * len(shape))
    if final:
        out_spec = pl.BlockSpec((1, TM, D), lambda b, i: (b, jnp.maximum(i - 1, 0), 0))
        out_shape = jax.ShapeDtypeStruct((B, T - TM, D), F32)
    else:
        out_spec = pl.BlockSpec((1, TM, D), lambda b, i: (b, i, 0))
        out_shape = jax.ShapeDtypeStruct((B, T, D), F32)
    return pl.pallas_call(
        functools.partial(_ffn_kernel, n_tiles=NTL, d_ff=d_ff, final=final),
        grid=(B, NTL),
        in_specs=[
            pl.BlockSpec((1, TM, D), lambda b, i: (b, i, 0)),
            pl.BlockSpec((1, GRID_W, D),
                         lambda b, i: (b, jnp.minimum((i + 1) * RW, T // GRID_W - 1), 0)),
            pl.BlockSpec((1, 1, 8, D), lambda b, i: (b, jnp.minimum(i, 1), 0, 0)),
            const((1, D)), const((D, 2 * d_ff)),
            const((16, d_ff)), const((1, d_ff)), const((d_ff, D)), const((1, D)),
        ],
        out_specs=out_spec,
        out_shape=out_shape,
        scratch_shapes=[pltpu.VMEM((GRID_W, d_ff), F32)],
        compiler_params=pltpu.CompilerParams(
            dimension_semantics=("parallel", "arbitrary"), vmem_limit_bytes=VMEM_LIMIT),
        name="ffn",
    )(x1, x1, modsel, g2, wup, cw, cbias, wdn, fg)


def kernel(x, c, ctx, c_ctx, ada_w, ada_b, norm1_g, norm2_g, w_in, conv_a_w, rw_w0, rw_w_up, rw_a0,
           rw_a_up, rw_k_k, rw_k_a, rw_r_k, rw_g_up, rw_ln_g, rw_ln_b, w_out, ffn_w_up, ffn_conv_w,
           ffn_conv_b, ffn_w_down, final_g):
    B, SEQ, D = x.shape
    CTX = ctx.shape[1]
    L = w_in.shape[0]
    d_ff = ffn_w_down.shape[1]
    assert CTX == TM and SEQ % TM == 0 and TM % GRID_W == 0 and CHUNK == HEAD
    assert w_in.shape[2] == 3 * D_CONV + 3 * D_RWKV + LORA_WA + G_LORA and d_ff % 256 == 0

    rows = -(-(B + 1) // 8) * 8
    c_rows = jnp.zeros((rows, D), F32).at[:B].set(c).at[B].set(c_ctx)
    mod = _ada(c_rows, ada_w, ada_b)

    hw = LORA_WA // 2
    zpad = jnp.zeros((L, 2, hw, D_RWKV), F32)
    wup_ext = jnp.concatenate([rw_w_up, zpad], axis=2).astype(BF16)
    aup_ext = jnp.concatenate([zpad, rw_a_up], axis=2).astype(BF16)
    head_of = jnp.arange(D_RWKV) // HEAD
    ones_bd = (head_of[:, None] == head_of[None, :]).astype(BF16)
    cw_a = jnp.zeros((L, 8, D_CONV), F32).at[:, :3].set(conv_a_w)
    cw_f = jnp.zeros((L, 16, d_ff), F32).at[:, :9].set(ffn_conv_w.reshape(L, 9, d_ff))

    za, zb = ctx, x
    for l in range(L):
        split = l == 0
        lat = mod[l, :B].reshape(B, 6, D)
        cm = jnp.broadcast_to(mod[l, B].reshape(1, 6, D), (B, 6, D))
        modsel = jnp.zeros((B, 2, 8, D), F32).at[:, 0, :6].set(cm).at[:, 1, :6].set(lat)

        cb, u, r, kkn, v, lw, b, kd, bonus, gate = _proj(
            za, zb, split, modsel, norm1_g[l].reshape(1, D), w_in[l].astype(BF16), wup_ext[l], aup_ext[l],
            rw_w0[l], rw_a0[l], rw_k_k[l].reshape(1, -1), rw_k_a[l].reshape(1, -1),
            rw_r_k[l].reshape(1, -1), rw_g_up[l].astype(BF16), ones_bd)
        yf, yb = _scan(r, kkn, v, lw, b, kd, CTX)
        x1 = _mix(
            za, zb, split, modsel, cb, u, yf, yb, bonus, gate, cw_a[l], rw_ln_g[l].reshape(1, -1),
            rw_ln_b[l].reshape(1, -1), ones_bd, w_out[l].astype(BF16))
        za = zb = _ffn(x1, modsel, norm2_g[l].reshape(1, D), ffn_w_up[l].astype(BF16), cw_f[l],
                 ffn_conv_b[l].reshape(1, -1), ffn_w_down[l].astype(BF16), final_g.reshape(1, D),
                 final=(l == L - 1))
    return za
```

```python
import functools
import math

import jax
import jax.numpy as jnp
from jax import lax
from jax.experimental import pallas as pl
from jax.experimental.pallas import tpu as pltpu

F32 = jnp.float32
BF16 = jnp.bfloat16

HEAD = 64
D_CONV = 512
D_RWKV = 512
LORA_WA = 128
G_LORA = 128
GRID_W = 64
RMS_EPS = 1e-6
GN_EPS = 64e-5
DECAY_SCALE = math.exp(-0.5)

TM = 256
CHUNK = 64
CPT = TM // CHUNK
HPG = 2
GROUP = HPG * HEAD
VMEM_LIMIT = 56 * 1024 * 1024

NN = (((1,), (0,)), ((), ()))
NT = (((1,), (1,)), ((), ()))
TN = (((0,), (0,)), ((), ()))


def _silu(x):
    return x * jax.nn.sigmoid(x)


def _split(x):
    hi = x.astype(BF16)
    return hi, (x - hi.astype(F32)).astype(BF16)


def _dg(a, b, dims):
    return lax.dot_general(a, b, dims, preferred_element_type=F32)


def _mm3(a, b_hi, b_lo, dims):
    a_hi, a_lo = _split(a)
    return _dg(a_hi, b_hi, dims) + (_dg(a_hi, b_lo, dims) + _dg(a_lo, b_hi, dims))


def _rms(x):
    return x * lax.rsqrt(jnp.mean(x * x, axis=-1, keepdims=True) + RMS_EPS)


def _dot01(t, m01, left=False):
    hi, lo = _split(t)
    if left:
        return _dg(m01, hi, NN) + _dg(m01, lo, NN)
    return _dg(hi, m01, NN) + _dg(lo, m01, NN)


def _z_tile(za_ref, zb_ref, split):
    if not split:
        return za_ref[0]
    return jnp.where(pl.program_id(1) == 0, za_ref[0], zb_ref[0])


def _z_specs(split, D):
    if split:
        return [pl.BlockSpec((1, TM, D), lambda b, i: (b, 0, 0)),
                pl.BlockSpec((1, TM, D), lambda b, i: (b, jnp.maximum(i - 1, 0), 0))]
    return [pl.BlockSpec((1, TM, D), lambda b, i: (b, i, 0)),
            pl.BlockSpec((1, 8, D), lambda b, i: (0, 0, 0))]


def _ada_kernel(c_ref, w_ref, b_ref, o_ref):
    s = _silu(c_ref[...])
    w_hi, w_lo = _split(w_ref[0])
    o_ref[0] = _mm3(s, w_hi, w_lo, NN) + b_ref[0]


def _ada(c_rows, ada_w, ada_b):
    L, D, N = ada_w.shape
    R = c_rows.shape[0]
    NB = 1536
    return pl.pallas_call(
        _ada_kernel,
        grid=(L, N // NB),
        in_specs=[
            pl.BlockSpec((R, D), lambda l, n: (0, 0)),
            pl.BlockSpec((1, D, NB), lambda l, n: (l, 0, n)),
            pl.BlockSpec((1, 1, NB), lambda l, n: (l, 0, n)),
        ],
        out_specs=pl.BlockSpec((1, R, NB), lambda l, n: (l, 0, n)),
        out_shape=jax.ShapeDtypeStruct((L, R, N), F32),
        compiler_params=pltpu.CompilerParams(
            dimension_semantics=("arbitrary", "arbitrary"), vmem_limit_bytes=VMEM_LIMIT),
        name="ada",
    )(c_rows, ada_w, ada_b.reshape(L, 1, N))


def _proj_kernel(za_ref, zb_ref, m_ref, g_ref, win_ref, wup_ref, aup_ref, w0_ref, a0_ref, kk_ref, ka_ref,
                 rk_ref, gup_ref, ones_ref, tri_ref,
                 cb_ref, u_ref, v_ref, bonus_ref, gate_ref, at_ref, rt_ref, bt_ref, kt_ref, cl_ref,
                 *, split):
    m = m_ref[0, 0]
    h = _rms(_z_tile(za_ref, zb_ref, split)) * g_ref[...]
    h = (h * (1.0 + m[1:2]) + m[0:1]).astype(BF16)

    def proj(lo, width):
        return jnp.dot(h, win_ref[:, lo:lo + width], preferred_element_type=F32)

    base = 3 * D_CONV
    lora = proj(base + 3 * D_RWKV, LORA_WA + G_LORA)
    k = proj(base + D_RWKV, D_RWKV)
    r = proj(base, D_RWKV)
    v = proj(base + 2 * D_RWKV, D_RWKV)
    wa = lora[:, :LORA_WA]
    gl = lora[:, LORA_WA:]
    ones_bd = ones_ref[...]

    kraw = k * kk_ref[...]
    kkn = kraw * lax.rsqrt(jnp.maximum(_dot01(kraw * kraw, ones_bd), 1e-24))
    lw2 = -DECAY_SCALE * jax.nn.sigmoid(
        w0_ref[...] + jnp.dot(jnp.tanh(wa).astype(BF16), wup_ref[...], preferred_element_type=F32))
    a2 = jax.nn.sigmoid(
        a0_ref[...] + jnp.dot(wa.astype(BF16), aup_ref[...], preferred_element_type=F32))

    cb_ref[0] = proj(0, D_CONV)
    v_ref[0] = v.astype(BF16)
    bonus_ref[0] = _dot01(r * k * rk_ref[...], ones_bd) * v
    gate_ref[0] = jnp.dot(jax.nn.sigmoid(gl).astype(BF16), gup_ref[...], preferred_element_type=F32)

    zero_rows = jnp.zeros((8 - CPT, D_RWKV), F32)
    conv_in = []
    for d in range(2):
        conv_in.append(proj((1 + d) * D_CONV, D_CONV))
        lw = lw2[:, d * D_RWKV:(d + 1) * D_RWKV]
        a = a2[:, d * D_RWKV:(d + 1) * D_RWKV]
        c = _dot01(lw, tri_ref[d], left=True)
        e_neg = jnp.exp(-c)
        at_ref[d, 0] = (kkn * jnp.exp(c - lw)).astype(BF16)
        rt_ref[d, 0] = (r * jnp.exp(c)).astype(BF16)
        bt_ref[d, 0] = (kkn * a * e_neg).astype(BF16)
        kt_ref[d, 0] = (k * (1.0 + (a - 1.0) * ka_ref[...]) * e_neg).astype(BF16)
        last = CHUNK - 1 if d == 0 else 0
        cl_ref[d, 0, 0] = jnp.concatenate(
            [c[j * CHUNK + last:j * CHUNK + last + 1] for j in range(CPT)] + [zero_rows], axis=0)
    u_ref[0] = conv_in[0] * conv_in[1]


def _proj(za, zb, split, modsel, g1, win, wup, aup, w0, a0, k_k, k_a, r_k, gup, ones_bd, tri):
    B, D = za.shape[0], za.shape[2]
    T = za.shape[1] + zb.shape[1] if split else za.shape[1]
    NTL = T // TM
    P = win.shape[1]
    W = D_RWKV
    tok = pl.BlockSpec((1, TM, W), lambda b, i: (b, i, 0))
    tok2 = pl.BlockSpec((2, 1, TM, W), lambda b, i: (0, b, i, 0))
    const = lambda shape: pl.BlockSpec(shape, lambda b, i: (0,) * len(shape))
    f32_tok = jax.ShapeDtypeStruct((B, T, W), F32)
    bf_tok2 = jax.ShapeDtypeStruct((2, B, T, W), BF16)
    return pl.pallas_call(
        functools.partial(_proj_kernel, split=split),
        grid=(B, NTL),
        in_specs=_z_specs(split, D) + [
            pl.BlockSpec((1, 1, 8, D), lambda b, i: (b, jnp.minimum(i, 1), 0, 0)),
            const((1, D)), const((D, P)), const((LORA_WA, 2 * W)), const((LORA_WA, 2 * W)),
            const((1, 2 * W)), const((1, 2 * W)), const((1, W)), const((1, W)),
            const((1, W)), const((G_LORA, W)), const((W, W)), const((2, TM, TM)),
        ],
        out_specs=[tok, tok, tok, tok, tok, tok2, tok2, tok2, tok2,
                   pl.BlockSpec((2, 1, 1, 8, W), lambda b, i: (0, b, i, 0, 0))],
        out_shape=[f32_tok, f32_tok, jax.ShapeDtypeStruct((B, T, W), BF16), f32_tok, f32_tok,
                   bf_tok2, bf_tok2, bf_tok2, bf_tok2,
                   jax.ShapeDtypeStruct((2, B, NTL, 8, W), F32)],
        compiler_params=pltpu.CompilerParams(
            dimension_semantics=("parallel", "parallel"), vmem_limit_bytes=VMEM_LIMIT),
        name="proj",
    )(za, zb, modsel, g1, win, wup, aup, w0, a0, k_k, k_a, r_k, gup, ones_bd, tri)


def _mm(a, x, dims):
    return _dg(a.astype(BF16), x.astype(BF16), dims)


def _scan_chunks(chains, bdmask):
    C = CHUNK
    row = lax.broadcasted_iota(jnp.int32, (C, GROUP), 0)
    col = lax.broadcasted_iota(jnp.int32, (C, GROUP), 1) & (HEAD - 1)
    eye = (row == col).astype(F32)
    blk16 = (row >> 4) == (col >> 4)
    lane_head = lax.broadcasted_iota(jnp.int32, (HEAD, GROUP), 1) >> 6
    zero_bf = jnp.zeros((), BF16)

    def each(f, *lists):
        return [f(*a) for a in zip(*lists)]

    def bd(x):
        return jnp.where(bdmask, jnp.concatenate([x.astype(BF16)] * HPG, axis=0), zero_bf)

    def bd2(x, y, axis):
        return jnp.concatenate([bd(x), bd(y)], axis=axis)

    def rows(*xs):
        return jnp.concatenate([x.astype(BF16) for x in xs], axis=0)

    def diag_blocks(full):
        out = jnp.where(lane_head == 0, full[0:HEAD], 0.0)
        for hh in range(1, HPG):
            out = out + jnp.where(lane_head == hh, full[hh * HEAD:(hh + 1) * HEAD], 0.0)
        return out

    rev = [ch["rev"] for ch in chains]
    at, rt, bt, kt, v, p_c, s0 = ([ch[n] for ch in chains]
                                  for n in ("at", "rt", "bt", "kt", "v", "p_c", "s0"))
    strict = [(col > row) if rv else (col < row) for rv in rev]
    incl = [(col >= row) if rv else (col <= row) for rv in rev]
    bh = each(lambda x, p: x * p, bt, p_c)
    kh = each(lambda x, p: x * p, kt, p_c)

    abk = each(lambda a_, r_, x, y: _mm(rows(a_, r_), bd2(x, y, 0), NT), at, rt, bt, kt)
    lmat = each(lambda m_, x: jnp.where(m_, x[:C, :GROUP], 0.0), strict, abk)
    aak = each(lambda m_, x: jnp.where(m_, x[:C, GROUP:], 0.0), strict, abk)
    arb = each(lambda m_, x: jnp.where(m_, x[C:, :GROUP], 0.0), incl, abk)
    ark = each(lambda m_, x: jnp.where(m_, x[C:, GROUP:], 0.0), incl, abk)

    ld = each(lambda x: jnp.where(blk16, x, 0.0), lmat)
    lo = each(lambda x, y: x - y, lmat, ld)
    p1 = each(lambda x: eye - x, ld)
    l2 = each(lambda x: _mm(x, bd(x), NN), ld)
    t = each(lambda p, x: _mm(rows(p, x), bd(x), NN), p1, l2)
    p2 = each(lambda p, x: p + x[:C], p1, t)
    l4 = each(lambda x: x[C:], t)
    t = each(lambda p, x: _mm(rows(p, x), bd(x), NN), p2, l4)
    p3 = each(lambda p, x: p + x[:C], p2, t)
    l8 = each(lambda x: x[C:], t)
    dinv = each(lambda p, x: p + _mm(p, bd(x), NN), p3, l8)
    mm = each(lambda d_, x: _mm(d_, bd(x), NN), dinv, lo)
    mm2 = each(lambda x: _mm(x, bd(x), NN), mm)
    g = each(lambda x, x2: eye - x + x2 - _mm(x, bd(x2), NN), mm, mm2)
    tinv = each(lambda g_, d_: _mm(g_, bd(d_), NN), g, dinv)

    t = each(lambda x, y, v_: _mm(rows(x, y), bd(v_), NN), aak, ark, v)
    akv = each(lambda x: x[:C], t)
    arkv = each(lambda x: x[C:], t)
    t = each(lambda t_, x, y: _mm(t_, bd2(x, y, 1), NN), tinv, at, akv)
    wm = each(lambda x: x[:, :GROUP], t)
    um = each(lambda x: x[:, GROUP:], t)
    t = each(lambda a_, x, y: _mm(a_, bd2(x, y, 1), NN), arb, wm, um)
    qh = each(lambda r_, x: r_.astype(F32) - x[:, :GROUP], rt, t)
    yl = each(lambda y_, x: y_ - x[:, GROUP:], arkv, t)
    t = each(lambda q_, w_, s_: _mm(rows(q_, w_), bd(s_), NT), qh, wm, s0)
    y = each(lambda x, y_: x[:C] + y_, t, yl)
    ds = each(lambda v_, u_, x, k_, b_: diag_blocks(_mm(rows(v_, u_, x[C:]), rows(k_, -b_, -b_), TN)),
              v, um, t, kh, bh)
    s_new = each(lambda s_, p_, d_: s_ * p_ + d_, s0, p_c, ds)
    return list(zip(y, s_new))


def _scan_kernel(vf_ref, vb_ref, atf_ref, rtf_ref, btf_ref, ktf_ref, clf_ref,
                 atb_ref, rtb_ref, btb_ref, ktb_ref, clb_ref, yf_ref, yb_ref, s_ref, *, nctx, n_chunks):
    s = pl.program_id(1)

    @pl.when(s == 0)
    def _():
        s_ref[...] = jnp.zeros_like(s_ref)

    back = jnp.where(s < nctx, nctx - 1 - s, nctx + n_chunks - 1 - s)
    bdmask = (lax.broadcasted_iota(jnp.int32, (GROUP, GROUP), 0) >> 6) == \
             (lax.broadcasted_iota(jnp.int32, (GROUP, GROUP), 1) >> 6)
    dirs = ((False, s, vf_ref, atf_ref, rtf_ref, btf_ref, ktf_ref, clf_ref, yf_ref),
            (True, back, vb_ref, atb_ref, rtb_ref, btb_ref, ktb_ref, clb_ref, yb_ref))
    chains, sinks = [], []
    for bi in range(s_ref.shape[0]):
        for d, (rev, chunk, v_ref, at_ref, rt_ref, bt_ref, kt_ref, cl_ref, y_ref) in enumerate(dirs):
            p_c = jnp.exp(cl_ref[0, bi, 0, pl.ds(chunk % CPT, 1), :])
            for q in range(D_RWKV // GROUP):
                sl = slice(q * GROUP, (q + 1) * GROUP)
                chains.append(dict(rev=rev, at=at_ref[0, bi, :, sl], rt=rt_ref[0, bi, :, sl],
                                   bt=bt_ref[0, bi, :, sl], kt=kt_ref[0, bi, :, sl],
                                   v=v_ref[bi, :, sl], p_c=p_c[:, sl], s0=s_ref[bi, d, :, sl]))
                sinks.append((y_ref, bi, d, sl))
    for (y, s_new), (y_ref, bi, d, sl) in zip(_scan_chunks(chains, bdmask), sinks):
        y_ref[bi, :, sl] = y
        s_ref[bi, d, :, sl] = s_new


def _scan(v, at, rt, bt, kt, cl, ctx_len):
    B, T, W = v.shape
    C = CHUNK
    NC = T // C
    nctx = ctx_len // C

    def back(s):
        return jnp.where(s < nctx, nctx - 1 - s, nctx + NC - 1 - s)

    BB = 2 if B % 2 == 0 else 1
    fwd = pl.BlockSpec((BB, C, W), lambda bb, s: (bb, s, 0))
    bwd = pl.BlockSpec((BB, C, W), lambda bb, s: (bb, back(s), 0))
    fwd_d = pl.BlockSpec((1, BB, C, W), lambda bb, s: (0, bb, s, 0))
    bwd_d = pl.BlockSpec((1, BB, C, W), lambda bb, s: (1, bb, back(s), 0))
    fwd_c = pl.BlockSpec((1, BB, 1, 8, W), lambda bb, s: (0, bb, s // CPT, 0, 0))
    bwd_c = pl.BlockSpec((1, BB, 1, 8, W), lambda bb, s: (1, bb, back(s) // CPT, 0, 0))
    out = jax.ShapeDtypeStruct((B, T, W), F32)
    return pl.pallas_call(
        functools.partial(_scan_kernel, nctx=nctx, n_chunks=NC),
        grid=(B // BB, NC),
        in_specs=[fwd, bwd, fwd_d, fwd_d, fwd_d, fwd_d, fwd_c, bwd_d, bwd_d, bwd_d, bwd_d, bwd_c],
        out_specs=[fwd, bwd],
        out_shape=[out, out],
        scratch_shapes=[pltpu.VMEM((BB, 2, HEAD, W), F32)],
        compiler_params=pltpu.CompilerParams(
            dimension_semantics=("parallel", "arbitrary"), vmem_limit_bytes=VMEM_LIMIT),
        name="scan",
    )(v, v, at, rt, bt, kt, cl, at, rt, bt, kt, cl)


def _mix_kernel(za_ref, zb_ref, m_ref, cb_ref, u_ref, up_ref, un_ref, yf_ref, yb_ref, bonus_ref, gate_ref,
                cw_ref, lng_ref, lnb_ref, ones_ref, wout_ref, x1_ref, *, n_tiles, split):
    i = pl.program_id(1)
    m = m_ref[0, 0]
    prev_ok = (i >= 2).astype(F32)
    next_ok = jnp.logical_and(i >= 1, i <= n_tiles - 2).astype(F32)
    u = u_ref[0]
    rowi = lax.broadcasted_iota(jnp.int32, (TM, 1), 0)
    um1 = jnp.where(rowi == 0, up_ref[0][7:8] * prev_ok, pltpu.roll(u, 1, 0))
    up1 = jnp.where(rowi == TM - 1, un_ref[0][0:1] * next_ok, pltpu.roll(u, TM - 1, 0))
    cw = cw_ref[...]
    yconv = cb_ref[0] * (cw[0:1] * um1 + cw[1:2] * u + cw[2:3] * up1)

    ones_bd = ones_ref[...]
    y = yf_ref[0] + yb_ref[0]
    mu = _dot01(y, ones_bd) * (1.0 / HEAD)
    dlt = y - mu
    var = _dot01(dlt * dlt, ones_bd) * (1.0 / HEAD)
    yn = dlt * lax.rsqrt(var + GN_EPS) * lng_ref[...] + lnb_ref[...]
    yrw = (yn + bonus_ref[0]) * gate_ref[0]

    out = (jnp.dot(yconv.astype(BF16), wout_ref[0:D_CONV, :], preferred_element_type=F32)
           + jnp.dot(yrw.astype(BF16), wout_ref[D_CONV:D_CONV + D_RWKV, :], preferred_element_type=F32))
    x1_ref[0] = _z_tile(za_ref, zb_ref, split) + m[2:3] * out


def _mix(za, zb, split, modsel, cb, u, yf, yb, bonus, gate, cw, lng, lnb, ones_bd, wout):
    B, T, D = cb.shape[0], cb.shape[1], za.shape[2]
    NTL = T // TM
    tok = pl.BlockSpec((1, TM, D_RWKV), lambda b, i: (b, i, 0))
    const = lambda shape: pl.BlockSpec(shape, lambda b, i: (0,) * len(shape))
    R8 = TM // 8
    return pl.pallas_call(
        functools.partial(_mix_kernel, n_tiles=NTL, split=split),
        grid=(B, NTL),
        in_specs=_z_specs(split, D) + [
            pl.BlockSpec((1, 1, 8, D), lambda b, i: (b, jnp.minimum(i, 1), 0, 0)),
            tok, tok,
            pl.BlockSpec((1, 8, D_CONV), lambda b, i: (b, jnp.maximum(i * R8 - 1, 0), 0)),
            pl.BlockSpec((1, 8, D_CONV), lambda b, i: (b, jnp.minimum((i + 1) * R8, T // 8 - 1), 0)),
            tok, tok, tok, tok,
            const((8, D_CONV)), const((1, D_RWKV)), const((1, D_RWKV)), const((D_RWKV, D_RWKV)),
            const((D_CONV + D_RWKV, D)),
        ],
        out_specs=pl.BlockSpec((1, TM, D), lambda b, i: (b, i, 0)),
        out_shape=jax.ShapeDtypeStruct((B, T, D), F32),
        compiler_params=pltpu.CompilerParams(
            dimension_semantics=("parallel", "parallel"), vmem_limit_bytes=VMEM_LIMIT),
        name="mix",
    )(za, zb, modsel, cb, u, u, u, yf, yb, bonus, gate, cw, lng, lnb, ones_bd, wout)


def _ffn_kernel(x1_ref, xb_ref, m_ref, g2_ref, wup_ref, cw_ref, cbias_ref, wdn_ref, fg_ref,
                o_ref, gtop_ref, *, n_tiles, d_ff, final):
    i = pl.program_id(1)
    m = m_ref[0, 0]
    lat = (i >= 1).astype(F32)
    top_ok = i >= 2
    bot_ok = jnp.logical_and(i >= 1, i <= n_tiles - 2).astype(F32)
    rowi = lax.broadcasted_iota(jnp.int32, (TM, 1), 0)
    colp = jnp.where(i == 0, rowi, rowi & (GRID_W - 1))
    width = jnp.where(i == 0, TM, GRID_W)
    lmask = (colp > 0).astype(F32)
    rmask = (colp < width - 1).astype(F32)

    def norm2(x, keep):
        h = _rms(x) * g2_ref[...]
        return ((h * (1.0 + m[4:5]) + m[3:4]) * keep).astype(BF16)

    @pl.when(i == 0)
    def _():
        gtop_ref[...] = jnp.zeros_like(gtop_ref)

    x1 = x1_ref[0]
    hx = jnp.concatenate([norm2(x1, 1.0), norm2(xb_ref[0], bot_ok)], axis=0)
    hc = hx[0:TM]

    NB = 256

    def up(n):
        return (jnp.dot(hx, wup_ref[:, n:n + NB], preferred_element_type=F32),
                jnp.dot(hc, wup_ref[:, d_ff + n:d_ff + n + NB], preferred_element_type=F32))

    acc = jnp.zeros((TM, o_ref.shape[-1]), F32)
    nxt = up(0)
    prev = None
    for n in range(0, d_ff, NB):
        gfull, val = nxt
        if n + NB < d_ff:
            nxt = up(n + NB)
        if prev is not None:
            acc = acc + jnp.dot(prev, wdn_ref[n - NB:n, :], preferred_element_type=F32)
        g, gd = gfull[0:TM], gfull[GRID_W:GRID_W + TM]
        top = jnp.where(top_ok, gtop_ref[:, n:n + NB], 0.0)
        gu = jnp.concatenate([top, g[0:TM - GRID_W]], axis=0)
        gtop_ref[:, n:n + NB] = g[TM - GRID_W:TM]
        w = cw_ref[:, n:n + NB]
        wv = w * lat
        h0 = wv[0:1] * gu + w[3:4] * g + wv[6:7] * gd
        h1 = wv[1:2] * gu + w[4:5] * g + wv[7:8] * gd
        h2 = wv[2:3] * gu + w[5:6] * g + wv[8:9] * gd
        conv = (lmask * pltpu.roll(h0, 1, 0) + h1 + rmask * pltpu.roll(h2, TM - 1, 0)
                + cbias_ref[:, n:n + NB])
        prev = (_silu(conv) * val).astype(BF16)
    acc = acc + jnp.dot(prev, wdn_ref[d_ff - NB:d_ff, :], preferred_element_type=F32)
    x2 = x1 + m[5:6] * acc
    if final:
        x2 = _rms(x2) * fg_ref[...]
    o_ref[0] = x2


def _ffn(x1, modsel, g2, wup, cw, cbias, wdn, fg, final):
    B, T, D = x1.shape
    NTL = T // TM
    d_ff = wdn.shape[0]
    RW = TM // GRID_W
    const = lambda shape: pl.BlockSpec(shape, lambda b, i: (0,) * len(shape))
    if final:
        out_spec = pl.BlockSpec((1, TM, D), lambda b, i: (b, jnp.maximum(i - 1, 0), 0))
        out_shape = jax.ShapeDtypeStruct((B, T - TM, D), F32)
    else:
        out_spec = pl.BlockSpec((1, TM, D), lambda b, i: (b, i, 0))
        out_shape = jax.ShapeDtypeStruct((B, T, D), F32)
    return pl.pallas_call(
        functools.partial(_ffn_kernel, n_tiles=NTL, d_ff=d_ff, final=final),
        grid=(B, NTL),
        in_specs=[
            pl.BlockSpec((1, TM, D), lambda b, i: (b, i, 0)),
            pl.BlockSpec((1, GRID_W, D),
                         lambda b, i: (b, jnp.minimum((i + 1) * RW, T // GRID_W - 1), 0)),
            pl.BlockSpec((1, 1, 8, D), lambda b, i: (b, jnp.minimum(i, 1), 0, 0)),
            const((1, D)), const((D, 2 * d_ff)),
            const((16, d_ff)), const((1, d_ff)), const((d_ff, D)), const((1, D)),
        ],
        out_specs=out_spec,
        out_shape=out_shape,
        scratch_shapes=[pltpu.VMEM((GRID_W, d_ff), F32)],
        compiler_params=pltpu.CompilerParams(
            dimension_semantics=("parallel", "arbitrary"), vmem_limit_bytes=VMEM_LIMIT),
        name="ffn",
    )(x1, x1, modsel, g2, wup, cw, cbias, wdn, fg)


def kernel(x, c, ctx, c_ctx, ada_w, ada_b, norm1_g, norm2_g, w_in, conv_a_w, rw_w0, rw_w_up, rw_a0,
           rw_a_up, rw_k_k, rw_k_a, rw_r_k, rw_g_up, rw_ln_g, rw_ln_b, w_out, ffn_w_up, ffn_conv_w,
           ffn_conv_b, ffn_w_down, final_g):
    B, SEQ, D = x.shape
    CTX = ctx.shape[1]
    L = w_in.shape[0]
    d_ff = ffn_w_down.shape[1]
    assert CTX == TM and SEQ % TM == 0 and TM % GRID_W == 0 and CHUNK == HEAD and CPT <= 8
    assert w_in.shape[2] == 3 * D_CONV + 3 * D_RWKV + LORA_WA + G_LORA and d_ff % 256 == 0

    rows = -(-(B + 1) // 8) * 8
    c_rows = jnp.zeros((rows, D), F32).at[:B].set(c).at[B].set(c_ctx)
    mod = _ada(c_rows, ada_w, ada_b)

    hw = LORA_WA // 2
    zpad = jnp.zeros((L, hw, 2 * D_RWKV), F32)
    both = lambda t: jnp.concatenate([t[:, 0], t[:, 1]], axis=-1)
    wup_ext = jnp.concatenate([both(rw_w_up), zpad], axis=1).astype(BF16)
    aup_ext = jnp.concatenate([zpad, both(rw_a_up)], axis=1).astype(BF16)
    w0_2 = both(rw_w0[:, :, None, :])
    a0_2 = both(rw_a0[:, :, None, :])
    head_of = jnp.arange(D_RWKV) // HEAD
    ones_bd = (head_of[:, None] == head_of[None, :]).astype(BF16)
    tt = jnp.arange(TM)
    same = (tt[:, None] // CHUNK) == (tt[None, :] // CHUNK)
    tri = jnp.stack([same & (tt[None, :] <= tt[:, None]),
                     same & (tt[None, :] >= tt[:, None])]).astype(BF16)
    cw_a = jnp.zeros((L, 8, D_CONV), F32).at[:, :3].set(conv_a_w)
    cw_f = jnp.zeros((L, 16, d_ff), F32).at[:, :9].set(ffn_conv_w.reshape(L, 9, d_ff))

    za, zb = ctx, x
    for l in range(L):
        split = l == 0
        lat = mod[l, :B].reshape(B, 6, D)
        cm = jnp.broadcast_to(mod[l, B].reshape(1, 6, D), (B, 6, D))
        modsel = jnp.zeros((B, 2, 8, D), F32).at[:, 0, :6].set(cm).at[:, 1, :6].set(lat)

        cb, u, v, bonus, gate, at, rt, bt, kt, cl = _proj(
            za, zb, split, modsel, norm1_g[l].reshape(1, D), w_in[l].astype(BF16), wup_ext[l], aup_ext[l],
            w0_2[l], a0_2[l], rw_k_k[l].reshape(1, -1), rw_k_a[l].reshape(1, -1),
            rw_r_k[l].reshape(1, -1), rw_g_up[l].astype(BF16), ones_bd, tri)
        yf, yb = _scan(v, at, rt, bt, kt, cl, CTX)
        x1 = _mix(
            za, zb, split, modsel, cb, u, yf, yb, bonus, gate, cw_a[l], rw_ln_g[l].reshape(1, -1),
            rw_ln_b[l].reshape(1, -1), ones_bd, w_out[l].astype(BF16))
        za = zb = _ffn(x1, modsel, norm2_g[l].reshape(1, D), ffn_w_up[l].astype(BF16), cw_f[l],
                       ffn_conv_b[l].reshape(1, -1), ffn_w_down[l].astype(BF16), final_g.reshape(1, D),
                       final=(l == L - 1))
    return za
```

```python
import functools
import math

import jax
import jax.numpy as jnp
from jax import lax
from jax.experimental import pallas as pl
from jax.experimental.pallas import tpu as pltpu

F32 = jnp.float32
BF16 = jnp.bfloat16

HEAD = 64
D_CONV = 512
D_RWKV = 512
LORA_WA = 128
G_LORA = 128
GRID_W = 64
RMS_EPS = 1e-6
GN_EPS = 64e-5
DECAY_SCALE = math.exp(-0.5)

TM = 256
CHUNK = 64
CPT = TM // CHUNK
HPG = 2
GROUP = HPG * HEAD
VMEM_LIMIT = 56 * 1024 * 1024

NN = (((1,), (0,)), ((), ()))
NT = (((1,), (1,)), ((), ()))
TN = (((0,), (0,)), ((), ()))


def _silu(x):
    return x * jax.nn.sigmoid(x)


def _split(x):
    hi = x.astype(BF16)
    return hi, (x - hi.astype(F32)).astype(BF16)


def _dg(a, b, dims):
    return lax.dot_general(a, b, dims, preferred_element_type=F32)


def _mm3(a, b_hi, b_lo, dims):
    a_hi, a_lo = _split(a)
    return _dg(a_hi, b_hi, dims) + (_dg(a_hi, b_lo, dims) + _dg(a_lo, b_hi, dims))


def _rms(x):
    return x * lax.rsqrt(jnp.mean(x * x, axis=-1, keepdims=True) + RMS_EPS)


def _dot01(t, m01, left=False):
    hi, lo = _split(t)
    if left:
        return _dg(m01, hi, NN) + _dg(m01, lo, NN)
    return _dg(hi, m01, NN) + _dg(lo, m01, NN)


def _z_tile(za_ref, zb_ref, split):
    if not split:
        return za_ref[0]
    return jnp.where(pl.program_id(1) == 0, za_ref[0], zb_ref[0])


def _z_specs(split, D):
    if split:
        return [pl.BlockSpec((1, TM, D), lambda b, i: (b, 0, 0)),
                pl.BlockSpec((1, TM, D), lambda b, i: (b, jnp.maximum(i - 1, 0), 0))]
    return [pl.BlockSpec((1, TM, D), lambda b, i: (b, i, 0)),
            pl.BlockSpec((1, 8, D), lambda b, i: (0, 0, 0))]


def _ada_kernel(c_ref, w_ref, b_ref, o_ref):
    s = _silu(c_ref[...])
    w_hi, w_lo = _split(w_ref[0])
    o_ref[0] = _mm3(s, w_hi, w_lo, NN) + b_ref[0]


def _ada(c_rows, ada_w, ada_b):
    L, D, N = ada_w.shape
    R = c_rows.shape[0]
    NB = 1536
    return pl.pallas_call(
        _ada_kernel,
        grid=(L, N // NB),
        in_specs=[
            pl.BlockSpec((R, D), lambda l, n: (0, 0)),
            pl.BlockSpec((1, D, NB), lambda l, n: (l, 0, n)),
            pl.BlockSpec((1, 1, NB), lambda l, n: (l, 0, n)),
        ],
        out_specs=pl.BlockSpec((1, R, NB), lambda l, n: (l, 0, n)),
        out_shape=jax.ShapeDtypeStruct((L, R, N), F32),
        compiler_params=pltpu.CompilerParams(
            dimension_semantics=("arbitrary", "arbitrary"), vmem_limit_bytes=VMEM_LIMIT),
        name="ada",
    )(c_rows, ada_w, ada_b.reshape(L, 1, N))


def _proj_kernel(za_ref, zb_ref, m_ref, g_ref, win_ref, wup_ref, aup_ref, w0_ref, a0_ref, kk_ref, ka_ref,
                 rk_ref, gup_ref, ones_ref, tri_ref,
                 cb_ref, u_ref, v_ref, bonus_ref, gate_ref, at_ref, rt_ref, bt_ref, kt_ref, cl_ref,
                 *, split):
    m = m_ref[0, 0]
    h = _rms(_z_tile(za_ref, zb_ref, split)) * g_ref[...]
    h = (h * (1.0 + m[1:2]) + m[0:1]).astype(BF16)

    def proj(lo, width):
        return jnp.dot(h, win_ref[:, lo:lo + width], preferred_element_type=F32)

    base = 3 * D_CONV
    lora = proj(base + 3 * D_RWKV, LORA_WA + G_LORA)
    k = proj(base + D_RWKV, D_RWKV)
    r = proj(base, D_RWKV)
    v = proj(base + 2 * D_RWKV, D_RWKV)
    wa = lora[:, :LORA_WA]
    gl = lora[:, LORA_WA:]
    ones_bd = ones_ref[...]

    kraw = k * kk_ref[...]
    kkn = kraw * lax.rsqrt(jnp.maximum(_dot01(kraw * kraw, ones_bd), 1e-24))
    lw2 = -DECAY_SCALE * jax.nn.sigmoid(
        w0_ref[...] + jnp.dot(jnp.tanh(wa).astype(BF16), wup_ref[...], preferred_element_type=F32))
    a2 = jax.nn.sigmoid(
        a0_ref[...] + jnp.dot(wa.astype(BF16), aup_ref[...], preferred_element_type=F32))

    cb_ref[0] = proj(0, D_CONV)
    v_ref[0] = v.astype(BF16)
    bonus_ref[0] = _dot01(r * k * rk_ref[...], ones_bd) * v
    gate_ref[0] = jnp.dot(jax.nn.sigmoid(gl).astype(BF16), gup_ref[...], preferred_element_type=F32)

    zero_rows = jnp.zeros((8 - CPT, D_RWKV), F32)
    conv_in = []
    for d in range(2):
        conv_in.append(proj((1 + d) * D_CONV, D_CONV))
        lw = lw2[:, d * D_RWKV:(d + 1) * D_RWKV]
        a = a2[:, d * D_RWKV:(d + 1) * D_RWKV]
        c = _dot01(lw, tri_ref[d], left=True)
        e_neg = jnp.exp(-c)
        at_ref[d, 0] = (kkn * jnp.exp(c - lw)).astype(BF16)
        rt_ref[d, 0] = (r * jnp.exp(c)).astype(BF16)
        bt_ref[d, 0] = (kkn * a * e_neg).astype(BF16)
        kt_ref[d, 0] = (k * (1.0 + (a - 1.0) * ka_ref[...]) * e_neg).astype(BF16)
        last = CHUNK - 1 if d == 0 else 0
        cl_ref[d, 0, 0] = jnp.concatenate(
            [c[j * CHUNK + last:j * CHUNK + last + 1] for j in range(CPT)] + [zero_rows], axis=0)
    u_ref[0] = conv_in[0] * conv_in[1]


def _proj(za, zb, split, modsel, g1, win, wup, aup, w0, a0, k_k, k_a, r_k, gup, ones_bd, tri):
    B, D = za.shape[0], za.shape[2]
    T = za.shape[1] + zb.shape[1] if split else za.shape[1]
    NTL = T // TM
    P = win.shape[1]
    W = D_RWKV
    tok = pl.BlockSpec((1, TM, W), lambda b, i: (b, i, 0))
    tok2 = pl.BlockSpec((2, 1, TM, W), lambda b, i: (0, b, i, 0))
    const = lambda shape: pl.BlockSpec(shape, lambda b, i: (0,) * len(shape))
    f32_tok = jax.ShapeDtypeStruct((B, T, W), F32)
    bf_tok2 = jax.ShapeDtypeStruct((2, B, T, W), BF16)
    return pl.pallas_call(
        functools.partial(_proj_kernel, split=split),
        grid=(B, NTL),
        in_specs=_z_specs(split, D) + [
            pl.BlockSpec((1, 1, 8, D), lambda b, i: (b, jnp.minimum(i, 1), 0, 0)),
            const((1, D)), const((D, P)), const((LORA_WA, 2 * W)), const((LORA_WA, 2 * W)),
            const((1, 2 * W)), const((1, 2 * W)), const((1, W)), const((1, W)),
            const((1, W)), const((G_LORA, W)), const((W, W)), const((2, TM, TM)),
        ],
        out_specs=[tok, tok, tok, tok, tok, tok2, tok2, tok2, tok2,
                   pl.BlockSpec((2, 1, 1, 8, W), lambda b, i: (0, b, i, 0, 0))],
        out_shape=[f32_tok, f32_tok, jax.ShapeDtypeStruct((B, T, W), BF16), f32_tok, f32_tok,
                   bf_tok2, bf_tok2, bf_tok2, bf_tok2,
                   jax.ShapeDtypeStruct((2, B, NTL, 8, W), F32)],
        compiler_params=pltpu.CompilerParams(
            dimension_semantics=("parallel", "parallel"), vmem_limit_bytes=VMEM_LIMIT),
        name="proj",
    )(za, zb, modsel, g1, win, wup, aup, w0, a0, k_k, k_a, r_k, gup, ones_bd, tri)


def _mm(a, x, dims):
    return _dg(a.astype(BF16), x.astype(BF16), dims)


def _scan_chunks(chains, bdmask):
    C = CHUNK
    row = lax.broadcasted_iota(jnp.int32, (C, GROUP), 0)
    col = lax.broadcasted_iota(jnp.int32, (C, GROUP), 1) & (HEAD - 1)
    eye = (row == col).astype(F32)
    blk16 = (row >> 4) == (col >> 4)
    lane_head = lax.broadcasted_iota(jnp.int32, (HEAD, GROUP), 1) >> 6
    zero_bf = jnp.zeros((), BF16)

    def each(f, *lists):
        return [f(*a) for a in zip(*lists)]

    def bd(x):
        return jnp.where(bdmask, jnp.concatenate([x.astype(BF16)] * HPG, axis=0), zero_bf)

    def bd2(x, y, axis):
        return jnp.concatenate([bd(x), bd(y)], axis=axis)

    def rows(*xs):
        return jnp.concatenate([x.astype(BF16) for x in xs], axis=0)

    def diag_blocks(full):
        out = jnp.where(lane_head == 0, full[0:HEAD], 0.0)
        for hh in range(1, HPG):
            out = out + jnp.where(lane_head == hh, full[hh * HEAD:(hh + 1) * HEAD], 0.0)
        return out

    rev = [ch["rev"] for ch in chains]
    at, rt, bt, kt, v, p_c, s0 = ([ch[n] for ch in chains]
                                  for n in ("at", "rt", "bt", "kt", "v", "p_c", "s0"))
    strict = [(col > row) if rv else (col < row) for rv in rev]
    incl = [(col >= row) if rv else (col <= row) for rv in rev]
    bh = each(lambda x, p: x * p, bt, p_c)
    kh = each(lambda x, p: x * p, kt, p_c)

    abk = each(lambda a_, r_, x, y: _mm(rows(a_, r_), bd2(x, y, 0), NT), at, rt, bt, kt)
    lmat = each(lambda m_, x: jnp.where(m_, x[:C, :GROUP], 0.0), strict, abk)
    aak = each(lambda m_, x: jnp.where(m_, x[:C, GROUP:], 0.0), strict, abk)
    arb = each(lambda m_, x: jnp.where(m_, x[C:, :GROUP], 0.0), incl, abk)
    ark = each(lambda m_, x: jnp.where(m_, x[C:, GROUP:], 0.0), incl, abk)

    S = 16
    lane16 = (lax.broadcasted_iota(jnp.int32, (S, GROUP), 1) & (HEAD - 1)) >> 4
    eye_p = (lax.broadcasted_iota(jnp.int32, (S, GROUP), 0)
             == (lax.broadcasted_iota(jnp.int32, (S, GROUP), 1) & (S - 1))).astype(F32)
    mask16 = (lax.broadcasted_iota(jnp.int32, (GROUP, GROUP), 0) >> 4) == \
             (lax.broadcasted_iota(jnp.int32, (GROUP, GROUP), 1) >> 4)

    def pack16(full):
        out = jnp.where(lane16 == 0, full[0:S], 0.0)
        for j in range(1, HEAD // S):
            out = out + jnp.where(lane16 == j, full[j * S:(j + 1) * S], 0.0)
        return out

    def unpack16(p):
        return jnp.concatenate([jnp.where(lane16 == j, p, 0.0) for j in range(HEAD // S)], axis=0)

    def bd16(p):
        return jnp.where(mask16, jnp.concatenate([p.astype(BF16)] * (GROUP // S), axis=0), zero_bf)

    ld = each(pack16, lmat)
    lo = each(lambda x: jnp.where(blk16, 0.0, x), lmat)
    p1 = each(lambda x: eye_p - x, ld)
    l2 = each(lambda x: _mm(x, bd16(x), NN), ld)
    t = each(lambda p, x: _mm(rows(p, x), bd16(x), NN), p1, l2)
    p2 = each(lambda p, x: p + x[:S], p1, t)
    l4 = each(lambda x: x[S:], t)
    t = each(lambda p, x: _mm(rows(p, x), bd16(x), NN), p2, l4)
    p3 = each(lambda p, x: p + x[:S], p2, t)
    l8 = each(lambda x: x[S:], t)
    dinv = each(lambda p, x: unpack16(p + _mm(p, bd16(x), NN)), p3, l8)
    mm = each(lambda d_, x: _mm(d_, bd(x), NN), dinv, lo)
    mm2 = each(lambda x: _mm(x, bd(x), NN), mm)
    g = each(lambda x, x2: eye - x + x2 - _mm(x, bd(x2), NN), mm, mm2)
    tinv = each(lambda g_, d_: _mm(g_, bd(d_), NN), g, dinv)

    t = each(lambda x, y, v_: _mm(rows(x, y), bd(v_), NN), aak, ark, v)
    akv = each(lambda x: x[:C], t)
    arkv = each(lambda x: x[C:], t)
    t = each(lambda t_, x, y: _mm(t_, bd2(x, y, 1), NN), tinv, at, akv)
    wm = each(lambda x: x[:, :GROUP], t)
    um = each(lambda x: x[:, GROUP:], t)
    t = each(lambda a_, x, y: _mm(a_, bd2(x, y, 1), NN), arb, wm, um)
    qh = each(lambda r_, x: r_.astype(F32) - x[:, :GROUP], rt, t)
    yl = each(lambda y_, x: y_ - x[:, GROUP:], arkv, t)
    t = each(lambda q_, w_, s_: _mm(rows(q_, w_), bd(s_), NT), qh, wm, s0)
    y = each(lambda x, y_: x[:C] + y_, t, yl)
    ds = each(lambda v_, u_, x, k_, b_: diag_blocks(_mm(rows(v_, u_, x[C:]), rows(k_, -b_, -b_), TN)),
              v, um, t, kh, bh)
    s_new = each(lambda s_, p_, d_: s_ * p_ + d_, s0, p_c, ds)
    return list(zip(y, s_new))


def _scan_kernel(vf_ref, vb_ref, atf_ref, rtf_ref, btf_ref, ktf_ref, clf_ref,
                 atb_ref, rtb_ref, btb_ref, ktb_ref, clb_ref, yf_ref, yb_ref, s_ref, *, nctx, n_chunks):
    s = pl.program_id(1)

    @pl.when(s == 0)
    def _():
        s_ref[...] = jnp.zeros_like(s_ref)

    back = jnp.where(s < nctx, nctx - 1 - s, nctx + n_chunks - 1 - s)
    bdmask = (lax.broadcasted_iota(jnp.int32, (GROUP, GROUP), 0) >> 6) == \
             (lax.broadcasted_iota(jnp.int32, (GROUP, GROUP), 1) >> 6)
    dirs = ((False, s, vf_ref, atf_ref, rtf_ref, btf_ref, ktf_ref, clf_ref, yf_ref),
            (True, back, vb_ref, atb_ref, rtb_ref, btb_ref, ktb_ref, clb_ref, yb_ref))
    chains, sinks = [], []
    for bi in range(s_ref.shape[0]):
        for d, (rev, chunk, v_ref, at_ref, rt_ref, bt_ref, kt_ref, cl_ref, y_ref) in enumerate(dirs):
            p_c = jnp.exp(cl_ref[0, bi, 0, pl.ds(chunk % CPT, 1), :])
            for q in range(D_RWKV // GROUP):
                sl = slice(q * GROUP, (q + 1) * GROUP)
                chains.append(dict(rev=rev, at=at_ref[0, bi, :, sl], rt=rt_ref[0, bi, :, sl],
                                   bt=bt_ref[0, bi, :, sl], kt=kt_ref[0, bi, :, sl],
                                   v=v_ref[bi, :, sl], p_c=p_c[:, sl], s0=s_ref[bi, d, :, sl]))
                sinks.append((y_ref, bi, d, sl))
    for (y, s_new), (y_ref, bi, d, sl) in zip(_scan_chunks(chains, bdmask), sinks):
        y_ref[bi, :, sl] = y
        s_ref[bi, d, :, sl] = s_new


def _scan(v, at, rt, bt, kt, cl, ctx_len):
    B, T, W = v.shape
    C = CHUNK
    NC = T // C
    nctx = ctx_len // C

    def back(s):
        return jnp.where(s < nctx, nctx - 1 - s, nctx + NC - 1 - s)

    BB = 4 if B % 4 == 0 else 2 if B % 2 == 0 else 1
    fwd = pl.BlockSpec((BB, C, W), lambda bb, s: (bb, s, 0))
    bwd = pl.BlockSpec((BB, C, W), lambda bb, s: (bb, back(s), 0))
    fwd_d = pl.BlockSpec((1, BB, C, W), lambda bb, s: (0, bb, s, 0))
    bwd_d = pl.BlockSpec((1, BB, C, W), lambda bb, s: (1, bb, back(s), 0))
    fwd_c = pl.BlockSpec((1, BB, 1, 8, W), lambda bb, s: (0, bb, s // CPT, 0, 0))
    bwd_c = pl.BlockSpec((1, BB, 1, 8, W), lambda bb, s: (1, bb, back(s) // CPT, 0, 0))
    out = jax.ShapeDtypeStruct((B, T, W), F32)
    return pl.pallas_call(
        functools.partial(_scan_kernel, nctx=nctx, n_chunks=NC),
        grid=(B // BB, NC),
        in_specs=[fwd, bwd, fwd_d, fwd_d, fwd_d, fwd_d, fwd_c, bwd_d, bwd_d, bwd_d, bwd_d, bwd_c],
        out_specs=[fwd, bwd],
        out_shape=[out, out],
        scratch_shapes=[pltpu.VMEM((BB, 2, HEAD, W), F32)],
        compiler_params=pltpu.CompilerParams(
            dimension_semantics=("parallel", "arbitrary"), vmem_limit_bytes=VMEM_LIMIT),
        name="scan",
    )(v, v, at, rt, bt, kt, cl, at, rt, bt, kt, cl)


def _mix_kernel(za_ref, zb_ref, m_ref, cb_ref, u_ref, up_ref, un_ref, yf_ref, yb_ref, bonus_ref, gate_ref,
                cw_ref, lng_ref, lnb_ref, ones_ref, wout_ref, x1_ref, *, n_tiles, split):
    i = pl.program_id(1)
    m = m_ref[0, 0]
    prev_ok = (i >= 2).astype(F32)
    next_ok = jnp.logical_and(i >= 1, i <= n_tiles - 2).astype(F32)
    u = u_ref[0]
    rowi = lax.broadcasted_iota(jnp.int32, (TM, 1), 0)
    um1 = jnp.where(rowi == 0, up_ref[0][7:8] * prev_ok, pltpu.roll(u, 1, 0))
    up1 = jnp.where(rowi == TM - 1, un_ref[0][0:1] * next_ok, pltpu.roll(u, TM - 1, 0))
    cw = cw_ref[...]
    yconv = cb_ref[0] * (cw[0:1] * um1 + cw[1:2] * u + cw[2:3] * up1)

    ones_bd = ones_ref[...]
    y = yf_ref[0] + yb_ref[0]
    mu = _dot01(y, ones_bd) * (1.0 / HEAD)
    dlt = y - mu
    var = _dot01(dlt * dlt, ones_bd) * (1.0 / HEAD)
    yn = dlt * lax.rsqrt(var + GN_EPS) * lng_ref[...] + lnb_ref[...]
    yrw = (yn + bonus_ref[0]) * gate_ref[0]

    out = (jnp.dot(yconv.astype(BF16), wout_ref[0:D_CONV, :], preferred_element_type=F32)
           + jnp.dot(yrw.astype(BF16), wout_ref[D_CONV:D_CONV + D_RWKV, :], preferred_element_type=F32))
    x1_ref[0] = _z_tile(za_ref, zb_ref, split) + m[2:3] * out


def _mix(za, zb, split, modsel, cb, u, yf, yb, bonus, gate, cw, lng, lnb, ones_bd, wout):
    B, T, D = cb.shape[0], cb.shape[1], za.shape[2]
    NTL = T // TM
    tok = pl.BlockSpec((1, TM, D_RWKV), lambda b, i: (b, i, 0))
    const = lambda shape: pl.BlockSpec(shape, lambda b, i: (0,) * len(shape))
    R8 = TM // 8
    return pl.pallas_call(
        functools.partial(_mix_kernel, n_tiles=NTL, split=split),
        grid=(B, NTL),
        in_specs=_z_specs(split, D) + [
            pl.BlockSpec((1, 1, 8, D), lambda b, i: (b, jnp.minimum(i, 1), 0, 0)),
            tok, tok,
            pl.BlockSpec((1, 8, D_CONV), lambda b, i: (b, jnp.maximum(i * R8 - 1, 0), 0)),
            pl.BlockSpec((1, 8, D_CONV), lambda b, i: (b, jnp.minimum((i + 1) * R8, T // 8 - 1), 0)),
            tok, tok, tok, tok,
            const((8, D_CONV)), const((1, D_RWKV)), const((1, D_RWKV)), const((D_RWKV, D_RWKV)),
            const((D_CONV + D_RWKV, D)),
        ],
        out_specs=pl.BlockSpec((1, TM, D), lambda b, i: (b, i, 0)),
        out_shape=jax.ShapeDtypeStruct((B, T, D), F32),
        compiler_params=pltpu.CompilerParams(
            dimension_semantics=("parallel", "parallel"), vmem_limit_bytes=VMEM_LIMIT),
        name="mix",
    )(za, zb, modsel, cb, u, u, u, yf, yb, bonus, gate, cw, lng, lnb, ones_bd, wout)


def _ffn_kernel(x1_ref, xb_ref, m_ref, g2_ref, wup_ref, cw_ref, cbias_ref, wdn_ref, fg_ref,
                o_ref, gtop_ref, *, n_tiles, d_ff, final):
    i = pl.program_id(1)
    m = m_ref[0, 0]
    lat = (i >= 1).astype(F32)
    top_ok = i >= 2
    bot_ok = jnp.logical_and(i >= 1, i <= n_tiles - 2).astype(F32)
    rowi = lax.broadcasted_iota(jnp.int32, (TM, 1), 0)
    colp = jnp.where(i == 0, rowi, rowi & (GRID_W - 1))
    width = jnp.where(i == 0, TM, GRID_W)
    lmask = (colp > 0).astype(F32)
    rmask = (colp < width - 1).astype(F32)

    def norm2(x, keep):
        h = _rms(x) * g2_ref[...]
        return ((h * (1.0 + m[4:5]) + m[3:4]) * keep).astype(BF16)

    @pl.when(i == 0)
    def _():
        gtop_ref[...] = jnp.zeros_like(gtop_ref)

    x1 = x1_ref[0]
    hx = jnp.concatenate([norm2(x1, 1.0), norm2(xb_ref[0], bot_ok)], axis=0)
    hc = hx[0:TM]

    NB = 256

    def up(n):
        return (jnp.dot(hx, wup_ref[:, n:n + NB], preferred_element_type=F32),
                jnp.dot(hc, wup_ref[:, d_ff + n:d_ff + n + NB], preferred_element_type=F32))

    acc = jnp.zeros((TM, o_ref.shape[-1]), F32)
    nxt = up(0)
    prev = None
    for n in range(0, d_ff, NB):
        gfull, val = nxt
        if n + NB < d_ff:
            nxt = up(n + NB)
        if prev is not None:
            acc = acc + jnp.dot(prev, wdn_ref[n - NB:n, :], preferred_element_type=F32)
        g, gd = gfull[0:TM], gfull[GRID_W:GRID_W + TM]
        top = jnp.where(top_ok, gtop_ref[:, n:n + NB], 0.0)
        gu = jnp.concatenate([top, g[0:TM - GRID_W]], axis=0)
        gtop_ref[:, n:n + NB] = g[TM - GRID_W:TM]
        w = cw_ref[:, n:n + NB]
        wv = w * lat
        h0 = wv[0:1] * gu + w[3:4] * g + wv[6:7] * gd
        h1 = wv[1:2] * gu + w[4:5] * g + wv[7:8] * gd
        h2 = wv[2:3] * gu + w[5:6] * g + wv[8:9] * gd
        conv = (lmask * pltpu.roll(h0, 1, 0) + h1 + rmask * pltpu.roll(h2, TM - 1, 0)
                + cbias_ref[:, n:n + NB])
        prev = (_silu(conv) * val).astype(BF16)
    acc = acc + jnp.dot(prev, wdn_ref[d_ff - NB:d_ff, :], preferred_element_type=F32)
    x2 = x1 + m[5:6] * acc
    if final:
        x2 = _rms(x2) * fg_ref[...]
    o_ref[0] = x2


def _ffn(x1, modsel, g2, wup, cw, cbias, wdn, fg, final):
    B, T, D = x1.shape
    NTL = T // TM
    d_ff = wdn.shape[0]
    RW = TM // GRID_W
    const = lambda shape: pl.BlockSpec(shape, lambda b, i: (0,) * len(shape))
    if final:
        out_spec = pl.BlockSpec((1, TM, D), lambda b, i: (b, jnp.maximum(i - 1, 0), 0))
        out_shape = jax.ShapeDtypeStruct((B, T - TM, D), F32)
    else:
        out_spec = pl.BlockSpec((1, TM, D), lambda b, i: (b, i, 0))
        out_shape = jax.ShapeDtypeStruct((B, T, D), F32)
    return pl.pallas_call(
        functools.partial(_ffn_kernel, n_tiles=NTL, d_ff=d_ff, final=final),
        grid=(B, NTL),
        in_specs=[
            pl.BlockSpec((1, TM, D), lambda b, i: (b, i, 0)),
            pl.BlockSpec((1, GRID_W, D),
                         lambda b, i: (b, jnp.minimum((i + 1) * RW, T // GRID_W - 1), 0)),
            pl.BlockSpec((1, 1, 8, D), lambda b, i: (b, jnp.minimum(i, 1), 0, 0)),
            const((1, D)), const((D, 2 * d_ff)),
            const((16, d_ff)), const((1, d_ff)), const((d_ff, D)), const((1, D)),
        ],
        out_specs=out_spec,
        out_shape=out_shape,
        scratch_shapes=[pltpu.VMEM((GRID_W, d_ff), F32)],
        compiler_params=pltpu.CompilerParams(
            dimension_semantics=("parallel", "arbitrary"), vmem_limit_bytes=VMEM_LIMIT),
        name="ffn",
    )(x1, x1, modsel, g2, wup, cw, cbias, wdn, fg)


def kernel(x, c, ctx, c_ctx, ada_w, ada_b, norm1_g, norm2_g, w_in, conv_a_w, rw_w0, rw_w_up, rw_a0,
           rw_a_up, rw_k_k, rw_k_a, rw_r_k, rw_g_up, rw_ln_g, rw_ln_b, w_out, ffn_w_up, ffn_conv_w,
           ffn_conv_b, ffn_w_down, final_g):
    B, SEQ, D = x.shape
    CTX = ctx.shape[1]
    L = w_in.shape[0]
    d_ff = ffn_w_down.shape[1]
    assert CTX == TM and SEQ % TM == 0 and TM % GRID_W == 0 and CHUNK == HEAD and CPT <= 8
    assert w_in.shape[2] == 3 * D_CONV + 3 * D_RWKV + LORA_WA + G_LORA and d_ff % 256 == 0

    rows = -(-(B + 1) // 8) * 8
    c_rows = jnp.zeros((rows, D), F32).at[:B].set(c).at[B].set(c_ctx)
    mod = _ada(c_rows, ada_w, ada_b)

    hw = LORA_WA // 2
    zpad = jnp.zeros((L, hw, 2 * D_RWKV), F32)
    both = lambda t: jnp.concatenate([t[:, 0], t[:, 1]], axis=-1)
    wup_ext = jnp.concatenate([both(rw_w_up), zpad], axis=1).astype(BF16)
    aup_ext = jnp.concatenate([zpad, both(rw_a_up)], axis=1).astype(BF16)
    w0_2 = both(rw_w0[:, :, None, :])
    a0_2 = both(rw_a0[:, :, None, :])
    head_of = jnp.arange(D_RWKV) // HEAD
    ones_bd = (head_of[:, None] == head_of[None, :]).astype(BF16)
    tt = jnp.arange(TM)
    same = (tt[:, None] // CHUNK) == (tt[None, :] // CHUNK)
    tri = jnp.stack([same & (tt[None, :] <= tt[:, None]),
                     same & (tt[None, :] >= tt[:, None])]).astype(BF16)
    cw_a = jnp.zeros((L, 8, D_CONV), F32).at[:, :3].set(conv_a_w)
    cw_f = jnp.zeros((L, 16, d_ff), F32).at[:, :9].set(ffn_conv_w.reshape(L, 9, d_ff))

    za, zb = ctx, x
    for l in range(L):
        split = l == 0
        lat = mod[l, :B].reshape(B, 6, D)
        cm = jnp.broadcast_to(mod[l, B].reshape(1, 6, D), (B, 6, D))
        modsel = jnp.zeros((B, 2, 8, D), F32).at[:, 0, :6].set(cm).at[:, 1, :6].set(lat)

        cb, u, v, bonus, gate, at, rt, bt, kt, cl = _proj(
            za, zb, split, modsel, norm1_g[l].reshape(1, D), w_in[l].astype(BF16), wup_ext[l], aup_ext[l],
            w0_2[l], a0_2[l], rw_k_k[l].reshape(1, -1), rw_k_a[l].reshape(1, -1),
            rw_r_k[l].reshape(1, -1), rw_g_up[l].astype(BF16), ones_bd, tri)
        yf, yb = _scan(v, at, rt, bt, kt, cl, CTX)
        x1 = _mix(
            za, zb, split, modsel, cb, u, yf, yb, bonus, gate, cw_a[l], rw_ln_g[l].reshape(1, -1),
            rw_ln_b[l].reshape(1, -1), ones_bd, w_out[l].astype(BF16))
        za = zb = _ffn(x1, modsel, norm2_g[l].reshape(1, D), ffn_w_up[l].astype(BF16), cw_f[l],
                       ffn_conv_b[l].reshape(1, -1), ffn_w_down[l].astype(BF16), final_g.reshape(1, D),
                       final=(l == L - 1))
    return za
```

```python
import functools
import math

import jax
import jax.numpy as jnp
from jax import lax
from jax.experimental import pallas as pl
from jax.experimental.pallas import tpu as pltpu

F32 = jnp.float32
BF16 = jnp.bfloat16

HEAD = 64
D_CONV = 512
D_RWKV = 512
LORA_WA = 128
G_LORA = 128
GRID_W = 64
RMS_EPS = 1e-6
GN_EPS = 64e-5
DECAY_SCALE = math.exp(-0.5)

TM = 256
CHUNK = 64
CPT = TM // CHUNK
HPG = 2
GROUP = HPG * HEAD
VMEM_LIMIT = 56 * 1024 * 1024

NN = (((1,), (0,)), ((), ()))
NT = (((1,), (1,)), ((), ()))
TN = (((0,), (0,)), ((), ()))


def _silu(x):
    return x * jax.nn.sigmoid(x)


def _split(x):
    hi = x.astype(BF16)
    return hi, (x - hi.astype(F32)).astype(BF16)


def _dg(a, b, dims):
    return lax.dot_general(a, b, dims, preferred_element_type=F32)


def _mm3(a, b_hi, b_lo, dims):
    a_hi, a_lo = _split(a)
    return _dg(a_hi, b_hi, dims) + (_dg(a_hi, b_lo, dims) + _dg(a_lo, b_hi, dims))


def _rms(x):
    return x * lax.rsqrt(jnp.mean(x * x, axis=-1, keepdims=True) + RMS_EPS)


def _dot01(t, m01, left=False):
    hi, lo = _split(t)
    if left:
        return _dg(m01, hi, NN) + _dg(m01, lo, NN)
    return _dg(hi, m01, NN) + _dg(lo, m01, NN)


def _z_tile(za_ref, zb_ref, split):
    if not split:
        return za_ref[0]
    return jnp.where(pl.program_id(1) == 0, za_ref[0], zb_ref[0])


def _z_specs(split, D):
    if split:
        return [pl.BlockSpec((1, TM, D), lambda b, i: (b, 0, 0)),
                pl.BlockSpec((1, TM, D), lambda b, i: (b, jnp.maximum(i - 1, 0), 0))]
    return [pl.BlockSpec((1, TM, D), lambda b, i: (b, i, 0)),
            pl.BlockSpec((1, 8, D), lambda b, i: (0, 0, 0))]


def _ada_kernel(c_ref, w_ref, b_ref, o_ref):
    s = _silu(c_ref[...])
    w_hi, w_lo = _split(w_ref[0])
    o_ref[0] = _mm3(s, w_hi, w_lo, NN) + b_ref[0]


def _ada(c_rows, ada_w, ada_b):
    L, D, N = ada_w.shape
    R = c_rows.shape[0]
    NB = 1536
    return pl.pallas_call(
        _ada_kernel,
        grid=(L, N // NB),
        in_specs=[
            pl.BlockSpec((R, D), lambda l, n: (0, 0)),
            pl.BlockSpec((1, D, NB), lambda l, n: (l, 0, n)),
            pl.BlockSpec((1, 1, NB), lambda l, n: (l, 0, n)),
        ],
        out_specs=pl.BlockSpec((1, R, NB), lambda l, n: (l, 0, n)),
        out_shape=jax.ShapeDtypeStruct((L, R, N), F32),
        compiler_params=pltpu.CompilerParams(
            dimension_semantics=("arbitrary", "arbitrary"), vmem_limit_bytes=VMEM_LIMIT),
        name="ada",
    )(c_rows, ada_w, ada_b.reshape(L, 1, N))


def _proj_kernel(za_ref, zb_ref, m_ref, g_ref, win_ref, wup_ref, aup_ref, w0_ref, a0_ref, kk_ref, ka_ref,
                 rk_ref, ones_ref, tri_ref,
                 cb_ref, u_ref, v_ref, prod_ref, gl_ref, at_ref, rt_ref, bt_ref, kt_ref, cl_ref,
                 *, split):
    m = m_ref[0, 0]
    h = _rms(_z_tile(za_ref, zb_ref, split)) * g_ref[...]
    h = (h * (1.0 + m[1:2]) + m[0:1]).astype(BF16)

    def proj(lo, width):
        return jnp.dot(h, win_ref[:, lo:lo + width], preferred_element_type=F32)

    base = 3 * D_CONV
    lora = proj(base + 3 * D_RWKV, LORA_WA + G_LORA)
    k = proj(base + D_RWKV, D_RWKV)
    r = proj(base, D_RWKV)
    v = proj(base + 2 * D_RWKV, D_RWKV)
    wa = lora[:, :LORA_WA]
    gl = lora[:, LORA_WA:]
    ones_bd = ones_ref[...]

    kraw = k * kk_ref[...]
    kkn = kraw * lax.rsqrt(jnp.maximum(_dot01(kraw * kraw, ones_bd), 1e-24))
    lw2 = -DECAY_SCALE * jax.nn.sigmoid(
        w0_ref[...] + jnp.dot(jnp.tanh(wa).astype(BF16), wup_ref[...], preferred_element_type=F32))
    a2 = jax.nn.sigmoid(
        a0_ref[...] + jnp.dot(wa.astype(BF16), aup_ref[...], preferred_element_type=F32))

    cb_ref[0] = proj(0, D_CONV)
    v_ref[0] = v.astype(BF16)
    prod_ref[0] = r * k * rk_ref[...]
    gl_ref[0] = gl

    zero_rows = jnp.zeros((8 - CPT, D_RWKV), F32)
    conv_in = []
    for d in range(2):
        conv_in.append(proj((1 + d) * D_CONV, D_CONV))
        lw = lw2[:, d * D_RWKV:(d + 1) * D_RWKV]
        a = a2[:, d * D_RWKV:(d + 1) * D_RWKV]
        c = _dot01(lw, tri_ref[d], left=True)
        e_neg = jnp.exp(-c)
        at_ref[d, 0] = (kkn * jnp.exp(c - lw)).astype(BF16)
        rt_ref[d, 0] = (r * jnp.exp(c)).astype(BF16)
        bt_ref[d, 0] = (kkn * a * e_neg).astype(BF16)
        kt_ref[d, 0] = (k * (1.0 + (a - 1.0) * ka_ref[...]) * e_neg).astype(BF16)
        last = CHUNK - 1 if d == 0 else 0
        cl_ref[d, 0, 0] = jnp.concatenate(
            [c[j * CHUNK + last:j * CHUNK + last + 1] for j in range(CPT)] + [zero_rows], axis=0)
    u_ref[0] = conv_in[0] * conv_in[1]


def _proj(za, zb, split, modsel, g1, win, wup, aup, w0, a0, k_k, k_a, r_k, ones_bd, tri):
    B, D = za.shape[0], za.shape[2]
    T = za.shape[1] + zb.shape[1] if split else za.shape[1]
    NTL = T // TM
    P = win.shape[1]
    W = D_RWKV
    tok = pl.BlockSpec((1, TM, W), lambda b, i: (b, i, 0))
    tok2 = pl.BlockSpec((2, 1, TM, W), lambda b, i: (0, b, i, 0))
    const = lambda shape: pl.BlockSpec(shape, lambda b, i: (0,) * len(shape))
    f32_tok = jax.ShapeDtypeStruct((B, T, W), F32)
    bf_tok2 = jax.ShapeDtypeStruct((2, B, T, W), BF16)
    return pl.pallas_call(
        functools.partial(_proj_kernel, split=split),
        grid=(B, NTL),
        in_specs=_z_specs(split, D) + [
            pl.BlockSpec((1, 1, 8, D), lambda b, i: (b, jnp.minimum(i, 1), 0, 0)),
            const((1, D)), const((D, P)), const((LORA_WA, 2 * W)), const((LORA_WA, 2 * W)),
            const((1, 2 * W)), const((1, 2 * W)), const((1, W)), const((1, W)),
            const((1, W)), const((W, W)), const((2, TM, TM)),
        ],
        out_specs=[tok, tok, tok, tok, pl.BlockSpec((1, TM, G_LORA), lambda b, i: (b, i, 0)),
                   tok2, tok2, tok2, tok2,
                   pl.BlockSpec((2, 1, 1, 8, W), lambda b, i: (0, b, i, 0, 0))],
        out_shape=[f32_tok, f32_tok, jax.ShapeDtypeStruct((B, T, W), BF16), f32_tok,
                   jax.ShapeDtypeStruct((B, T, G_LORA), F32),
                   bf_tok2, bf_tok2, bf_tok2, bf_tok2,
                   jax.ShapeDtypeStruct((2, B, NTL, 8, W), F32)],
        compiler_params=pltpu.CompilerParams(
            dimension_semantics=("parallel", "parallel"), vmem_limit_bytes=VMEM_LIMIT),
        name="proj",
    )(za, zb, modsel, g1, win, wup, aup, w0, a0, k_k, k_a, r_k, ones_bd, tri)


def _mm(a, x, dims):
    return _dg(a.astype(BF16), x.astype(BF16), dims)


def _scan_chunks(chains, bdmask):
    C = CHUNK
    row = lax.broadcasted_iota(jnp.int32, (C, GROUP), 0)
    col = lax.broadcasted_iota(jnp.int32, (C, GROUP), 1) & (HEAD - 1)
    eye = (row == col).astype(F32)
    blk16 = (row >> 4) == (col >> 4)
    lane_head = lax.broadcasted_iota(jnp.int32, (HEAD, GROUP), 1) >> 6
    zero_bf = jnp.zeros((), BF16)

    def each(f, *lists):
        return [f(*a) for a in zip(*lists)]

    def bd(x):
        return jnp.where(bdmask, jnp.concatenate([x.astype(BF16)] * HPG, axis=0), zero_bf)

    def bd2(x, y, axis):
        return jnp.concatenate([bd(x), bd(y)], axis=axis)

    def rows(*xs):
        return jnp.concatenate([x.astype(BF16) for x in xs], axis=0)

    def diag_blocks(full):
        out = jnp.where(lane_head == 0, full[0:HEAD], 0.0)
        for hh in range(1, HPG):
            out = out + jnp.where(lane_head == hh, full[hh * HEAD:(hh + 1) * HEAD], 0.0)
        return out

    rev = [ch["rev"] for ch in chains]
    at, rt, bt, kt, v, p_c, s0 = ([ch[n] for ch in chains]
                                  for n in ("at", "rt", "bt", "kt", "v", "p_c", "s0"))
    strict = [(col > row) if rv else (col < row) for rv in rev]
    incl = [(col >= row) if rv else (col <= row) for rv in rev]
    bh = each(lambda x, p: x * p, bt, p_c)
    kh = each(lambda x, p: x * p, kt, p_c)

    abk = each(lambda a_, r_, x, y: _mm(rows(a_, r_), bd2(x, y, 0), NT), at, rt, bt, kt)
    lmat = each(lambda m_, x: jnp.where(m_, x[:C, :GROUP], 0.0), strict, abk)
    aak = each(lambda m_, x: jnp.where(m_, x[:C, GROUP:], 0.0), strict, abk)
    arb = each(lambda m_, x: jnp.where(m_, x[C:, :GROUP], 0.0), incl, abk)
    ark = each(lambda m_, x: jnp.where(m_, x[C:, GROUP:], 0.0), incl, abk)

    S = 16
    lane16 = (lax.broadcasted_iota(jnp.int32, (S, GROUP), 1) & (HEAD - 1)) >> 4
    eye_p = (lax.broadcasted_iota(jnp.int32, (S, GROUP), 0)
             == (lax.broadcasted_iota(jnp.int32, (S, GROUP), 1) & (S - 1))).astype(F32)
    mask16 = (lax.broadcasted_iota(jnp.int32, (GROUP, GROUP), 0) >> 4) == \
             (lax.broadcasted_iota(jnp.int32, (GROUP, GROUP), 1) >> 4)

    def pack16(full):
        out = jnp.where(lane16 == 0, full[0:S], 0.0)
        for j in range(1, HEAD // S):
            out = out + jnp.where(lane16 == j, full[j * S:(j + 1) * S], 0.0)
        return out

    def unpack16(p):
        return jnp.concatenate([jnp.where(lane16 == j, p, 0.0) for j in range(HEAD // S)], axis=0)

    def bd16(p):
        return jnp.where(mask16, jnp.concatenate([p.astype(BF16)] * (GROUP // S), axis=0), zero_bf)

    ld = each(pack16, lmat)
    lo = each(lambda x: jnp.where(blk16, 0.0, x), lmat)
    p1 = each(lambda x: eye_p - x, ld)
    l2 = each(lambda x: _mm(x, bd16(x), NN), ld)
    t = each(lambda p, x: _mm(rows(p, x), bd16(x), NN), p1, l2)
    p2 = each(lambda p, x: p + x[:S], p1, t)
    l4 = each(lambda x: x[S:], t)
    t = each(lambda p, x: _mm(rows(p, x), bd16(x), NN), p2, l4)
    p3 = each(lambda p, x: p + x[:S], p2, t)
    l8 = each(lambda x: x[S:], t)
    dinv = each(lambda p, x: unpack16(p + _mm(p, bd16(x), NN)), p3, l8)
    mm = each(lambda d_, x: _mm(d_, bd(x), NN), dinv, lo)
    mm2 = each(lambda x: _mm(x, bd(x), NN), mm)
    g = each(lambda x, x2: eye - x + x2 - _mm(x, bd(x2), NN), mm, mm2)
    tinv = each(lambda g_, d_: _mm(g_, bd(d_), NN), g, dinv)

    t = each(lambda x, y, v_: _mm(rows(x, y), bd(v_), NN), aak, ark, v)
    akv = each(lambda x: x[:C], t)
    arkv = each(lambda x: x[C:], t)
    t = each(lambda t_, x, y: _mm(t_, bd2(x, y, 1), NN), tinv, at, akv)
    wm = each(lambda x: x[:, :GROUP], t)
    um = each(lambda x: x[:, GROUP:], t)
    t = each(lambda a_, x, y: _mm(a_, bd2(x, y, 1), NN), arb, wm, um)
    qh = each(lambda r_, x: r_.astype(F32) - x[:, :GROUP], rt, t)
    yl = each(lambda y_, x: y_ - x[:, GROUP:], arkv, t)
    t = each(lambda q_, w_, s_: _mm(rows(q_, w_), bd(s_), NT), qh, wm, s0)
    y = each(lambda x, y_: x[:C] + y_, t, yl)
    ds = each(lambda v_, u_, x, k_, b_: diag_blocks(_mm(rows(v_, u_, x[C:]), rows(k_, -b_, -b_), TN)),
              v, um, t, kh, bh)
    s_new = each(lambda s_, p_, d_: s_ * p_ + d_, s0, p_c, ds)
    return list(zip(y, s_new))


def _scan_kernel(vf_ref, vb_ref, atf_ref, rtf_ref, btf_ref, ktf_ref, clf_ref,
                 atb_ref, rtb_ref, btb_ref, ktb_ref, clb_ref, yf_ref, yb_ref, s_ref, *, nctx, n_chunks):
    s = pl.program_id(1)

    @pl.when(s == 0)
    def _():
        s_ref[...] = jnp.zeros_like(s_ref)

    back = jnp.where(s < nctx, nctx - 1 - s, nctx + n_chunks - 1 - s)
    bdmask = (lax.broadcasted_iota(jnp.int32, (GROUP, GROUP), 0) >> 6) == \
             (lax.broadcasted_iota(jnp.int32, (GROUP, GROUP), 1) >> 6)
    dirs = ((False, s, vf_ref, atf_ref, rtf_ref, btf_ref, ktf_ref, clf_ref, yf_ref),
            (True, back, vb_ref, atb_ref, rtb_ref, btb_ref, ktb_ref, clb_ref, yb_ref))
    chains, sinks = [], []
    for bi in range(s_ref.shape[0]):
        for d, (rev, chunk, v_ref, at_ref, rt_ref, bt_ref, kt_ref, cl_ref, y_ref) in enumerate(dirs):
            p_c = jnp.exp(cl_ref[0, bi, 0, pl.ds(chunk % CPT, 1), :])
            for q in range(D_RWKV // GROUP):
                sl = slice(q * GROUP, (q + 1) * GROUP)
                chains.append(dict(rev=rev, at=at_ref[0, bi, :, sl], rt=rt_ref[0, bi, :, sl],
                                   bt=bt_ref[0, bi, :, sl], kt=kt_ref[0, bi, :, sl],
                                   v=v_ref[bi, :, sl], p_c=p_c[:, sl], s0=s_ref[bi, d, :, sl]))
                sinks.append((y_ref, bi, d, sl))
    for (y, s_new), (y_ref, bi, d, sl) in zip(_scan_chunks(chains, bdmask), sinks):
        y_ref[bi, :, sl] = y
        s_ref[bi, d, :, sl] = s_new


def _scan(v, at, rt, bt, kt, cl, ctx_len):
    B, T, W = v.shape
    C = CHUNK
    NC = T // C
    nctx = ctx_len // C

    def back(s):
        return jnp.where(s < nctx, nctx - 1 - s, nctx + NC - 1 - s)

    BB = 4 if B % 4 == 0 else 2 if B % 2 == 0 else 1
    fwd = pl.BlockSpec((BB, C, W), lambda bb, s: (bb, s, 0))
    bwd = pl.BlockSpec((BB, C, W), lambda bb, s: (bb, back(s), 0))
    fwd_d = pl.BlockSpec((1, BB, C, W), lambda bb, s: (0, bb, s, 0))
    bwd_d = pl.BlockSpec((1, BB, C, W), lambda bb, s: (1, bb, back(s), 0))
    fwd_c = pl.BlockSpec((1, BB, 1, 8, W), lambda bb, s: (0, bb, s // CPT, 0, 0))
    bwd_c = pl.BlockSpec((1, BB, 1, 8, W), lambda bb, s: (1, bb, back(s) // CPT, 0, 0))
    out = jax.ShapeDtypeStruct((B, T, W), F32)
    return pl.pallas_call(
        functools.partial(_scan_kernel, nctx=nctx, n_chunks=NC),
        grid=(B // BB, NC),
        in_specs=[fwd, bwd, fwd_d, fwd_d, fwd_d, fwd_d, fwd_c, bwd_d, bwd_d, bwd_d, bwd_d, bwd_c],
        out_specs=[fwd, bwd],
        out_shape=[out, out],
        scratch_shapes=[pltpu.VMEM((BB, 2, HEAD, W), F32)],
        compiler_params=pltpu.CompilerParams(
            dimension_semantics=("parallel", "arbitrary"), vmem_limit_bytes=VMEM_LIMIT),
        name="scan",
    )(v, v, at, rt, bt, kt, cl, at, rt, bt, kt, cl)


def _mix_kernel(za_ref, zb_ref, m_ref, cb_ref, u_ref, up_ref, un_ref, yf_ref, yb_ref, prod_ref, v_ref,
                gl_ref, cw_ref, lng_ref, lnb_ref, ones_ref, gup_ref, wout_ref, x1_ref, *, n_tiles, split):
    i = pl.program_id(1)
    m = m_ref[0, 0]
    prev_ok = (i >= 2).astype(F32)
    next_ok = jnp.logical_and(i >= 1, i <= n_tiles - 2).astype(F32)
    u = u_ref[0]
    rowi = lax.broadcasted_iota(jnp.int32, (TM, 1), 0)
    um1 = jnp.where(rowi == 0, up_ref[0][7:8] * prev_ok, pltpu.roll(u, 1, 0))
    up1 = jnp.where(rowi == TM - 1, un_ref[0][0:1] * next_ok, pltpu.roll(u, TM - 1, 0))
    cw = cw_ref[...]
    yconv = cb_ref[0] * (cw[0:1] * um1 + cw[1:2] * u + cw[2:3] * up1)

    ones_bd = ones_ref[...]
    y = yf_ref[0] + yb_ref[0]
    mu = _dot01(y, ones_bd) * (1.0 / HEAD)
    dlt = y - mu
    var = _dot01(dlt * dlt, ones_bd) * (1.0 / HEAD)
    yn = dlt * lax.rsqrt(var + GN_EPS) * lng_ref[...] + lnb_ref[...]
    bonus = _dot01(prod_ref[0], ones_bd) * v_ref[0].astype(F32)
    gate = jnp.dot(jax.nn.sigmoid(gl_ref[0]).astype(BF16), gup_ref[...], preferred_element_type=F32)
    yrw = (yn + bonus) * gate

    out = (jnp.dot(yconv.astype(BF16), wout_ref[0:D_CONV, :], preferred_element_type=F32)
           + jnp.dot(yrw.astype(BF16), wout_ref[D_CONV:D_CONV + D_RWKV, :], preferred_element_type=F32))
    x1_ref[0] = _z_tile(za_ref, zb_ref, split) + m[2:3] * out


def _mix(za, zb, split, modsel, cb, u, yf, yb, prod, v, gl, cw, lng, lnb, ones_bd, gup, wout):
    B, T, D = cb.shape[0], cb.shape[1], za.shape[2]
    NTL = T // TM
    tok = pl.BlockSpec((1, TM, D_RWKV), lambda b, i: (b, i, 0))
    const = lambda shape: pl.BlockSpec(shape, lambda b, i: (0,) * len(shape))
    R8 = TM // 8
    return pl.pallas_call(
        functools.partial(_mix_kernel, n_tiles=NTL, split=split),
        grid=(B, NTL),
        in_specs=_z_specs(split, D) + [
            pl.BlockSpec((1, 1, 8, D), lambda b, i: (b, jnp.minimum(i, 1), 0, 0)),
            tok, tok,
            pl.BlockSpec((1, 8, D_CONV), lambda b, i: (b, jnp.maximum(i * R8 - 1, 0), 0)),
            pl.BlockSpec((1, 8, D_CONV), lambda b, i: (b, jnp.minimum((i + 1) * R8, T // 8 - 1), 0)),
            tok, tok, tok, tok, pl.BlockSpec((1, TM, G_LORA), lambda b, i: (b, i, 0)),
            const((8, D_CONV)), const((1, D_RWKV)), const((1, D_RWKV)), const((D_RWKV, D_RWKV)),
            const((G_LORA, D_RWKV)), const((D_CONV + D_RWKV, D)),
        ],
        out_specs=pl.BlockSpec((1, TM, D), lambda b, i: (b, i, 0)),
        out_shape=jax.ShapeDtypeStruct((B, T, D), F32),
        compiler_params=pltpu.CompilerParams(
            dimension_semantics=("parallel", "parallel"), vmem_limit_bytes=VMEM_LIMIT),
        name="mix",
    )(za, zb, modsel, cb, u, u, u, yf, yb, prod, v, gl, cw, lng, lnb, ones_bd, gup, wout)


def _ffn_kernel(x1_ref, xb_ref, m_ref, g2_ref, wup_ref, cw_ref, cbias_ref, wdn_ref, fg_ref,
                o_ref, gtop_ref, *, n_tiles, d_ff, final):
    i = pl.program_id(1)
    m = m_ref[0, 0]
    lat = (i >= 1).astype(F32)
    top_ok = i >= 2
    bot_ok = jnp.logical_and(i >= 1, i <= n_tiles - 2).astype(F32)
    rowi = lax.broadcasted_iota(jnp.int32, (TM, 1), 0)
    colp = jnp.where(i == 0, rowi, rowi & (GRID_W - 1))
    width = jnp.where(i == 0, TM, GRID_W)
    lmask = (colp > 0).astype(F32)
    rmask = (colp < width - 1).astype(F32)

    def norm2(x, keep):
        h = _rms(x) * g2_ref[...]
        return ((h * (1.0 + m[4:5]) + m[3:4]) * keep).astype(BF16)

    @pl.when(i == 0)
    def _():
        gtop_ref[...] = jnp.zeros_like(gtop_ref)

    x1 = x1_ref[0]
    hx = jnp.concatenate([norm2(x1, 1.0), norm2(xb_ref[0], bot_ok)], axis=0)
    hc = hx[0:TM]

    NB = 256

    def up(n):
        return (jnp.dot(hx, wup_ref[:, n:n + NB], preferred_element_type=F32),
                jnp.dot(hc, wup_ref[:, d_ff + n:d_ff + n + NB], preferred_element_type=F32))

    acc = jnp.zeros((TM, o_ref.shape[-1]), F32)
    nxt = up(0)
    prev = None
    for n in range(0, d_ff, NB):
        gfull, val = nxt
        if n + NB < d_ff:
            nxt = up(n + NB)
        if prev is not None:
            acc = acc + jnp.dot(prev, wdn_ref[n - NB:n, :], preferred_element_type=F32)
        g, gd = gfull[0:TM], gfull[GRID_W:GRID_W + TM]
        top = jnp.where(top_ok, gtop_ref[:, n:n + NB], 0.0)
        gu = jnp.concatenate([top, g[0:TM - GRID_W]], axis=0)
        gtop_ref[:, n:n + NB] = g[TM - GRID_W:TM]
        w = cw_ref[:, n:n + NB]
        wv = w * lat
        h0 = wv[0:1] * gu + w[3:4] * g + wv[6:7] * gd
        h1 = wv[1:2] * gu + w[4:5] * g + wv[7:8] * gd
        h2 = wv[2:3] * gu + w[5:6] * g + wv[8:9] * gd
        conv = (lmask * pltpu.roll(h0, 1, 0) + h1 + rmask * pltpu.roll(h2, TM - 1, 0)
                + cbias_ref[:, n:n + NB])
        prev = (_silu(conv) * val).astype(BF16)
    acc = acc + jnp.dot(prev, wdn_ref[d_ff - NB:d_ff, :], preferred_element_type=F32)
    x2 = x1 + m[5:6] * acc
    if final:
        x2 = _rms(x2) * fg_ref[...]
    o_ref[0] = x2


def _ffn(x1, modsel, g2, wup, cw, cbias, wdn, fg, final):
    B, T, D = x1.shape
    NTL = T // TM
    d_ff = wdn.shape[0]
    RW = TM // GRID_W
    const = lambda shape: pl.BlockSpec(shape, lambda b, i: (0,) * len(shape))
    if final:
        out_spec = pl.BlockSpec((1, TM, D), lambda b, i: (b, jnp.maximum(i - 1, 0), 0))
        out_shape = jax.ShapeDtypeStruct((B, T - TM, D), F32)
    else:
        out_spec = pl.BlockSpec((1, TM, D), lambda b, i: (b, i, 0))
        out_shape = jax.ShapeDtypeStruct((B, T, D), F32)
    return pl.pallas_call(
        functools.partial(_ffn_kernel, n_tiles=NTL, d_ff=d_ff, final=final),
        grid=(B, NTL),
        in_specs=[
            pl.BlockSpec((1, TM, D), lambda b, i: (b, i, 0)),
            pl.BlockSpec((1, GRID_W, D),
                         lambda b, i: (b, jnp.minimum((i + 1) * RW, T // GRID_W - 1), 0)),
            pl.BlockSpec((1, 1, 8, D), lambda b, i: (b, jnp.minimum(i, 1), 0, 0)),
            const((1, D)), const((D, 2 * d_ff)),
            const((16, d_ff)), const((1, d_ff)), const((d_ff, D)), const((1, D)),
        ],
        out_specs=out_spec,
        out_shape=out_shape,
        scratch_shapes=[pltpu.VMEM((GRID_W, d_ff), F32)],
        compiler_params=pltpu.CompilerParams(
            dimension_semantics=("parallel", "arbitrary"), vmem_limit_bytes=VMEM_LIMIT),
        name="ffn",
    )(x1, x1, modsel, g2, wup, cw, cbias, wdn, fg)


def kernel(x, c, ctx, c_ctx, ada_w, ada_b, norm1_g, norm2_g, w_in, conv_a_w, rw_w0, rw_w_up, rw_a0,
           rw_a_up, rw_k_k, rw_k_a, rw_r_k, rw_g_up, rw_ln_g, rw_ln_b, w_out, ffn_w_up, ffn_conv_w,
           ffn_conv_b, ffn_w_down, final_g):
    B, SEQ, D = x.shape
    CTX = ctx.shape[1]
    L = w_in.shape[0]
    d_ff = ffn_w_down.shape[1]
    assert CTX == TM and SEQ % TM == 0 and TM % GRID_W == 0 and CHUNK == HEAD and CPT <= 8
    assert w_in.shape[2] == 3 * D_CONV + 3 * D_RWKV + LORA_WA + G_LORA and d_ff % 256 == 0

    rows = -(-(B + 1) // 8) * 8
    c_rows = jnp.zeros((rows, D), F32).at[:B].set(c).at[B].set(c_ctx)
    mod = _ada(c_rows, ada_w, ada_b)

    hw = LORA_WA // 2
    zpad = jnp.zeros((L, hw, 2 * D_RWKV), F32)
    both = lambda t: jnp.concatenate([t[:, 0], t[:, 1]], axis=-1)
    wup_ext = jnp.concatenate([both(rw_w_up), zpad], axis=1).astype(BF16)
    aup_ext = jnp.concatenate([zpad, both(rw_a_up)], axis=1).astype(BF16)
    w0_2 = both(rw_w0[:, :, None, :])
    a0_2 = both(rw_a0[:, :, None, :])
    head_of = jnp.arange(D_RWKV) // HEAD
    ones_bd = (head_of[:, None] == head_of[None, :]).astype(BF16)
    tt = jnp.arange(TM)
    same = (tt[:, None] // CHUNK) == (tt[None, :] // CHUNK)
    tri = jnp.stack([same & (tt[None, :] <= tt[:, None]),
                     same & (tt[None, :] >= tt[:, None])]).astype(BF16)
    cw_a = jnp.zeros((L, 8, D_CONV), F32).at[:, :3].set(conv_a_w)
    cw_f = jnp.zeros((L, 16, d_ff), F32).at[:, :9].set(ffn_conv_w.reshape(L, 9, d_ff))

    za, zb = ctx, x
    for l in range(L):
        split = l == 0
        lat = mod[l, :B].reshape(B, 6, D)
        cm = jnp.broadcast_to(mod[l, B].reshape(1, 6, D), (B, 6, D))
        modsel = jnp.zeros((B, 2, 8, D), F32).at[:, 0, :6].set(cm).at[:, 1, :6].set(lat)

        cb, u, v, prod, gl, at, rt, bt, kt, cl = _proj(
            za, zb, split, modsel, norm1_g[l].reshape(1, D), w_in[l].astype(BF16), wup_ext[l], aup_ext[l],
            w0_2[l], a0_2[l], rw_k_k[l].reshape(1, -1), rw_k_a[l].reshape(1, -1),
            rw_r_k[l].reshape(1, -1), ones_bd, tri)
        yf, yb = _scan(v, at, rt, bt, kt, cl, CTX)
        x1 = _mix(
            za, zb, split, modsel, cb, u, yf, yb, prod, v, gl, cw_a[l], rw_ln_g[l].reshape(1, -1),
            rw_ln_b[l].reshape(1, -1), ones_bd, rw_g_up[l].astype(BF16), w_out[l].astype(BF16))
        za = zb = _ffn(x1, modsel, norm2_g[l].reshape(1, D), ffn_w_up[l].astype(BF16), cw_f[l],
                       ffn_conv_b[l].reshape(1, -1), ffn_w_down[l].astype(BF16), final_g.reshape(1, D),
                       final=(l == L - 1))
    return za
```

```python
import functools
import math

import jax
import jax.numpy as jnp
from jax import lax
from jax.experimental import pallas as pl
from jax.experimental.pallas import tpu as pltpu

F32 = jnp.float32
BF16 = jnp.bfloat16

HEAD = 64
D_CONV = 512
D_RWKV = 512
LORA_WA = 128
G_LORA = 128
GRID_W = 64
RMS_EPS = 1e-6
GN_EPS = 64e-5
DECAY_SCALE = math.exp(-0.5)

TM = 256
CHUNK = 64
CPT = TM // CHUNK
HPG = 2
GROUP = HPG * HEAD
VMEM_LIMIT = 56 * 1024 * 1024

NN = (((1,), (0,)), ((), ()))
NT = (((1,), (1,)), ((), ()))
TN = (((0,), (0,)), ((), ()))


def _silu(x):
    return x * jax.nn.sigmoid(x)


def _split(x):
    hi = x.astype(BF16)
    return hi, (x - hi.astype(F32)).astype(BF16)


def _dg(a, b, dims):
    return lax.dot_general(a, b, dims, preferred_element_type=F32)


def _mm3(a, b_hi, b_lo, dims):
    a_hi, a_lo = _split(a)
    return _dg(a_hi, b_hi, dims) + (_dg(a_hi, b_lo, dims) + _dg(a_lo, b_hi, dims))


def _rms(x):
    return x * lax.rsqrt(jnp.mean(x * x, axis=-1, keepdims=True) + RMS_EPS)


def _dot01(t, m01, left=False):
    hi, lo = _split(t)
    if left:
        return _dg(m01, hi, NN) + _dg(m01, lo, NN)
    return _dg(hi, m01, NN) + _dg(lo, m01, NN)


def _z_tile(za_ref, zb_ref, split):
    if not split:
        return za_ref[0]
    return jnp.where(pl.program_id(1) == 0, za_ref[0], zb_ref[0])


def _z_specs(split, D):
    if split:
        return [pl.BlockSpec((1, TM, D), lambda b, i: (b, 0, 0)),
                pl.BlockSpec((1, TM, D), lambda b, i: (b, jnp.maximum(i - 1, 0), 0))]
    return [pl.BlockSpec((1, TM, D), lambda b, i: (b, i, 0)),
            pl.BlockSpec((1, 8, D), lambda b, i: (0, 0, 0))]


def _ada_kernel(c_ref, w_ref, b_ref, o_ref):
    s = _silu(c_ref[...])
    w_hi, w_lo = _split(w_ref[0])
    o_ref[0] = _mm3(s, w_hi, w_lo, NN) + b_ref[0]


def _ada(c_rows, ada_w, ada_b):
    L, D, N = ada_w.shape
    R = c_rows.shape[0]
    NB = 1536
    return pl.pallas_call(
        _ada_kernel,
        grid=(L, N // NB),
        in_specs=[
            pl.BlockSpec((R, D), lambda l, n: (0, 0)),
            pl.BlockSpec((1, D, NB), lambda l, n: (l, 0, n)),
            pl.BlockSpec((1, 1, NB), lambda l, n: (l, 0, n)),
        ],
        out_specs=pl.BlockSpec((1, R, NB), lambda l, n: (l, 0, n)),
        out_shape=jax.ShapeDtypeStruct((L, R, N), F32),
        compiler_params=pltpu.CompilerParams(
            dimension_semantics=("arbitrary", "arbitrary"), vmem_limit_bytes=VMEM_LIMIT),
        name="ada",
    )(c_rows, ada_w, ada_b.reshape(L, 1, N))


def _proj_kernel(za_ref, zb_ref, m_ref, g_ref, win_ref, wup_ref, aup_ref, w0_ref, a0_ref, kk_ref, ka_ref,
                 rk_ref, ones_ref, tri_ref,
                 cb_ref, u_ref, v_ref, prod_ref, gl_ref, ops_ref, cl_ref,
                 *, split):
    m = m_ref[0, 0]
    h = _rms(_z_tile(za_ref, zb_ref, split)) * g_ref[...]
    h = (h * (1.0 + m[1:2]) + m[0:1]).astype(BF16)

    def proj(lo, width):
        return jnp.dot(h, win_ref[:, lo:lo + width], preferred_element_type=F32)

    base = 3 * D_CONV
    lora = proj(base + 3 * D_RWKV, LORA_WA + G_LORA)
    k = proj(base + D_RWKV, D_RWKV)
    r = proj(base, D_RWKV)
    v = proj(base + 2 * D_RWKV, D_RWKV)
    wa = lora[:, :LORA_WA]
    gl = lora[:, LORA_WA:]
    ones_bd = ones_ref[...]

    kraw = k * kk_ref[...]
    kkn = kraw * lax.rsqrt(jnp.maximum(_dot01(kraw * kraw, ones_bd), 1e-24))
    lw2 = -DECAY_SCALE * jax.nn.sigmoid(
        w0_ref[...] + jnp.dot(jnp.tanh(wa).astype(BF16), wup_ref[...], preferred_element_type=F32))
    a2 = jax.nn.sigmoid(
        a0_ref[...] + jnp.dot(wa.astype(BF16), aup_ref[...], preferred_element_type=F32))

    cb_ref[0] = proj(0, D_CONV)
    v_ref[0] = v.astype(BF16)
    prod_ref[0] = r * k * rk_ref[...]
    gl_ref[0] = gl

    zero_rows = jnp.zeros((8 - CPT, D_RWKV), F32)
    conv_in = []
    for d in range(2):
        conv_in.append(proj((1 + d) * D_CONV, D_CONV))
        lw = lw2[:, d * D_RWKV:(d + 1) * D_RWKV]
        a = a2[:, d * D_RWKV:(d + 1) * D_RWKV]
        c = _dot01(lw, tri_ref[d], left=True)
        e_neg = jnp.exp(-c)
        ops = (kkn * jnp.exp(c - lw), r * jnp.exp(c), kkn * a * e_neg,
               k * (1.0 + (a - 1.0) * ka_ref[...]) * e_neg)
        for j, x in enumerate(ops):
            ops_ref[d, 0, :, j * D_RWKV:(j + 1) * D_RWKV] = x.astype(BF16)
        last = CHUNK - 1 if d == 0 else 0
        cl_ref[d, 0, 0] = jnp.concatenate(
            [c[j * CHUNK + last:j * CHUNK + last + 1] for j in range(CPT)] + [zero_rows], axis=0)
    u_ref[0] = conv_in[0] * conv_in[1]


def _proj(za, zb, split, modsel, g1, win, wup, aup, w0, a0, k_k, k_a, r_k, ones_bd, tri):
    B, D = za.shape[0], za.shape[2]
    T = za.shape[1] + zb.shape[1] if split else za.shape[1]
    NTL = T // TM
    P = win.shape[1]
    W = D_RWKV
    tok = pl.BlockSpec((1, TM, W), lambda b, i: (b, i, 0))
    tok2 = pl.BlockSpec((2, 1, TM, 4 * W), lambda b, i: (0, b, i, 0))
    const = lambda shape: pl.BlockSpec(shape, lambda b, i: (0,) * len(shape))
    f32_tok = jax.ShapeDtypeStruct((B, T, W), F32)
    bf_tok2 = jax.ShapeDtypeStruct((2, B, T, 4 * W), BF16)
    return pl.pallas_call(
        functools.partial(_proj_kernel, split=split),
        grid=(B, NTL),
        in_specs=_z_specs(split, D) + [
            pl.BlockSpec((1, 1, 8, D), lambda b, i: (b, jnp.minimum(i, 1), 0, 0)),
            const((1, D)), const((D, P)), const((LORA_WA, 2 * W)), const((LORA_WA, 2 * W)),
            const((1, 2 * W)), const((1, 2 * W)), const((1, W)), const((1, W)),
            const((1, W)), const((W, W)), const((2, TM, TM)),
        ],
        out_specs=[tok, tok, tok, tok, pl.BlockSpec((1, TM, G_LORA), lambda b, i: (b, i, 0)),
                   tok2,
                   pl.BlockSpec((2, 1, 1, 8, W), lambda b, i: (0, b, i, 0, 0))],
        out_shape=[f32_tok, f32_tok, jax.ShapeDtypeStruct((B, T, W), BF16), f32_tok,
                   jax.ShapeDtypeStruct((B, T, G_LORA), F32),
                   bf_tok2,
                   jax.ShapeDtypeStruct((2, B, NTL, 8, W), F32)],
        compiler_params=pltpu.CompilerParams(
            dimension_semantics=("parallel", "parallel"), vmem_limit_bytes=VMEM_LIMIT),
        name="proj",
    )(za, zb, modsel, g1, win, wup, aup, w0, a0, k_k, k_a, r_k, ones_bd, tri)


def _mm(a, x, dims):
    return _dg(a.astype(BF16), x.astype(BF16), dims)


def _scan_chunks(chains, bdmask):
    C = CHUNK
    row = lax.broadcasted_iota(jnp.int32, (C, GROUP), 0)
    col = lax.broadcasted_iota(jnp.int32, (C, GROUP), 1) & (HEAD - 1)
    eye = (row == col).astype(F32)
    blk16 = (row >> 4) == (col >> 4)
    lane_head = lax.broadcasted_iota(jnp.int32, (HEAD, GROUP), 1) >> 6
    zero_bf = jnp.zeros((), BF16)

    def each(f, *lists):
        return [f(*a) for a in zip(*lists)]

    def bd(x):
        return jnp.where(bdmask, jnp.concatenate([x.astype(BF16)] * HPG, axis=0), zero_bf)

    def bd2(x, y, axis):
        return jnp.concatenate([bd(x), bd(y)], axis=axis)

    def rows(*xs):
        return jnp.concatenate([x.astype(BF16) for x in xs], axis=0)

    def diag_blocks(full):
        out = jnp.where(lane_head == 0, full[0:HEAD], 0.0)
        for hh in range(1, HPG):
            out = out + jnp.where(lane_head == hh, full[hh * HEAD:(hh + 1) * HEAD], 0.0)
        return out

    rev = [ch["rev"] for ch in chains]
    at, rt, bt, kt, v, p_c, s0 = ([ch[n] for ch in chains]
                                  for n in ("at", "rt", "bt", "kt", "v", "p_c", "s0"))
    strict = [(col > row) if rv else (col < row) for rv in rev]
    incl = [(col >= row) if rv else (col <= row) for rv in rev]
    bh = each(lambda x, p: x * p, bt, p_c)
    kh = each(lambda x, p: x * p, kt, p_c)

    abk = each(lambda a_, r_, x, y: _mm(rows(a_, r_), bd2(x, y, 0), NT), at, rt, bt, kt)
    lmat = each(lambda m_, x: jnp.where(m_, x[:C, :GROUP], 0.0), strict, abk)
    aak = each(lambda m_, x: jnp.where(m_, x[:C, GROUP:], 0.0), strict, abk)
    arb = each(lambda m_, x: jnp.where(m_, x[C:, :GROUP], 0.0), incl, abk)
    ark = each(lambda m_, x: jnp.where(m_, x[C:, GROUP:], 0.0), incl, abk)

    S = 16
    lane16 = (lax.broadcasted_iota(jnp.int32, (S, GROUP), 1) & (HEAD - 1)) >> 4
    eye_p = (lax.broadcasted_iota(jnp.int32, (S, GROUP), 0)
             == (lax.broadcasted_iota(jnp.int32, (S, GROUP), 1) & (S - 1))).astype(F32)
    mask16 = (lax.broadcasted_iota(jnp.int32, (GROUP, GROUP), 0) >> 4) == \
             (lax.broadcasted_iota(jnp.int32, (GROUP, GROUP), 1) >> 4)

    def pack16(full):
        out = jnp.where(lane16 == 0, full[0:S], 0.0)
        for j in range(1, HEAD // S):
            out = out + jnp.where(lane16 == j, full[j * S:(j + 1) * S], 0.0)
        return out

    def unpack16(p):
        return jnp.concatenate([jnp.where(lane16 == j, p, 0.0) for j in range(HEAD // S)], axis=0)

    def bd16(p):
        return jnp.where(mask16, jnp.concatenate([p.astype(BF16)] * (GROUP // S), axis=0), zero_bf)

    ld = each(pack16, lmat)
    lo = each(lambda x: jnp.where(blk16, 0.0, x), lmat)
    p1 = each(lambda x: eye_p - x, ld)
    l2 = each(lambda x: _mm(x, bd16(x), NN), ld)
    t = each(lambda p, x: _mm(rows(p, x), bd16(x), NN), p1, l2)
    p2 = each(lambda p, x: p + x[:S], p1, t)
    l4 = each(lambda x: x[S:], t)
    t = each(lambda p, x: _mm(rows(p, x), bd16(x), NN), p2, l4)
    p3 = each(lambda p, x: p + x[:S], p2, t)
    l8 = each(lambda x: x[S:], t)
    dinv = each(lambda p, x: unpack16(p + _mm(p, bd16(x), NN)), p3, l8)
    mm = each(lambda d_, x: _mm(d_, bd(x), NN), dinv, lo)
    mm2 = each(lambda x: _mm(x, bd(x), NN), mm)
    g = each(lambda x, x2: eye - x + x2 - _mm(x, bd(x2), NN), mm, mm2)
    tinv = each(lambda g_, d_: _mm(g_, bd(d_), NN), g, dinv)

    t = each(lambda x, y, v_: _mm(rows(x, y), bd(v_), NN), aak, ark, v)
    akv = each(lambda x: x[:C], t)
    arkv = each(lambda x: x[C:], t)
    t = each(lambda t_, x, y: _mm(t_, bd2(x, y, 1), NN), tinv, at, akv)
    wm = each(lambda x: x[:, :GROUP], t)
    um = each(lambda x: x[:, GROUP:], t)
    t = each(lambda a_, x, y: _mm(a_, bd2(x, y, 1), NN), arb, wm, um)
    qh = each(lambda r_, x: r_.astype(F32) - x[:, :GROUP], rt, t)
    yl = each(lambda y_, x: y_ - x[:, GROUP:], arkv, t)
    t = each(lambda q_, w_, s_: _mm(rows(q_, w_), bd(s_), NT), qh, wm, s0)
    y = each(lambda x, y_: x[:C] + y_, t, yl)
    ds = each(lambda v_, u_, x, k_, b_: diag_blocks(_mm(rows(v_, u_, x[C:]), rows(k_, -b_, -b_), TN)),
              v, um, t, kh, bh)
    s_new = each(lambda s_, p_, d_: s_ * p_ + d_, s0, p_c, ds)
    return list(zip(y, s_new))


def _scan_kernel(vf_ref, vb_ref, opf_ref, clf_ref, opb_ref, clb_ref, yf_ref, yb_ref, s_ref,
                 *, nctx, n_chunks):
    s = pl.program_id(1)

    @pl.when(s == 0)
    def _():
        s_ref[...] = jnp.zeros_like(s_ref)

    back = jnp.where(s < nctx, nctx - 1 - s, nctx + n_chunks - 1 - s)
    bdmask = (lax.broadcasted_iota(jnp.int32, (GROUP, GROUP), 0) >> 6) == \
             (lax.broadcasted_iota(jnp.int32, (GROUP, GROUP), 1) >> 6)
    dirs = ((False, s, vf_ref, opf_ref, clf_ref, yf_ref), (True, back, vb_ref, opb_ref, clb_ref, yb_ref))
    chains, sinks = [], []
    for bi in range(s_ref.shape[0]):
        for d, (rev, chunk, v_ref, op_ref, cl_ref, y_ref) in enumerate(dirs):
            p_c = jnp.exp(cl_ref[0, bi, 0, pl.ds(chunk % CPT, 1), :])
            for q in range(D_RWKV // GROUP):
                sl = slice(q * GROUP, (q + 1) * GROUP)
                at, rt, bt, kt = (op_ref[0, bi, :, j * D_RWKV + q * GROUP:j * D_RWKV + (q + 1) * GROUP]
                                  for j in range(4))
                chains.append(dict(rev=rev, at=at, rt=rt, bt=bt, kt=kt,
                                   v=v_ref[bi, :, sl], p_c=p_c[:, sl], s0=s_ref[bi, d, :, sl]))
                sinks.append((y_ref, bi, d, sl))
    for (y, s_new), (y_ref, bi, d, sl) in zip(_scan_chunks(chains, bdmask), sinks):
        y_ref[bi, :, sl] = y
        s_ref[bi, d, :, sl] = s_new


def _scan(v, ops, cl, ctx_len):
    B, T, W = v.shape
    C = CHUNK
    NC = T // C
    nctx = ctx_len // C

    def back(s):
        return jnp.where(s < nctx, nctx - 1 - s, nctx + NC - 1 - s)

    BB = 4 if B % 4 == 0 else 2 if B % 2 == 0 else 1
    fwd = pl.BlockSpec((BB, C, W), lambda bb, s: (bb, s, 0))
    bwd = pl.BlockSpec((BB, C, W), lambda bb, s: (bb, back(s), 0))
    fwd_d = pl.BlockSpec((1, BB, C, 4 * W), lambda bb, s: (0, bb, s, 0))
    bwd_d = pl.BlockSpec((1, BB, C, 4 * W), lambda bb, s: (1, bb, back(s), 0))
    fwd_c = pl.BlockSpec((1, BB, 1, 8, W), lambda bb, s: (0, bb, s // CPT, 0, 0))
    bwd_c = pl.BlockSpec((1, BB, 1, 8, W), lambda bb, s: (1, bb, back(s) // CPT, 0, 0))
    out = jax.ShapeDtypeStruct((B, T, W), F32)
    return pl.pallas_call(
        functools.partial(_scan_kernel, nctx=nctx, n_chunks=NC),
        grid=(B // BB, NC),
        in_specs=[fwd, bwd, fwd_d, fwd_c, bwd_d, bwd_c],
        out_specs=[fwd, bwd],
        out_shape=[out, out],
        scratch_shapes=[pltpu.VMEM((BB, 2, HEAD, W), F32)],
        compiler_params=pltpu.CompilerParams(
            dimension_semantics=("parallel", "arbitrary"), vmem_limit_bytes=VMEM_LIMIT),
        name="scan",
    )(v, v, ops, cl, ops, cl)


def _mix_kernel(za_ref, zb_ref, m_ref, cb_ref, u_ref, up_ref, un_ref, yf_ref, yb_ref, prod_ref, v_ref,
                gl_ref, cw_ref, lng_ref, lnb_ref, ones_ref, gup_ref, wout_ref, x1_ref, *, n_tiles, split):
    i = pl.program_id(1)
    m = m_ref[0, 0]
    prev_ok = (i >= 2).astype(F32)
    next_ok = jnp.logical_and(i >= 1, i <= n_tiles - 2).astype(F32)
    u = u_ref[0]
    rowi = lax.broadcasted_iota(jnp.int32, (TM, 1), 0)
    um1 = jnp.where(rowi == 0, up_ref[0][7:8] * prev_ok, pltpu.roll(u, 1, 0))
    up1 = jnp.where(rowi == TM - 1, un_ref[0][0:1] * next_ok, pltpu.roll(u, TM - 1, 0))
    cw = cw_ref[...]
    yconv = cb_ref[0] * (cw[0:1] * um1 + cw[1:2] * u + cw[2:3] * up1)

    ones_bd = ones_ref[...]
    y = yf_ref[0] + yb_ref[0]
    mu = _dot01(y, ones_bd) * (1.0 / HEAD)
    dlt = y - mu
    var = _dot01(dlt * dlt, ones_bd) * (1.0 / HEAD)
    yn = dlt * lax.rsqrt(var + GN_EPS) * lng_ref[...] + lnb_ref[...]
    bonus = _dot01(prod_ref[0], ones_bd) * v_ref[0].astype(F32)
    gate = jnp.dot(jax.nn.sigmoid(gl_ref[0]).astype(BF16), gup_ref[...], preferred_element_type=F32)
    yrw = (yn + bonus) * gate

    out = (jnp.dot(yconv.astype(BF16), wout_ref[0:D_CONV, :], preferred_element_type=F32)
           + jnp.dot(yrw.astype(BF16), wout_ref[D_CONV:D_CONV + D_RWKV, :], preferred_element_type=F32))
    x1_ref[0] = _z_tile(za_ref, zb_ref, split) + m[2:3] * out


def _mix(za, zb, split, modsel, cb, u, yf, yb, prod, v, gl, cw, lng, lnb, ones_bd, gup, wout):
    B, T, D = cb.shape[0], cb.shape[1], za.shape[2]
    NTL = T // TM
    tok = pl.BlockSpec((1, TM, D_RWKV), lambda b, i: (b, i, 0))
    const = lambda shape: pl.BlockSpec(shape, lambda b, i: (0,) * len(shape))
    R8 = TM // 8
    return pl.pallas_call(
        functools.partial(_mix_kernel, n_tiles=NTL, split=split),
        grid=(B, NTL),
        in_specs=_z_specs(split, D) + [
            pl.BlockSpec((1, 1, 8, D), lambda b, i: (b, jnp.minimum(i, 1), 0, 0)),
            tok, tok,
            pl.BlockSpec((1, 8, D_CONV), lambda b, i: (b, jnp.maximum(i * R8 - 1, 0), 0)),
            pl.BlockSpec((1, 8, D_CONV), lambda b, i: (b, jnp.minimum((i + 1) * R8, T // 8 - 1), 0)),
            tok, tok, tok, tok, pl.BlockSpec((1, TM, G_LORA), lambda b, i: (b, i, 0)),
            const((8, D_CONV)), const((1, D_RWKV)), const((1, D_RWKV)), const((D_RWKV, D_RWKV)),
            const((G_LORA, D_RWKV)), const((D_CONV + D_RWKV, D)),
        ],
        out_specs=pl.BlockSpec((1, TM, D), lambda b, i: (b, i, 0)),
        out_shape=jax.ShapeDtypeStruct((B, T, D), F32),
        compiler_params=pltpu.CompilerParams(
            dimension_semantics=("parallel", "parallel"), vmem_limit_bytes=VMEM_LIMIT),
        name="mix",
    )(za, zb, modsel, cb, u, u, u, yf, yb, prod, v, gl, cw, lng, lnb, ones_bd, gup, wout)


def _ffn_kernel(x1_ref, xb_ref, m_ref, g2_ref, wup_ref, cw_ref, cbias_ref, wdn_ref, fg_ref,
                o_ref, gtop_ref, *, n_tiles, d_ff, final):
    i = pl.program_id(1)
    m = m_ref[0, 0]
    lat = (i >= 1).astype(F32)
    top_ok = i >= 2
    bot_ok = jnp.logical_and(i >= 1, i <= n_tiles - 2).astype(F32)
    rowi = lax.broadcasted_iota(jnp.int32, (TM, 1), 0)
    colp = jnp.where(i == 0, rowi, rowi & (GRID_W - 1))
    width = jnp.where(i == 0, TM, GRID_W)
    lmask = (colp > 0).astype(F32)
    rmask = (colp < width - 1).astype(F32)

    def norm2(x, keep):
        h = _rms(x) * g2_ref[...]
        return ((h * (1.0 + m[4:5]) + m[3:4]) * keep).astype(BF16)

    @pl.when(i == 0)
    def _():
        gtop_ref[...] = jnp.zeros_like(gtop_ref)

    x1 = x1_ref[0]
    hx = jnp.concatenate([norm2(x1, 1.0), norm2(xb_ref[0], bot_ok)], axis=0)
    hc = hx[0:TM]

    NB = 256

    def up(n):
        return (jnp.dot(hx, wup_ref[:, n:n + NB], preferred_element_type=F32),
                jnp.dot(hc, wup_ref[:, d_ff + n:d_ff + n + NB], preferred_element_type=F32))

    acc = jnp.zeros((TM, o_ref.shape[-1]), F32)
    nxt = up(0)
    prev = None
    for n in range(0, d_ff, NB):
        gfull, val = nxt
        if n + NB < d_ff:
            nxt = up(n + NB)
        if prev is not None:
            acc = acc + jnp.dot(prev, wdn_ref[n - NB:n, :], preferred_element_type=F32)
        g, gd = gfull[0:TM], gfull[GRID_W:GRID_W + TM]
        top = jnp.where(top_ok, gtop_ref[:, n:n + NB], 0.0)
        gu = jnp.concatenate([top, g[0:TM - GRID_W]], axis=0)
        gtop_ref[:, n:n + NB] = g[TM - GRID_W:TM]
        w = cw_ref[:, n:n + NB]
        wv = w * lat
        h0 = wv[0:1] * gu + w[3:4] * g + wv[6:7] * gd
        h1 = wv[1:2] * gu + w[4:5] * g + wv[7:8] * gd
        h2 = wv[2:3] * gu + w[5:6] * g + wv[8:9] * gd
        conv = (lmask * pltpu.roll(h0, 1, 0) + h1 + rmask * pltpu.roll(h2, TM - 1, 0)
                + cbias_ref[:, n:n + NB])
        prev = (_silu(conv) * val).astype(BF16)
    acc = acc + jnp.dot(prev, wdn_ref[d_ff - NB:d_ff, :], preferred_element_type=F32)
    x2 = x1 + m[5:6] * acc
    if final:
        x2 = _rms(x2) * fg_ref[...]
    o_ref[0] = x2


def _ffn(x1, modsel, g2, wup, cw, cbias, wdn, fg, final):
    B, T, D = x1.shape
    NTL = T // TM
    d_ff = wdn.shape[0]
    RW = TM // GRID_W
    const = lambda shape: pl.BlockSpec(shape, lambda b, i: (0,) * len(shape))
    if final:
        out_spec = pl.BlockSpec((1, TM, D), lambda b, i: (b, jnp.maximum(i - 1, 0), 0))
        out_shape = jax.ShapeDtypeStruct((B, T - TM, D), F32)
    else:
        out_spec = pl.BlockSpec((1, TM, D), lambda b, i: (b, i, 0))
        out_shape = jax.ShapeDtypeStruct((B, T, D), F32)
    return pl.pallas_call(
        functools.partial(_ffn_kernel, n_tiles=NTL, d_ff=d_ff, final=final),
        grid=(B, NTL),
        in_specs=[
            pl.BlockSpec((1, TM, D), lambda b, i: (b, i, 0)),
            pl.BlockSpec((1, GRID_W, D),
                         lambda b, i: (b, jnp.minimum((i + 1) * RW, T // GRID_W - 1), 0)),
            pl.BlockSpec((1, 1, 8, D), lambda b, i: (b, jnp.minimum(i, 1), 0, 0)),
            const((1, D)), const((D, 2 * d_ff)),
            const((16, d_ff)), const((1, d_ff)), const((d_ff, D)), const((1, D)),
        ],
        out_specs=out_spec,
        out_shape=out_shape,
        scratch_shapes=[pltpu.VMEM((GRID_W, d_ff), F32)],
        compiler_params=pltpu.CompilerParams(
            dimension_semantics=("parallel", "arbitrary"), vmem_limit_bytes=VMEM_LIMIT),
        name="ffn",
    )(x1, x1, modsel, g2, wup, cw, cbias, wdn, fg)


def kernel(x, c, ctx, c_ctx, ada_w, ada_b, norm1_g, norm2_g, w_in, conv_a_w, rw_w0, rw_w_up, rw_a0,
           rw_a_up, rw_k_k, rw_k_a, rw_r_k, rw_g_up, rw_ln_g, rw_ln_b, w_out, ffn_w_up, ffn_conv_w,
           ffn_conv_b, ffn_w_down, final_g):
    B, SEQ, D = x.shape
    CTX = ctx.shape[1]
    L = w_in.shape[0]
    d_ff = ffn_w_down.shape[1]
    assert CTX == TM and SEQ % TM == 0 and TM % GRID_W == 0 and CHUNK == HEAD and CPT <= 8
    assert w_in.shape[2] == 3 * D_CONV + 3 * D_RWKV + LORA_WA + G_LORA and d_ff % 256 == 0

    rows = -(-(B + 1) // 8) * 8
    c_rows = jnp.zeros((rows, D), F32).at[:B].set(c).at[B].set(c_ctx)
    mod = _ada(c_rows, ada_w, ada_b)

    hw = LORA_WA // 2
    zpad = jnp.zeros((L, hw, 2 * D_RWKV), F32)
    both = lambda t: jnp.concatenate([t[:, 0], t[:, 1]], axis=-1)
    wup_ext = jnp.concatenate([both(rw_w_up), zpad], axis=1).astype(BF16)
    aup_ext = jnp.concatenate([zpad, both(rw_a_up)], axis=1).astype(BF16)
    w0_2 = both(rw_w0[:, :, None, :])
    a0_2 = both(rw_a0[:, :, None, :])
    head_of = jnp.arange(D_RWKV) // HEAD
    ones_bd = (head_of[:, None] == head_of[None, :]).astype(BF16)
    tt = jnp.arange(TM)
    same = (tt[:, None] // CHUNK) == (tt[None, :] // CHUNK)
    tri = jnp.stack([same & (tt[None, :] <= tt[:, None]),
                     same & (tt[None, :] >= tt[:, None])]).astype(BF16)
    cw_a = jnp.zeros((L, 8, D_CONV), F32).at[:, :3].set(conv_a_w)
    cw_f = jnp.zeros((L, 16, d_ff), F32).at[:, :9].set(ffn_conv_w.reshape(L, 9, d_ff))

    za, zb = ctx, x
    for l in range(L):
        split = l == 0
        lat = mod[l, :B].reshape(B, 6, D)
        cm = jnp.broadcast_to(mod[l, B].reshape(1, 6, D), (B, 6, D))
        modsel = jnp.zeros((B, 2, 8, D), F32).at[:, 0, :6].set(cm).at[:, 1, :6].set(lat)

        cb, u, v, prod, gl, ops, cl = _proj(
            za, zb, split, modsel, norm1_g[l].reshape(1, D), w_in[l].astype(BF16), wup_ext[l], aup_ext[l],
            w0_2[l], a0_2[l], rw_k_k[l].reshape(1, -1), rw_k_a[l].reshape(1, -1),
            rw_r_k[l].reshape(1, -1), ones_bd, tri)
        yf, yb = _scan(v, ops, cl, CTX)
        x1 = _mix(
            za, zb, split, modsel, cb, u, yf, yb, prod, v, gl, cw_a[l], rw_ln_g[l].reshape(1, -1),
            rw_ln_b[l].reshape(1, -1), ones_bd, rw_g_up[l].astype(BF16), w_out[l].astype(BF16))
        za = zb = _ffn(x1, modsel, norm2_g[l].reshape(1, D), ffn_w_up[l].astype(BF16), cw_f[l],
                       ffn_conv_b[l].reshape(1, -1), ffn_w_down[l].astype(BF16), final_g.reshape(1, D),
                       final=(l == L - 1))
    return za
```

```python
import functools
import math

import jax
import jax.numpy as jnp
from jax import lax
from jax.experimental import pallas as pl
from jax.experimental.pallas import tpu as pltpu

F32 = jnp.float32
BF16 = jnp.bfloat16

HEAD = 64
D_CONV = 512
D_RWKV = 512
LORA_WA = 128
G_LORA = 128
GRID_W = 64
RMS_EPS = 1e-6
GN_EPS = 64e-5
DECAY_SCALE = math.exp(-0.5)

TM = 256
CHUNK = 64
CPT = TM // CHUNK
BT = 2
HPG = 2
GROUP = HPG * HEAD
VMEM_LIMIT = 56 * 1024 * 1024

NN = (((1,), (0,)), ((), ()))
NT = (((1,), (1,)), ((), ()))
TN = (((0,), (0,)), ((), ()))


def _silu(x):
    return x * jax.nn.sigmoid(x)


def _split(x):
    hi = x.astype(BF16)
    return hi, (x - hi.astype(F32)).astype(BF16)


def _dg(a, b, dims):
    return lax.dot_general(a, b, dims, preferred_element_type=F32)


def _mm3(a, b_hi, b_lo, dims):
    a_hi, a_lo = _split(a)
    return _dg(a_hi, b_hi, dims) + (_dg(a_hi, b_lo, dims) + _dg(a_lo, b_hi, dims))


def _rms(x):
    return x * lax.rsqrt(jnp.mean(x * x, axis=-1, keepdims=True) + RMS_EPS)


def _dot01(t, m01, left=False):
    hi, lo = _split(t)
    if left:
        return _dg(m01, hi, NN) + _dg(m01, lo, NN)
    return _dg(hi, m01, NN) + _dg(lo, m01, NN)


def _z_tile(za_ref, zb_ref, split, bi):
    if not split:
        return za_ref[bi]
    return jnp.where(pl.program_id(1) == 0, za_ref[bi], zb_ref[bi])


def _z_specs(split, bt, D):
    if split:
        return [pl.BlockSpec((bt, TM, D), lambda b, i: (b, 0, 0)),
                pl.BlockSpec((bt, TM, D), lambda b, i: (b, jnp.maximum(i - 1, 0), 0))]
    return [pl.BlockSpec((bt, TM, D), lambda b, i: (b, i, 0)),
            pl.BlockSpec((1, 8, D), lambda b, i: (0, 0, 0))]


def _rows_per_step(B):
    return BT if B % BT == 0 else 1


def _ada_kernel(c_ref, w_ref, b_ref, o_ref):
    s = _silu(c_ref[...])
    w_hi, w_lo = _split(w_ref[0])
    o_ref[0] = _mm3(s, w_hi, w_lo, NN) + b_ref[0]


def _ada(c_rows, ada_w, ada_b):
    L, D, N = ada_w.shape
    R = c_rows.shape[0]
    NB = 1536
    return pl.pallas_call(
        _ada_kernel,
        grid=(L, N // NB),
        in_specs=[
            pl.BlockSpec((R, D), lambda l, n: (0, 0)),
            pl.BlockSpec((1, D, NB), lambda l, n: (l, 0, n)),
            pl.BlockSpec((1, 1, NB), lambda l, n: (l, 0, n)),
        ],
        out_specs=pl.BlockSpec((1, R, NB), lambda l, n: (l, 0, n)),
        out_shape=jax.ShapeDtypeStruct((L, R, N), F32),
        compiler_params=pltpu.CompilerParams(
            dimension_semantics=("arbitrary", "arbitrary"), vmem_limit_bytes=VMEM_LIMIT),
        name="ada",
    )(c_rows, ada_w, ada_b.reshape(L, 1, N))


def _proj_kernel(za_ref, zb_ref, m_ref, g_ref, win_ref, wup_ref, aup_ref, w0_ref, a0_ref, kk_ref, ka_ref,
                 rk_ref, ones_ref, tri_ref,
                 cb_ref, u_ref, v_ref, prod_ref, gl_ref, ops_ref, cl_ref,
                 *, split):
    for bi in range(cb_ref.shape[0]):
        m = m_ref[bi, 0]
        h = _rms(_z_tile(za_ref, zb_ref, split, bi)) * g_ref[...]
        h = (h * (1.0 + m[1:2]) + m[0:1]).astype(BF16)

        def proj(lo, width):
            return jnp.dot(h, win_ref[:, lo:lo + width], preferred_element_type=F32)

        base = 3 * D_CONV
        lora = proj(base + 3 * D_RWKV, LORA_WA + G_LORA)
        k = proj(base + D_RWKV, D_RWKV)
        r = proj(base, D_RWKV)
        v = proj(base + 2 * D_RWKV, D_RWKV)
        wa = lora[:, :LORA_WA]
        gl = lora[:, LORA_WA:]
        ones_bd = ones_ref[...]

        kraw = k * kk_ref[...]
        kkn = kraw * lax.rsqrt(jnp.maximum(_dot01(kraw * kraw, ones_bd), 1e-24))
        lw2 = -DECAY_SCALE * jax.nn.sigmoid(
            w0_ref[...] + jnp.dot(jnp.tanh(wa).astype(BF16), wup_ref[...], preferred_element_type=F32))
        a2 = jax.nn.sigmoid(
            a0_ref[...] + jnp.dot(wa.astype(BF16), aup_ref[...], preferred_element_type=F32))

        cb_ref[bi] = proj(0, D_CONV)
        v_ref[bi] = v.astype(BF16)
        prod_ref[bi] = r * k * rk_ref[...]
        gl_ref[bi] = gl

        zero_rows = jnp.zeros((8 - CPT, D_RWKV), F32)
        conv_in = []
        for d in range(2):
            conv_in.append(proj((1 + d) * D_CONV, D_CONV))
            lw = lw2[:, d * D_RWKV:(d + 1) * D_RWKV]
            a = a2[:, d * D_RWKV:(d + 1) * D_RWKV]
            c = _dot01(lw, tri_ref[d], left=True)
            e_neg = jnp.exp(-c)
            ops = (kkn * jnp.exp(c - lw), r * jnp.exp(c), kkn * a * e_neg,
                   k * (1.0 + (a - 1.0) * ka_ref[...]) * e_neg)
            for j, x in enumerate(ops):
                ops_ref[d, bi, :, j * D_RWKV:(j + 1) * D_RWKV] = x.astype(BF16)
            last = CHUNK - 1 if d == 0 else 0
            cl_ref[d, bi, 0] = jnp.concatenate(
                [c[j * CHUNK + last:j * CHUNK + last + 1] for j in range(CPT)] + [zero_rows], axis=0)
        u_ref[bi] = conv_in[0] * conv_in[1]


def _proj(za, zb, split, modsel, g1, win, wup, aup, w0, a0, k_k, k_a, r_k, ones_bd, tri):
    B, D = za.shape[0], za.shape[2]
    T = za.shape[1] + zb.shape[1] if split else za.shape[1]
    NTL = T // TM
    P = win.shape[1]
    W = D_RWKV
    bt = _rows_per_step(B)
    tok = pl.BlockSpec((bt, TM, W), lambda b, i: (b, i, 0))
    tok2 = pl.BlockSpec((2, bt, TM, 4 * W), lambda b, i: (0, b, i, 0))
    const = lambda shape: pl.BlockSpec(shape, lambda b, i: (0,) * len(shape))
    f32_tok = jax.ShapeDtypeStruct((B, T, W), F32)
    bf_tok2 = jax.ShapeDtypeStruct((2, B, T, 4 * W), BF16)
    return pl.pallas_call(
        functools.partial(_proj_kernel, split=split),
        grid=(B // bt, NTL),
        in_specs=_z_specs(split, bt, D) + [
            pl.BlockSpec((bt, 1, 8, D), lambda b, i: (b, jnp.minimum(i, 1), 0, 0)),
            const((1, D)), const((D, P)), const((LORA_WA, 2 * W)), const((LORA_WA, 2 * W)),
            const((1, 2 * W)), const((1, 2 * W)), const((1, W)), const((1, W)),
            const((1, W)), const((W, W)), const((2, TM, TM)),
        ],
        out_specs=[tok, tok, tok, tok, pl.BlockSpec((bt, TM, G_LORA), lambda b, i: (b, i, 0)),
                   tok2,
                   pl.BlockSpec((2, bt, 1, 8, W), lambda b, i: (0, b, i, 0, 0))],
        out_shape=[f32_tok, f32_tok, jax.ShapeDtypeStruct((B, T, W), BF16), f32_tok,
                   jax.ShapeDtypeStruct((B, T, G_LORA), F32),
                   bf_tok2,
                   jax.ShapeDtypeStruct((2, B, NTL, 8, W), F32)],
        compiler_params=pltpu.CompilerParams(
            dimension_semantics=("parallel", "parallel"), vmem_limit_bytes=VMEM_LIMIT),
        name="proj",
    )(za, zb, modsel, g1, win, wup, aup, w0, a0, k_k, k_a, r_k, ones_bd, tri)


def _mm(a, x, dims):
    return _dg(a.astype(BF16), x.astype(BF16), dims)


def _scan_chunks(chains, bdmask):
    C = CHUNK
    row = lax.broadcasted_iota(jnp.int32, (C, GROUP), 0)
    col = lax.broadcasted_iota(jnp.int32, (C, GROUP), 1) & (HEAD - 1)
    eye = (row == col).astype(F32)
    blk16 = (row >> 4) == (col >> 4)
    lane_head = lax.broadcasted_iota(jnp.int32, (HEAD, GROUP), 1) >> 6
    zero_bf = jnp.zeros((), BF16)

    def each(f, *lists):
        return [f(*a) for a in zip(*lists)]

    def bd(x):
        return jnp.where(bdmask, jnp.concatenate([x.astype(BF16)] * HPG, axis=0), zero_bf)

    def bd2(x, y, axis):
        return jnp.concatenate([bd(x), bd(y)], axis=axis)

    def rows(*xs):
        return jnp.concatenate([x.astype(BF16) for x in xs], axis=0)

    def diag_blocks(full):
        out = jnp.where(lane_head == 0, full[0:HEAD], 0.0)
        for hh in range(1, HPG):
            out = out + jnp.where(lane_head == hh, full[hh * HEAD:(hh + 1) * HEAD], 0.0)
        return out

    rev = [ch["rev"] for ch in chains]
    at, rt, bt, kt, v, p_c, s0 = ([ch[n] for ch in chains]
                                  for n in ("at", "rt", "bt", "kt", "v", "p_c", "s0"))
    strict = [(col > row) if rv else (col < row) for rv in rev]
    incl = [(col >= row) if rv else (col <= row) for rv in rev]
    bh = each(lambda x, p: x * p, bt, p_c)
    kh = each(lambda x, p: x * p, kt, p_c)

    abk = each(lambda a_, r_, x, y: _mm(rows(a_, r_), bd2(x, y, 0), NT), at, rt, bt, kt)
    lmat = each(lambda m_, x: jnp.where(m_, x[:C, :GROUP], 0.0), strict, abk)
    aak = each(lambda m_, x: jnp.where(m_, x[:C, GROUP:], 0.0), strict, abk)
    arb = each(lambda m_, x: jnp.where(m_, x[C:, :GROUP], 0.0), incl, abk)
    ark = each(lambda m_, x: jnp.where(m_, x[C:, GROUP:], 0.0), incl, abk)

    S = 16
    lane16 = (lax.broadcasted_iota(jnp.int32, (S, GROUP), 1) & (HEAD - 1)) >> 4
    eye_p = (lax.broadcasted_iota(jnp.int32, (S, GROUP), 0)
             == (lax.broadcasted_iota(jnp.int32, (S, GROUP), 1) & (S - 1))).astype(F32)
    mask16 = (lax.broadcasted_iota(jnp.int32, (GROUP, GROUP), 0) >> 4) == \
             (lax.broadcasted_iota(jnp.int32, (GROUP, GROUP), 1) >> 4)

    def pack16(full):
        out = jnp.where(lane16 == 0, full[0:S], 0.0)
        for j in range(1, HEAD // S):
            out = out + jnp.where(lane16 == j, full[j * S:(j + 1) * S], 0.0)
        return out

    def unpack16(p):
        return jnp.concatenate([jnp.where(lane16 == j, p, 0.0) for j in range(HEAD // S)], axis=0)

    def bd16(p):
        return jnp.where(mask16, jnp.concatenate([p.astype(BF16)] * (GROUP // S), axis=0), zero_bf)

    ld = each(pack16, lmat)
    lo = each(lambda x: jnp.where(blk16, 0.0, x), lmat)
    p1 = each(lambda x: eye_p - x, ld)
    l2 = each(lambda x: _mm(x, bd16(x), NN), ld)
    t = each(lambda p, x: _mm(rows(p, x), bd16(x), NN), p1, l2)
    p2 = each(lambda p, x: p + x[:S], p1, t)
    l4 = each(lambda x: x[S:], t)
    t = each(lambda p, x: _mm(rows(p, x), bd16(x), NN), p2, l4)
    p3 = each(lambda p, x: p + x[:S], p2, t)
    l8 = each(lambda x: x[S:], t)
    dinv = each(lambda p, x: unpack16(p + _mm(p, bd16(x), NN)), p3, l8)
    mm = each(lambda d_, x: _mm(d_, bd(x), NN), dinv, lo)
    mm2 = each(lambda x: _mm(x, bd(x), NN), mm)
    g = each(lambda x, x2: eye - x + x2 - _mm(x, bd(x2), NN), mm, mm2)
    tinv = each(lambda g_, d_: _mm(g_, bd(d_), NN), g, dinv)

    t = each(lambda x, y, v_: _mm(rows(x, y), bd(v_), NN), aak, ark, v)
    akv = each(lambda x: x[:C], t)
    arkv = each(lambda x: x[C:], t)
    t = each(lambda t_, x, y: _mm(t_, bd2(x, y, 1), NN), tinv, at, akv)
    wm = each(lambda x: x[:, :GROUP], t)
    um = each(lambda x: x[:, GROUP:], t)
    t = each(lambda a_, x, y: _mm(a_, bd2(x, y, 1), NN), arb, wm, um)
    qh = each(lambda r_, x: r_.astype(F32) - x[:, :GROUP], rt, t)
    yl = each(lambda y_, x: y_ - x[:, GROUP:], arkv, t)
    t = each(lambda q_, w_, s_: _mm(rows(q_, w_), bd(s_), NT), qh, wm, s0)
    y = each(lambda x, y_: x[:C] + y_, t, yl)
    ds = each(lambda v_, u_, x, k_, b_: diag_blocks(_mm(rows(v_, u_, x[C:]), rows(k_, -b_, -b_), TN)),
              v, um, t, kh, bh)
    s_new = each(lambda s_, p_, d_: s_ * p_ + d_, s0, p_c, ds)
    return list(zip(y, s_new))


def _scan_kernel(vf_ref, vb_ref, opf_ref, clf_ref, opb_ref, clb_ref, yf_ref, yb_ref, s_ref,
                 *, nctx, n_chunks):
    s = pl.program_id(1)

    @pl.when(s == 0)
    def _():
        s_ref[...] = jnp.zeros_like(s_ref)

    back = jnp.where(s < nctx, nctx - 1 - s, nctx + n_chunks - 1 - s)
    bdmask = (lax.broadcasted_iota(jnp.int32, (GROUP, GROUP), 0) >> 6) == \
             (lax.broadcasted_iota(jnp.int32, (GROUP, GROUP), 1) >> 6)
    dirs = ((False, s, vf_ref, opf_ref, clf_ref, yf_ref), (True, back, vb_ref, opb_ref, clb_ref, yb_ref))
    chains, sinks = [], []
    for bi in range(s_ref.shape[0]):
        for d, (rev, chunk, v_ref, op_ref, cl_ref, y_ref) in enumerate(dirs):
            p_c = jnp.exp(cl_ref[0, bi, 0, pl.ds(chunk % CPT, 1), :])
            for q in range(D_RWKV // GROUP):
                sl = slice(q * GROUP, (q + 1) * GROUP)
                at, rt, bt, kt = (op_ref[0, bi, :, j * D_RWKV + q * GROUP:j * D_RWKV + (q + 1) * GROUP]
                                  for j in range(4))
                chains.append(dict(rev=rev, at=at, rt=rt, bt=bt, kt=kt,
                                   v=v_ref[bi, :, sl], p_c=p_c[:, sl], s0=s_ref[bi, d, :, sl]))
                sinks.append((y_ref, bi, d, sl))
    for (y, s_new), (y_ref, bi, d, sl) in zip(_scan_chunks(chains, bdmask), sinks):
        y_ref[bi, :, sl] = y
        s_ref[bi, d, :, sl] = s_new


def _scan(v, ops, cl, ctx_len):
    B, T, W = v.shape
    C = CHUNK
    NC = T // C
    nctx = ctx_len // C

    def back(s):
        return jnp.where(s < nctx, nctx - 1 - s, nctx + NC - 1 - s)

    BB = 4 if B % 4 == 0 else 2 if B % 2 == 0 else 1
    fwd = pl.BlockSpec((BB, C, W), lambda bb, s: (bb, s, 0))
    bwd = pl.BlockSpec((BB, C, W), lambda bb, s: (bb, back(s), 0))
    fwd_d = pl.BlockSpec((1, BB, C, 4 * W), lambda bb, s: (0, bb, s, 0))
    bwd_d = pl.BlockSpec((1, BB, C, 4 * W), lambda bb, s: (1, bb, back(s), 0))
    fwd_c = pl.BlockSpec((1, BB, 1, 8, W), lambda bb, s: (0, bb, s // CPT, 0, 0))
    bwd_c = pl.BlockSpec((1, BB, 1, 8, W), lambda bb, s: (1, bb, back(s) // CPT, 0, 0))
    out = jax.ShapeDtypeStruct((B, T, W), F32)
    return pl.pallas_call(
        functools.partial(_scan_kernel, nctx=nctx, n_chunks=NC),
        grid=(B // BB, NC),
        in_specs=[fwd, bwd, fwd_d, fwd_c, bwd_d, bwd_c],
        out_specs=[fwd, bwd],
        out_shape=[out, out],
        scratch_shapes=[pltpu.VMEM((BB, 2, HEAD, W), F32)],
        compiler_params=pltpu.CompilerParams(
            dimension_semantics=("parallel", "arbitrary"), vmem_limit_bytes=VMEM_LIMIT),
        name="scan",
    )(v, v, ops, cl, ops, cl)


def _mix_kernel(za_ref, zb_ref, m_ref, cb_ref, u_ref, up_ref, un_ref, yf_ref, yb_ref, prod_ref, v_ref,
                gl_ref, cw_ref, lng_ref, lnb_ref, ones_ref, gup_ref, wout_ref, x1_ref, *, n_tiles, split):
    i = pl.program_id(1)
    for bi in range(x1_ref.shape[0]):
        m = m_ref[bi, 0]
        prev_ok = (i >= 2).astype(F32)
        next_ok = jnp.logical_and(i >= 1, i <= n_tiles - 2).astype(F32)
        u = u_ref[bi]
        rowi = lax.broadcasted_iota(jnp.int32, (TM, 1), 0)
        um1 = jnp.where(rowi == 0, up_ref[bi][7:8] * prev_ok, pltpu.roll(u, 1, 0))
        up1 = jnp.where(rowi == TM - 1, un_ref[bi][0:1] * next_ok, pltpu.roll(u, TM - 1, 0))
        cw = cw_ref[...]
        yconv = cb_ref[bi] * (cw[0:1] * um1 + cw[1:2] * u + cw[2:3] * up1)

        ones_bd = ones_ref[...]
        y = yf_ref[bi] + yb_ref[bi]
        mu = _dot01(y, ones_bd) * (1.0 / HEAD)
        dlt = y - mu
        var = _dot01(dlt * dlt, ones_bd) * (1.0 / HEAD)
        yn = dlt * lax.rsqrt(var + GN_EPS) * lng_ref[...] + lnb_ref[...]
        bonus = _dot01(prod_ref[bi], ones_bd) * v_ref[bi].astype(F32)
        gate = jnp.dot(jax.nn.sigmoid(gl_ref[bi]).astype(BF16), gup_ref[...], preferred_element_type=F32)
        yrw = (yn + bonus) * gate

        out = (jnp.dot(yconv.astype(BF16), wout_ref[0:D_CONV, :], preferred_element_type=F32)
               + jnp.dot(yrw.astype(BF16), wout_ref[D_CONV:D_CONV + D_RWKV, :], preferred_element_type=F32))
        x1_ref[bi] = _z_tile(za_ref, zb_ref, split, bi) + m[2:3] * out


def _mix(za, zb, split, modsel, cb, u, yf, yb, prod, v, gl, cw, lng, lnb, ones_bd, gup, wout):
    B, T, D = cb.shape[0], cb.shape[1], za.shape[2]
    NTL = T // TM
    bt = _rows_per_step(B)
    tok = pl.BlockSpec((bt, TM, D_RWKV), lambda b, i: (b, i, 0))
    const = lambda shape: pl.BlockSpec(shape, lambda b, i: (0,) * len(shape))
    R8 = TM // 8
    return pl.pallas_call(
        functools.partial(_mix_kernel, n_tiles=NTL, split=split),
        grid=(B // bt, NTL),
        in_specs=_z_specs(split, bt, D) + [
            pl.BlockSpec((bt, 1, 8, D), lambda b, i: (b, jnp.minimum(i, 1), 0, 0)),
            tok, tok,
            pl.BlockSpec((bt, 8, D_CONV), lambda b, i: (b, jnp.maximum(i * R8 - 1, 0), 0)),
            pl.BlockSpec((bt, 8, D_CONV), lambda b, i: (b, jnp.minimum((i + 1) * R8, T // 8 - 1), 0)),
            tok, tok, tok, tok, pl.BlockSpec((bt, TM, G_LORA), lambda b, i: (b, i, 0)),
            const((8, D_CONV)), const((1, D_RWKV)), const((1, D_RWKV)), const((D_RWKV, D_RWKV)),
            const((G_LORA, D_RWKV)), const((D_CONV + D_RWKV, D)),
        ],
        out_specs=pl.BlockSpec((bt, TM, D), lambda b, i: (b, i, 0)),
        out_shape=jax.ShapeDtypeStruct((B, T, D), F32),
        compiler_params=pltpu.CompilerParams(
            dimension_semantics=("parallel", "parallel"), vmem_limit_bytes=VMEM_LIMIT),
        name="mix",
    )(za, zb, modsel, cb, u, u, u, yf, yb, prod, v, gl, cw, lng, lnb, ones_bd, gup, wout)


def _ffn_kernel(x1_ref, xb_ref, m_ref, g2_ref, wup_ref, cw_ref, cbias_ref, wdn_ref, fg_ref,
                o_ref, gtop_ref, *, n_tiles, d_ff, final):
    i = pl.program_id(1)
    lat = (i >= 1).astype(F32)
    top_ok = i >= 2
    bot_ok = jnp.logical_and(i >= 1, i <= n_tiles - 2).astype(F32)
    rowi = lax.broadcasted_iota(jnp.int32, (TM, 1), 0)
    colp = jnp.where(i == 0, rowi, rowi & (GRID_W - 1))
    width = jnp.where(i == 0, TM, GRID_W)
    lmask = (colp > 0).astype(F32)
    rmask = (colp < width - 1).astype(F32)
    NB = 256

    @pl.when(i == 0)
    def _():
        gtop_ref[...] = jnp.zeros_like(gtop_ref)

    for bi in range(x1_ref.shape[0]):
        m = m_ref[bi, 0]

        def norm2(x, keep):
            h = _rms(x) * g2_ref[...]
            return ((h * (1.0 + m[4:5]) + m[3:4]) * keep).astype(BF16)

        x1 = x1_ref[bi]
        hx = jnp.concatenate([norm2(x1, 1.0), norm2(xb_ref[bi], bot_ok)], axis=0)
        hc = hx[0:TM]

        def up(n):
            return (jnp.dot(hx, wup_ref[:, n:n + NB], preferred_element_type=F32),
                    jnp.dot(hc, wup_ref[:, d_ff + n:d_ff + n + NB], preferred_element_type=F32))

        acc = jnp.zeros((TM, o_ref.shape[-1]), F32)
        nxt = up(0)
        prev = None
        for n in range(0, d_ff, NB):
            gfull, val = nxt
            if n + NB < d_ff:
                nxt = up(n + NB)
            if prev is not None:
                acc = acc + jnp.dot(prev, wdn_ref[n - NB:n, :], preferred_element_type=F32)
            g, gd = gfull[0:TM], gfull[GRID_W:GRID_W + TM]
            top = jnp.where(top_ok, gtop_ref[bi, :, n:n + NB], 0.0)
            gu = jnp.concatenate([top, g[0:TM - GRID_W]], axis=0)
            gtop_ref[bi, :, n:n + NB] = g[TM - GRID_W:TM]
            w = cw_ref[:, n:n + NB]
            wv = w * lat
            h0 = wv[0:1] * gu + w[3:4] * g + wv[6:7] * gd
            h1 = wv[1:2] * gu + w[4:5] * g + wv[7:8] * gd
            h2 = wv[2:3] * gu + w[5:6] * g + wv[8:9] * gd
            conv = (lmask * pltpu.roll(h0, 1, 0) + h1 + rmask * pltpu.roll(h2, TM - 1, 0)
                    + cbias_ref[:, n:n + NB])
            prev = (_silu(conv) * val).astype(BF16)
        acc = acc + jnp.dot(prev, wdn_ref[d_ff - NB:d_ff, :], preferred_element_type=F32)
        x2 = x1 + m[5:6] * acc
        if final:
            x2 = _rms(x2) * fg_ref[...]
        o_ref[bi] = x2


def _ffn(x1, modsel, g2, wup, cw, cbias, wdn, fg, final):
    B, T, D = x1.shape
    NTL = T // TM
    d_ff = wdn.shape[0]
    RW = TM // GRID_W
    bt = _rows_per_step(B)
    const = lambda shape: pl.BlockSpec(shape, lambda b, i: (0,) * len(shape))
    if final:
        out_spec = pl.BlockSpec((bt, TM, D), lambda b, i: (b, jnp.maximum(i - 1, 0), 0))
        out_shape = jax.ShapeDtypeStruct((B, T - TM, D), F32)
    else:
        out_spec = pl.BlockSpec((bt, TM, D), lambda b, i: (b, i, 0))
        out_shape = jax.ShapeDtypeStruct((B, T, D), F32)
    return pl.pallas_call(
        functools.partial(_ffn_kernel, n_tiles=NTL, d_ff=d_ff, final=final),
        grid=(B // bt, NTL),
        in_specs=[
            pl.BlockSpec((bt, TM, D), lambda b, i: (b, i, 0)),
            pl.BlockSpec((bt, GRID_W, D),
                         lambda b, i: (b, jnp.minimum((i + 1) * RW, T // GRID_W - 1), 0)),
            pl.BlockSpec((bt, 1, 8, D), lambda b, i: (b, jnp.minimum(i, 1), 0, 0)),
            const((1, D)), const((D, 2 * d_ff)),
            const((16, d_ff)), const((1, d_ff)), const((d_ff, D)), const((1, D)),
        ],
        out_specs=out_spec,
        out_shape=out_shape,
        scratch_shapes=[pltpu.VMEM((bt, GRID_W, d_ff), F32)],
        compiler_params=pltpu.CompilerParams(
            dimension_semantics=("parallel", "arbitrary"), vmem_limit_bytes=VMEM_LIMIT),
        name="ffn",
    )(x1, x1, modsel, g2, wup, cw, cbias, wdn, fg)


def kernel(x, c, ctx, c_ctx, ada_w, ada_b, norm1_g, norm2_g, w_in, conv_a_w, rw_w0, rw_w_up, rw_a0,
           rw_a_up, rw_k_k, rw_k_a, rw_r_k, rw_g_up, rw_ln_g, rw_ln_b, w_out, ffn_w_up, ffn_conv_w,
           ffn_conv_b, ffn_w_down, final_g):
    B, SEQ, D = x.shape
    CTX = ctx.shape[1]
    L = w_in.shape[0]
    d_ff = ffn_w_down.shape[1]
    assert CTX == TM and SEQ % TM == 0 and TM % GRID_W == 0 and CHUNK == HEAD and CPT <= 8
    assert w_in.shape[2] == 3 * D_CONV + 3 * D_RWKV + LORA_WA + G_LORA and d_ff % 256 == 0

    rows = -(-(B + 1) // 8) * 8
    c_rows = jnp.zeros((rows, D), F32).at[:B].set(c).at[B].set(c_ctx)
    mod = _ada(c_rows, ada_w, ada_b)

    hw = LORA_WA // 2
    zpad = jnp.zeros((L, hw, 2 * D_RWKV), F32)
    both = lambda t: jnp.concatenate([t[:, 0], t[:, 1]], axis=-1)
    wup_ext = jnp.concatenate([both(rw_w_up), zpad], axis=1).astype(BF16)
    aup_ext = jnp.concatenate([zpad, both(rw_a_up)], axis=1).astype(BF16)
    w0_2 = both(rw_w0[:, :, None, :])
    a0_2 = both(rw_a0[:, :, None, :])
    head_of = jnp.arange(D_RWKV) // HEAD
    ones_bd = (head_of[:, None] == head_of[None, :]).astype(BF16)
    tt = jnp.arange(TM)
    same = (tt[:, None] // CHUNK) == (tt[None, :] // CHUNK)
    tri = jnp.stack([same & (tt[None, :] <= tt[:, None]),
                     same & (tt[None, :] >= tt[:, None])]).astype(BF16)
    cw_a = jnp.zeros((L, 8, D_CONV), F32).at[:, :3].set(conv_a_w)
    cw_f = jnp.zeros((L, 16, d_ff), F32).at[:, :9].set(ffn_conv_w.reshape(L, 9, d_ff))

    za, zb = ctx, x
    for l in range(L):
        split = l == 0
        lat = mod[l, :B].reshape(B, 6, D)
        cm = jnp.broadcast_to(mod[l, B].reshape(1, 6, D), (B, 6, D))
        modsel = jnp.zeros((B, 2, 8, D), F32).at[:, 0, :6].set(cm).at[:, 1, :6].set(lat)

        cb, u, v, prod, gl, ops, cl = _proj(
            za, zb, split, modsel, norm1_g[l].reshape(1, D), w_in[l].astype(BF16), wup_ext[l], aup_ext[l],
            w0_2[l], a0_2[l], rw_k_k[l].reshape(1, -1), rw_k_a[l].reshape(1, -1),
            rw_r_k[l].reshape(1, -1), ones_bd, tri)
        yf, yb = _scan(v, ops, cl, CTX)
        x1 = _mix(
            za, zb, split, modsel, cb, u, yf, yb, prod, v, gl, cw_a[l], rw_ln_g[l].reshape(1, -1),
            rw_ln_b[l].reshape(1, -1), ones_bd, rw_g_up[l].astype(BF16), w_out[l].astype(BF16))
        za = zb = _ffn(x1, modsel, norm2_g[l].reshape(1, D), ffn_w_up[l].astype(BF16), cw_f[l],
                       ffn_conv_b[l].reshape(1, -1), ffn_w_down[l].astype(BF16), final_g.reshape(1, D),
                       final=(l == L - 1))
    return za
```

```python
import functools
import math

import jax
import jax.numpy as jnp
from jax import lax
from jax.experimental import pallas as pl
from jax.experimental.pallas import tpu as pltpu

F32 = jnp.float32
BF16 = jnp.bfloat16

HEAD = 64
D_CONV = 512
D_RWKV = 512
LORA_WA = 128
G_LORA = 128
GRID_W = 64
RMS_EPS = 1e-6
GN_EPS = 64e-5
DECAY_SCALE = math.exp(-0.5)

TM = 256
CHUNK = 64
CPT = TM // CHUNK
BT = 2
HPG = 2
GROUP = HPG * HEAD
VMEM_LIMIT = 56 * 1024 * 1024

NN = (((1,), (0,)), ((), ()))
NT = (((1,), (1,)), ((), ()))
TN = (((0,), (0,)), ((), ()))


def _silu(x):
    return x * jax.nn.sigmoid(x)


def _split(x):
    hi = x.astype(BF16)
    return hi, (x - hi.astype(F32)).astype(BF16)


def _dg(a, b, dims):
    return lax.dot_general(a, b, dims, preferred_element_type=F32)


def _mm3(a, b_hi, b_lo, dims):
    a_hi, a_lo = _split(a)
    return _dg(a_hi, b_hi, dims) + (_dg(a_hi, b_lo, dims) + _dg(a_lo, b_hi, dims))


def _rms(x):
    return x * lax.rsqrt(jnp.mean(x * x, axis=-1, keepdims=True) + RMS_EPS)


def _dot01(t, m01, left=False):
    hi, lo = _split(t)
    if left:
        return _dg(m01, hi, NN) + _dg(m01, lo, NN)
    return _dg(hi, m01, NN) + _dg(lo, m01, NN)


def _z_tile(za_ref, zb_ref, split, bi):
    if not split:
        return za_ref[bi]
    return jnp.where(pl.program_id(1) == 0, za_ref[bi], zb_ref[bi])


def _z_specs(split, bt, D):
    if split:
        return [pl.BlockSpec((bt, TM, D), lambda b, i: (b, 0, 0)),
                pl.BlockSpec((bt, TM, D), lambda b, i: (b, jnp.maximum(i - 1, 0), 0))]
    return [pl.BlockSpec((bt, TM, D), lambda b, i: (b, i, 0)),
            pl.BlockSpec((1, 8, D), lambda b, i: (0, 0, 0))]


def _rows_per_step(B):
    return BT if B % BT == 0 else 1


def _ada_kernel(c_ref, w_ref, b_ref, o_ref):
    s = _silu(c_ref[...])
    w_hi, w_lo = _split(w_ref[0])
    o_ref[0] = _mm3(s, w_hi, w_lo, NN) + b_ref[0]


def _ada(c_rows, ada_w, ada_b):
    L, D, N = ada_w.shape
    R = c_rows.shape[0]
    NB = 1536
    return pl.pallas_call(
        _ada_kernel,
        grid=(L, N // NB),
        in_specs=[
            pl.BlockSpec((R, D), lambda l, n: (0, 0)),
            pl.BlockSpec((1, D, NB), lambda l, n: (l, 0, n)),
            pl.BlockSpec((1, 1, NB), lambda l, n: (l, 0, n)),
        ],
        out_specs=pl.BlockSpec((1, R, NB), lambda l, n: (l, 0, n)),
        out_shape=jax.ShapeDtypeStruct((L, R, N), F32),
        compiler_params=pltpu.CompilerParams(
            dimension_semantics=("arbitrary", "arbitrary"), vmem_limit_bytes=VMEM_LIMIT),
        name="ada",
    )(c_rows, ada_w, ada_b.reshape(L, 1, N))


def _proj_kernel(za_ref, zb_ref, m_ref, g_ref, win_ref, wup_ref, aup_ref, w0_ref, a0_ref, kk_ref, ka_ref,
                 rk_ref, ones_ref, tri_ref,
                 cb_ref, u_ref, v_ref, prod_ref, gl_ref, ops_ref, cl_ref,
                 *, split):
    for bi in range(cb_ref.shape[0]):
        m = m_ref[bi, 0]
        h = _rms(_z_tile(za_ref, zb_ref, split, bi)) * g_ref[...]
        h = (h * (1.0 + m[1:2]) + m[0:1]).astype(BF16)

        def proj(lo, width):
            return jnp.dot(h, win_ref[:, lo:lo + width], preferred_element_type=F32)

        base = 3 * D_CONV
        lora = proj(base + 3 * D_RWKV, LORA_WA + G_LORA)
        k = proj(base + D_RWKV, D_RWKV)
        r = proj(base, D_RWKV)
        v = proj(base + 2 * D_RWKV, D_RWKV)
        wa = lora[:, :LORA_WA]
        gl = lora[:, LORA_WA:]
        ones_bd = ones_ref[...]

        kraw = k * kk_ref[...]
        kkn = kraw * lax.rsqrt(jnp.maximum(_dot01(kraw * kraw, ones_bd), 1e-24))
        lw2 = -DECAY_SCALE * jax.nn.sigmoid(
            w0_ref[...] + jnp.dot(jnp.tanh(wa).astype(BF16), wup_ref[...], preferred_element_type=F32))
        a2 = jax.nn.sigmoid(
            a0_ref[...] + jnp.dot(wa.astype(BF16), aup_ref[...], preferred_element_type=F32))

        cb_ref[bi] = proj(0, D_CONV)
        v_ref[bi] = v.astype(BF16)
        prod_ref[bi] = r * k * rk_ref[...]
        gl_ref[bi] = gl

        zero_rows = jnp.zeros((8 - CPT, D_RWKV), F32)
        conv_in = []
        for d in range(2):
            conv_in.append(proj((1 + d) * D_CONV, D_CONV))
            lw = lw2[:, d * D_RWKV:(d + 1) * D_RWKV]
            a = a2[:, d * D_RWKV:(d + 1) * D_RWKV]
            c = _dot01(lw, tri_ref[d], left=True)
            e_neg = jnp.exp(-c)
            ops = (kkn * jnp.exp(c - lw), r * jnp.exp(c), kkn * a * e_neg,
                   k * (1.0 + (a - 1.0) * ka_ref[...]) * e_neg)
            for j, x in enumerate(ops):
                ops_ref[d, bi, :, j * D_RWKV:(j + 1) * D_RWKV] = x.astype(BF16)
            last = CHUNK - 1 if d == 0 else 0
            cl_ref[d, bi, 0] = jnp.concatenate(
                [c[j * CHUNK + last:j * CHUNK + last + 1] for j in range(CPT)] + [zero_rows], axis=0)
        u_ref[bi] = conv_in[0] * conv_in[1]


def _proj(za, zb, split, modsel, g1, win, wup, aup, w0, a0, k_k, k_a, r_k, ones_bd, tri):
    B, D = za.shape[0], za.shape[2]
    T = za.shape[1] + zb.shape[1] if split else za.shape[1]
    NTL = T // TM
    P = win.shape[1]
    W = D_RWKV
    bt = _rows_per_step(B)
    tok = pl.BlockSpec((bt, TM, W), lambda b, i: (b, i, 0))
    tok2 = pl.BlockSpec((2, bt, TM, 4 * W), lambda b, i: (0, b, i, 0))
    const = lambda shape: pl.BlockSpec(shape, lambda b, i: (0,) * len(shape))
    f32_tok = jax.ShapeDtypeStruct((B, T, W), F32)
    bf_tok2 = jax.ShapeDtypeStruct((2, B, T, 4 * W), BF16)
    return pl.pallas_call(
        functools.partial(_proj_kernel, split=split),
        grid=(B // bt, NTL),
        in_specs=_z_specs(split, bt, D) + [
            pl.BlockSpec((bt, 1, 8, D), lambda b, i: (b, jnp.minimum(i, 1), 0, 0)),
            const((1, D)), const((D, P)), const((LORA_WA, 2 * W)), const((LORA_WA, 2 * W)),
            const((1, 2 * W)), const((1, 2 * W)), const((1, W)), const((1, W)),
            const((1, W)), const((W, W)), const((2, TM, TM)),
        ],
        out_specs=[tok, tok, tok, tok, pl.BlockSpec((bt, TM, G_LORA), lambda b, i: (b, i, 0)),
                   tok2,
                   pl.BlockSpec((2, bt, 1, 8, W), lambda b, i: (0, b, i, 0, 0))],
        out_shape=[f32_tok, f32_tok, jax.ShapeDtypeStruct((B, T, W), BF16), f32_tok,
                   jax.ShapeDtypeStruct((B, T, G_LORA), F32),
                   bf_tok2,
                   jax.ShapeDtypeStruct((2, B, NTL, 8, W), F32)],
        compiler_params=pltpu.CompilerParams(
            dimension_semantics=("parallel", "parallel"), vmem_limit_bytes=VMEM_LIMIT),
        name="proj",
    )(za, zb, modsel, g1, win, wup, aup, w0, a0, k_k, k_a, r_k, ones_bd, tri)


def _mm(a, x, dims):
    return _dg(a.astype(BF16), x.astype(BF16), dims)


def _scan_chunks(chains, bdmask):
    C = CHUNK
    row = lax.broadcasted_iota(jnp.int32, (C, GROUP), 0)
    col = lax.broadcasted_iota(jnp.int32, (C, GROUP), 1) & (HEAD - 1)
    eye = (row == col).astype(F32)
    blk16 = (row >> 4) == (col >> 4)
    lane_head = lax.broadcasted_iota(jnp.int32, (HEAD, GROUP), 1) >> 6
    zero_bf = jnp.zeros((), BF16)

    def each(f, *lists):
        return [f(*a) for a in zip(*lists)]

    def bd(x):
        return jnp.where(bdmask, jnp.concatenate([x.astype(BF16)] * HPG, axis=0), zero_bf)

    def bd2(x, y, axis):
        return jnp.concatenate([bd(x), bd(y)], axis=axis)

    def rows(*xs):
        return jnp.concatenate([x.astype(BF16) for x in xs], axis=0)

    def diag_blocks(full):
        out = jnp.where(lane_head == 0, full[0:HEAD], 0.0)
        for hh in range(1, HPG):
            out = out + jnp.where(lane_head == hh, full[hh * HEAD:(hh + 1) * HEAD], 0.0)
        return out

    rev = [ch["rev"] for ch in chains]
    at, rt, bt, kt, v, p_c, s0 = ([ch[n] for ch in chains]
                                  for n in ("at", "rt", "bt", "kt", "v", "p_c", "s0"))
    strict = [(col > row) if rv else (col < row) for rv in rev]
    incl = [(col >= row) if rv else (col <= row) for rv in rev]
    bh = each(lambda x, p: x * p, bt, p_c)
    kh = each(lambda x, p: x * p, kt, p_c)

    abk = each(lambda a_, r_, x, y: _mm(rows(a_, r_), bd2(x, y, 0), NT), at, rt, bt, kt)
    lmat = each(lambda m_, x: jnp.where(m_, x[:C, :GROUP], 0.0), strict, abk)
    aak = each(lambda m_, x: jnp.where(m_, x[:C, GROUP:], 0.0), strict, abk)
    arb = each(lambda m_, x: jnp.where(m_, x[C:, :GROUP], 0.0), incl, abk)
    ark = each(lambda m_, x: jnp.where(m_, x[C:, GROUP:], 0.0), incl, abk)

    S = 16
    lane16 = (lax.broadcasted_iota(jnp.int32, (S, GROUP), 1) & (HEAD - 1)) >> 4
    eye_p = (lax.broadcasted_iota(jnp.int32, (S, GROUP), 0)
             == (lax.broadcasted_iota(jnp.int32, (S, GROUP), 1) & (S - 1))).astype(F32)
    mask16 = (lax.broadcasted_iota(jnp.int32, (GROUP, GROUP), 0) >> 4) == \
             (lax.broadcasted_iota(jnp.int32, (GROUP, GROUP), 1) >> 4)

    def pack16(full):
        out = jnp.where(lane16 == 0, full[0:S], 0.0)
        for j in range(1, HEAD // S):
            out = out + jnp.where(lane16 == j, full[j * S:(j + 1) * S], 0.0)
        return out

    def unpack16(p):
        return jnp.concatenate([jnp.where(lane16 == j, p, 0.0) for j in range(HEAD // S)], axis=0)

    def bd16(p):
        return jnp.where(mask16, jnp.concatenate([p.astype(BF16)] * (GROUP // S), axis=0), zero_bf)

    ld = each(pack16, lmat)
    lo = each(lambda x: jnp.where(blk16, 0.0, x), lmat)
    p1 = each(lambda x: eye_p - x, ld)
    l2 = each(lambda x: _mm(x, bd16(x), NN), ld)
    t = each(lambda p, x: _mm(rows(p, x), bd16(x), NN), p1, l2)
    p2 = each(lambda p, x: p + x[:S], p1, t)
    l4 = each(lambda x: x[S:], t)
    t = each(lambda p, x: _mm(rows(p, x), bd16(x), NN), p2, l4)
    p3 = each(lambda p, x: p + x[:S], p2, t)
    l8 = each(lambda x: x[S:], t)
    dinv = each(lambda p, x: unpack16(p + _mm(p, bd16(x), NN)), p3, l8)
    mm = each(lambda d_, x: _mm(d_, bd(x), NN), dinv, lo)
    mm2 = each(lambda x: _mm(x, bd(x), NN), mm)
    g = each(lambda x, x2: eye - x + x2 - _mm(x, bd(x2), NN), mm, mm2)
    tinv = each(lambda g_, d_: _mm(g_, bd(d_), NN), g, dinv)

    t = each(lambda x, y, v_: _mm(rows(x, y), bd(v_), NN), aak, ark, v)
    akv = each(lambda x: x[:C], t)
    arkv = each(lambda x: x[C:], t)
    t = each(lambda t_, x, y: _mm(t_, bd2(x, y, 1), NN), tinv, at, akv)
    wm = each(lambda x: x[:, :GROUP], t)
    um = each(lambda x: x[:, GROUP:], t)
    t = each(lambda a_, x, y: _mm(a_, bd2(x, y, 1), NN), arb, wm, um)
    qh = each(lambda r_, x: r_.astype(F32) - x[:, :GROUP], rt, t)
    yl = each(lambda y_, x: y_ - x[:, GROUP:], arkv, t)
    t = each(lambda q_, w_, s_: _mm(rows(q_, w_), bd(s_), NT), qh, wm, s0)
    y = each(lambda x, y_: x[:C] + y_, t, yl)
    ds = each(lambda v_, u_, x, k_, b_: diag_blocks(_mm(rows(v_, u_, x[C:]), rows(k_, -b_, -b_), TN)),
              v, um, t, kh, bh)
    s_new = each(lambda s_, p_, d_: s_ * p_ + d_, s0, p_c, ds)
    return list(zip(y, s_new))


def _scan_kernel(vf_ref, vb_ref, opf_ref, clf_ref, opb_ref, clb_ref, yf_ref, yb_ref, s_ref,
                 *, nctx, n_chunks):
    s = pl.program_id(1)

    @pl.when(s == 0)
    def _():
        s_ref[...] = jnp.zeros_like(s_ref)

    back = jnp.where(s < nctx, nctx - 1 - s, nctx + n_chunks - 1 - s)
    bdmask = (lax.broadcasted_iota(jnp.int32, (GROUP, GROUP), 0) >> 6) == \
             (lax.broadcasted_iota(jnp.int32, (GROUP, GROUP), 1) >> 6)
    dirs = ((False, s, vf_ref, opf_ref, clf_ref, yf_ref), (True, back, vb_ref, opb_ref, clb_ref, yb_ref))
    chains, sinks = [], []
    for bi in range(s_ref.shape[0]):
        for d, (rev, chunk, v_ref, op_ref, cl_ref, y_ref) in enumerate(dirs):
            p_c = jnp.exp(cl_ref[0, bi, 0, pl.ds(chunk % CPT, 1), :])
            for q in range(D_RWKV // GROUP):
                sl = slice(q * GROUP, (q + 1) * GROUP)
                at, rt, bt, kt = (op_ref[0, bi, :, j * D_RWKV + q * GROUP:j * D_RWKV + (q + 1) * GROUP]
                                  for j in range(4))
                chains.append(dict(rev=rev, at=at, rt=rt, bt=bt, kt=kt,
                                   v=v_ref[bi, :, sl], p_c=p_c[:, sl], s0=s_ref[bi, d, :, sl]))
                sinks.append((y_ref, bi, d, sl))
    for (y, s_new), (y_ref, bi, d, sl) in zip(_scan_chunks(chains, bdmask), sinks):
        y_ref[bi, :, sl] = y
        s_ref[bi, d, :, sl] = s_new


def _scan(v, ops, cl, ctx_len):
    B, T, W = v.shape
    C = CHUNK
    NC = T // C
    nctx = ctx_len // C

    def back(s):
        return jnp.where(s < nctx, nctx - 1 - s, nctx + NC - 1 - s)

    BB = 4 if B % 4 == 0 else 2 if B % 2 == 0 else 1
    fwd = pl.BlockSpec((BB, C, W), lambda bb, s: (bb, s, 0))
    bwd = pl.BlockSpec((BB, C, W), lambda bb, s: (bb, back(s), 0))
    fwd_d = pl.BlockSpec((1, BB, C, 4 * W), lambda bb, s: (0, bb, s, 0))
    bwd_d = pl.BlockSpec((1, BB, C, 4 * W), lambda bb, s: (1, bb, back(s), 0))
    fwd_c = pl.BlockSpec((1, BB, 1, 8, W), lambda bb, s: (0, bb, s // CPT, 0, 0))
    bwd_c = pl.BlockSpec((1, BB, 1, 8, W), lambda bb, s: (1, bb, back(s) // CPT, 0, 0))
    out = jax.ShapeDtypeStruct((B, T, W), F32)
    return pl.pallas_call(
        functools.partial(_scan_kernel, nctx=nctx, n_chunks=NC),
        grid=(B // BB, NC),
        in_specs=[fwd, bwd, fwd_d, fwd_c, bwd_d, bwd_c],
        out_specs=[fwd, bwd],
        out_shape=[out, out],
        scratch_shapes=[pltpu.VMEM((BB, 2, HEAD, W), F32)],
        compiler_params=pltpu.CompilerParams(
            dimension_semantics=("parallel", "arbitrary"), vmem_limit_bytes=VMEM_LIMIT),
        name="scan",
    )(v, v, ops, cl, ops, cl)


def _mix_kernel(za_ref, zb_ref, m_ref, cb_ref, u_ref, up_ref, un_ref, yf_ref, yb_ref, prod_ref, v_ref,
                gl_ref, cw_ref, lng_ref, lnb_ref, ones_ref, gup_ref, wout_ref, x1_ref, *, n_tiles, split):
    i = pl.program_id(1)
    for bi in range(x1_ref.shape[0]):
        m = m_ref[bi, 0]
        prev_ok = (i >= 2).astype(F32)
        next_ok = jnp.logical_and(i >= 1, i <= n_tiles - 2).astype(F32)
        u = u_ref[bi]
        rowi = lax.broadcasted_iota(jnp.int32, (TM, 1), 0)
        um1 = jnp.where(rowi == 0, up_ref[bi][7:8] * prev_ok, pltpu.roll(u, 1, 0))
        up1 = jnp.where(rowi == TM - 1, un_ref[bi][0:1] * next_ok, pltpu.roll(u, TM - 1, 0))
        cw = cw_ref[...]
        yconv = cb_ref[bi] * (cw[0:1] * um1 + cw[1:2] * u + cw[2:3] * up1)

        ones_bd = ones_ref[...]
        y = yf_ref[bi] + yb_ref[bi]
        mu = _dot01(y, ones_bd) * (1.0 / HEAD)
        dlt = y - mu
        var = _dot01(dlt * dlt, ones_bd) * (1.0 / HEAD)
        yn = dlt * lax.rsqrt(var + GN_EPS) * lng_ref[...] + lnb_ref[...]
        bonus = _dot01(prod_ref[bi], ones_bd) * v_ref[bi].astype(F32)
        gate = jnp.dot(jax.nn.sigmoid(gl_ref[bi]).astype(BF16), gup_ref[...], preferred_element_type=F32)
        yrw = (yn + bonus) * gate

        out = (jnp.dot(yconv.astype(BF16), wout_ref[0:D_CONV, :], preferred_element_type=F32)
               + jnp.dot(yrw.astype(BF16), wout_ref[D_CONV:D_CONV + D_RWKV, :], preferred_element_type=F32))
        x1_ref[bi] = _z_tile(za_ref, zb_ref, split, bi) + m[2:3] * out


def _mix(za, zb, split, modsel, cb, u, yf, yb, prod, v, gl, cw, lng, lnb, ones_bd, gup, wout):
    B, T, D = cb.shape[0], cb.shape[1], za.shape[2]
    NTL = T // TM
    bt = _rows_per_step(B)
    tok = pl.BlockSpec((bt, TM, D_RWKV), lambda b, i: (b, i, 0))
    const = lambda shape: pl.BlockSpec(shape, lambda b, i: (0,) * len(shape))
    R8 = TM // 8
    return pl.pallas_call(
        functools.partial(_mix_kernel, n_tiles=NTL, split=split),
        grid=(B // bt, NTL),
        in_specs=_z_specs(split, bt, D) + [
            pl.BlockSpec((bt, 1, 8, D), lambda b, i: (b, jnp.minimum(i, 1), 0, 0)),
            tok, tok,
            pl.BlockSpec((bt, 8, D_CONV), lambda b, i: (b, jnp.maximum(i * R8 - 1, 0), 0)),
            pl.BlockSpec((bt, 8, D_CONV), lambda b, i: (b, jnp.minimum((i + 1) * R8, T // 8 - 1), 0)),
            tok, tok, tok, tok, pl.BlockSpec((bt, TM, G_LORA), lambda b, i: (b, i, 0)),
            const((8, D_CONV)), const((1, D_RWKV)), const((1, D_RWKV)), const((D_RWKV, D_RWKV)),
            const((G_LORA, D_RWKV)), const((D_CONV + D_RWKV, D)),
        ],
        out_specs=pl.BlockSpec((bt, TM, D), lambda b, i: (b, i, 0)),
        out_shape=jax.ShapeDtypeStruct((B, T, D), F32),
        compiler_params=pltpu.CompilerParams(
            dimension_semantics=("parallel", "parallel"), vmem_limit_bytes=VMEM_LIMIT),
        name="mix",
    )(za, zb, modsel, cb, u, u, u, yf, yb, prod, v, gl, cw, lng, lnb, ones_bd, gup, wout)


def _ffn_kernel(x1_ref, xb_ref, m_ref, g2_ref, wup_ref, cw_ref, cbias_ref, wdn_ref, fg_ref,
                o_ref, gtop_ref, *, n_tiles, d_ff, final):
    i = pl.program_id(1)
    lat = (i >= 1).astype(F32)
    top_ok = i >= 2
    bot_ok = jnp.logical_and(i >= 1, i <= n_tiles - 2).astype(F32)
    rowi = lax.broadcasted_iota(jnp.int32, (TM, 1), 0)
    colp = jnp.where(i == 0, rowi, rowi & (GRID_W - 1))
    width = jnp.where(i == 0, TM, GRID_W)
    lmask = (colp > 0).astype(F32)
    rmask = (colp < width - 1).astype(F32)
    NB = 256

    @pl.when(i == 0)
    def _():
        gtop_ref[...] = jnp.zeros_like(gtop_ref)

    for bi in range(x1_ref.shape[0]):
        m = m_ref[bi, 0]

        def norm2(x, keep):
            h = _rms(x) * g2_ref[...]
            return ((h * (1.0 + m[4:5]) + m[3:4]) * keep).astype(BF16)

        x1 = x1_ref[bi]
        hx = jnp.concatenate([norm2(x1, 1.0), norm2(xb_ref[bi], bot_ok)], axis=0)
        hc = hx[0:TM]

        def up(n):
            return (jnp.dot(hx, wup_ref[:, n:n + NB], preferred_element_type=F32),
                    jnp.dot(hc, wup_ref[:, d_ff + n:d_ff + n + NB], preferred_element_type=F32))

        acc = jnp.zeros((TM, o_ref.shape[-1]), F32)
        nxt = up(0)
        prev = None
        for n in range(0, d_ff, NB):
            gfull, val = nxt
            if n + NB < d_ff:
                nxt = up(n + NB)
            if prev is not None:
                acc = acc + jnp.dot(prev, wdn_ref[n - NB:n, :], preferred_element_type=F32)
            g, gd = gfull[0:TM], gfull[GRID_W:GRID_W + TM]
            top = jnp.where(top_ok, gtop_ref[bi, :, n:n + NB], 0.0)
            gu = jnp.concatenate([top, g[0:TM - GRID_W]], axis=0)
            gtop_ref[bi, :, n:n + NB] = g[TM - GRID_W:TM]
            w = cw_ref[:, n:n + NB]
            wv = w * lat
            h0 = wv[0:1] * gu + w[3:4] * g + wv[6:7] * gd
            h1 = wv[1:2] * gu + w[4:5] * g + wv[7:8] * gd
            h2 = wv[2:3] * gu + w[5:6] * g + wv[8:9] * gd
            conv = (lmask * pltpu.roll(h0, 1, 0) + h1 + rmask * pltpu.roll(h2, TM - 1, 0)
                    + cbias_ref[:, n:n + NB])
            prev = (_silu(conv) * val).astype(BF16)
        acc = acc + jnp.dot(prev, wdn_ref[d_ff - NB:d_ff, :], preferred_element_type=F32)
        x2 = x1 + m[5:6] * acc
        if final:
            x2 = _rms(x2) * fg_ref[...]
        o_ref[bi] = x2


def _ffn(x1, modsel, g2, wup, cw, cbias, wdn, fg, final):
    B, T, D = x1.shape
    NTL = T // TM
    d_ff = wdn.shape[0]
    RW = TM // GRID_W
    bt = 1
    const = lambda shape: pl.BlockSpec(shape, lambda b, i: (0,) * len(shape))
    if final:
        out_spec = pl.BlockSpec((bt, TM, D), lambda b, i: (b, jnp.maximum(i - 1, 0), 0))
        out_shape = jax.ShapeDtypeStruct((B, T - TM, D), F32)
    else:
        out_spec = pl.BlockSpec((bt, TM, D), lambda b, i: (b, i, 0))
        out_shape = jax.ShapeDtypeStruct((B, T, D), F32)
    return pl.pallas_call(
        functools.partial(_ffn_kernel, n_tiles=NTL, d_ff=d_ff, final=final),
        grid=(B // bt, NTL),
        in_specs=[
            pl.BlockSpec((bt, TM, D), lambda b, i: (b, i, 0)),
            pl.BlockSpec((bt, GRID_W, D),
                         lambda b, i: (b, jnp.minimum((i + 1) * RW, T // GRID_W - 1), 0)),
            pl.BlockSpec((bt, 1, 8, D), lambda b, i: (b, jnp.minimum(i, 1), 0, 0)),
            const((1, D)), const((D, 2 * d_ff)),
            const((16, d_ff)), const((1, d_ff)), const((d_ff, D)), const((1, D)),
        ],
        out_specs=out_spec,
        out_shape=out_shape,
        scratch_shapes=[pltpu.VMEM((bt, GRID_W, d_ff), F32)],
        compiler_params=pltpu.CompilerParams(
            dimension_semantics=("parallel", "arbitrary"), vmem_limit_bytes=VMEM_LIMIT),
        name="ffn",
    )(x1, x1, modsel, g2, wup, cw, cbias, wdn, fg)


def kernel(x, c, ctx, c_ctx, ada_w, ada_b, norm1_g, norm2_g, w_in, conv_a_w, rw_w0, rw_w_up, rw_a0,
           rw_a_up, rw_k_k, rw_k_a, rw_r_k, rw_g_up, rw_ln_g, rw_ln_b, w_out, ffn_w_up, ffn_conv_w,
           ffn_conv_b, ffn_w_down, final_g):
    B, SEQ, D = x.shape
    CTX = ctx.shape[1]
    L = w_in.shape[0]
    d_ff = ffn_w_down.shape[1]
    assert CTX == TM and SEQ % TM == 0 and TM % GRID_W == 0 and CHUNK == HEAD and CPT <= 8
    assert w_in.shape[2] == 3 * D_CONV + 3 * D_RWKV + LORA_WA + G_LORA and d_ff % 256 == 0

    rows = -(-(B + 1) // 8) * 8
    c_rows = jnp.zeros((rows, D), F32).at[:B].set(c).at[B].set(c_ctx)
    mod = _ada(c_rows, ada_w, ada_b)

    hw = LORA_WA // 2
    zpad = jnp.zeros((L, hw, 2 * D_RWKV), F32)
    both = lambda t: jnp.concatenate([t[:, 0], t[:, 1]], axis=-1)
    wup_ext = jnp.concatenate([both(rw_w_up), zpad], axis=1).astype(BF16)
    aup_ext = jnp.concatenate([zpad, both(rw_a_up)], axis=1).astype(BF16)
    w0_2 = both(rw_w0[:, :, None, :])
    a0_2 = both(rw_a0[:, :, None, :])
    head_of = jnp.arange(D_RWKV) // HEAD
    ones_bd = (head_of[:, None] == head_of[None, :]).astype(BF16)
    tt = jnp.arange(TM)
    same = (tt[:, None] // CHUNK) == (tt[None, :] // CHUNK)
    tri = jnp.stack([same & (tt[None, :] <= tt[:, None]),
                     same & (tt[None, :] >= tt[:, None])]).astype(BF16)
    cw_a = jnp.zeros((L, 8, D_CONV), F32).at[:, :3].set(conv_a_w)
    cw_f = jnp.zeros((L, 16, d_ff), F32).at[:, :9].set(ffn_conv_w.reshape(L, 9, d_ff))

    za, zb = ctx, x
    for l in range(L):
        split = l == 0
        lat = mod[l, :B].reshape(B, 6, D)
        cm = jnp.broadcast_to(mod[l, B].reshape(1, 6, D), (B, 6, D))
        modsel = jnp.zeros((B, 2, 8, D), F32).at[:, 0, :6].set(cm).at[:, 1, :6].set(lat)

        cb, u, v, prod, gl, ops, cl = _proj(
            za, zb, split, modsel, norm1_g[l].reshape(1, D), w_in[l].astype(BF16), wup_ext[l], aup_ext[l],
            w0_2[l], a0_2[l], rw_k_k[l].reshape(1, -1), rw_k_a[l].reshape(1, -1),
            rw_r_k[l].reshape(1, -1), ones_bd, tri)
        yf, yb = _scan(v, ops, cl, CTX)
        x1 = _mix(
            za, zb, split, modsel, cb, u, yf, yb, prod, v, gl, cw_a[l], rw_ln_g[l].reshape(1, -1),
            rw_ln_b[l].reshape(1, -1), ones_bd, rw_g_up[l].astype(BF16), w_out[l].astype(BF16))
        za = zb = _ffn(x1, modsel, norm2_g[l].reshape(1, D), ffn_w_up[l].astype(BF16), cw_f[l],
                       ffn_conv_b[l].reshape(1, -1), ffn_w_down[l].astype(BF16), final_g.reshape(1, D),
                       final=(l == L - 1))
    return za
```

```python
import functools
import math

import jax
import jax.numpy as jnp
from jax import lax
from jax.experimental import pallas as pl
from jax.experimental.pallas import tpu as pltpu

F32 = jnp.float32
BF16 = jnp.bfloat16

HEAD = 64
D_CONV = 512
D_RWKV = 512
LORA_WA = 128
G_LORA = 128
GRID_W = 64
RMS_EPS = 1e-6
GN_EPS = 64e-5
DECAY_SCALE = math.exp(-0.5)

TM = 256
CHUNK = 64
CPT = TM // CHUNK
BT = 2
UP_AHEAD = 4
HPG = 2
GROUP = HPG * HEAD
VMEM_LIMIT = 56 * 1024 * 1024

NN = (((1,), (0,)), ((), ()))
NT = (((1,), (1,)), ((), ()))
TN = (((0,), (0,)), ((), ()))


def _silu(x):
    return x * jax.nn.sigmoid(x)


def _split(x):
    hi = x.astype(BF16)
    return hi, (x - hi.astype(F32)).astype(BF16)


def _dg(a, b, dims):
    return lax.dot_general(a, b, dims, preferred_element_type=F32)


def _mm3(a, b_hi, b_lo, dims):
    a_hi, a_lo = _split(a)
    return _dg(a_hi, b_hi, dims) + (_dg(a_hi, b_lo, dims) + _dg(a_lo, b_hi, dims))


def _rms(x):
    return x * lax.rsqrt(jnp.mean(x * x, axis=-1, keepdims=True) + RMS_EPS)


def _dot01(t, m01, left=False):
    hi, lo = _split(t)
    if left:
        return _dg(m01, hi, NN) + _dg(m01, lo, NN)
    return _dg(hi, m01, NN) + _dg(lo, m01, NN)


def _z_tile(za_ref, zb_ref, split, bi):
    if not split:
        return za_ref[bi]
    return jnp.where(pl.program_id(1) == 0, za_ref[bi], zb_ref[bi])


def _z_specs(split, bt, D):
    if split:
        return [pl.BlockSpec((bt, TM, D), lambda b, i: (b, 0, 0)),
                pl.BlockSpec((bt, TM, D), lambda b, i: (b, jnp.maximum(i - 1, 0), 0))]
    return [pl.BlockSpec((bt, TM, D), lambda b, i: (b, i, 0)),
            pl.BlockSpec((1, 8, D), lambda b, i: (0, 0, 0))]


def _rows_per_step(B):
    return BT if B % BT == 0 else 1


def _ada_kernel(c_ref, w_ref, b_ref, o_ref):
    s = _silu(c_ref[...])
    w_hi, w_lo = _split(w_ref[0])
    o_ref[0] = _mm3(s, w_hi, w_lo, NN) + b_ref[0]


def _ada(c_rows, ada_w, ada_b):
    L, D, N = ada_w.shape
    R = c_rows.shape[0]
    NB = 1536
    return pl.pallas_call(
        _ada_kernel,
        grid=(L, N // NB),
        in_specs=[
            pl.BlockSpec((R, D), lambda l, n: (0, 0)),
            pl.BlockSpec((1, D, NB), lambda l, n: (l, 0, n)),
            pl.BlockSpec((1, 1, NB), lambda l, n: (l, 0, n)),
        ],
        out_specs=pl.BlockSpec((1, R, NB), lambda l, n: (l, 0, n)),
        out_shape=jax.ShapeDtypeStruct((L, R, N), F32),
        compiler_params=pltpu.CompilerParams(
            dimension_semantics=("arbitrary", "arbitrary"), vmem_limit_bytes=VMEM_LIMIT),
        name="ada",
    )(c_rows, ada_w, ada_b.reshape(L, 1, N))


def _proj_kernel(za_ref, zb_ref, m_ref, g_ref, win_ref, wup_ref, aup_ref, w0_ref, a0_ref, kk_ref, ka_ref,
                 rk_ref, ones_ref, tri_ref,
                 cb_ref, u_ref, v_ref, prod_ref, gl_ref, ops_ref, cl_ref,
                 *, split):
    for bi in range(cb_ref.shape[0]):
        m = m_ref[bi, 0]
        h = _rms(_z_tile(za_ref, zb_ref, split, bi)) * g_ref[...]
        h = (h * (1.0 + m[1:2]) + m[0:1]).astype(BF16)

        def proj(lo, width):
            return jnp.dot(h, win_ref[:, lo:lo + width], preferred_element_type=F32)

        base = 3 * D_CONV
        lora = proj(base + 3 * D_RWKV, LORA_WA + G_LORA)
        k = proj(base + D_RWKV, D_RWKV)
        r = proj(base, D_RWKV)
        v = proj(base + 2 * D_RWKV, D_RWKV)
        wa = lora[:, :LORA_WA]
        gl = lora[:, LORA_WA:]
        ones_bd = ones_ref[...]

        kraw = k * kk_ref[...]
        kkn = kraw * lax.rsqrt(jnp.maximum(_dot01(kraw * kraw, ones_bd), 1e-24))
        lw2 = -DECAY_SCALE * jax.nn.sigmoid(
            w0_ref[...] + jnp.dot(jnp.tanh(wa).astype(BF16), wup_ref[...], preferred_element_type=F32))
        a2 = jax.nn.sigmoid(
            a0_ref[...] + jnp.dot(wa.astype(BF16), aup_ref[...], preferred_element_type=F32))

        cb_ref[bi] = proj(0, D_CONV)
        v_ref[bi] = v.astype(BF16)
        prod_ref[bi] = r * k * rk_ref[...]
        gl_ref[bi] = gl

        zero_rows = jnp.zeros((8 - CPT, D_RWKV), F32)
        conv_in = []
        for d in range(2):
            conv_in.append(proj((1 + d) * D_CONV, D_CONV))
            lw = lw2[:, d * D_RWKV:(d + 1) * D_RWKV]
            a = a2[:, d * D_RWKV:(d + 1) * D_RWKV]
            c = _dot01(lw, tri_ref[d], left=True)
            e_neg = jnp.exp(-c)
            ops = (kkn * jnp.exp(c - lw), r * jnp.exp(c), kkn * a * e_neg,
                   k * (1.0 + (a - 1.0) * ka_ref[...]) * e_neg)
            for j, x in enumerate(ops):
                ops_ref[d, bi, :, j * D_RWKV:(j + 1) * D_RWKV] = x.astype(BF16)
            last = CHUNK - 1 if d == 0 else 0
            cl_ref[d, bi, 0] = jnp.concatenate(
                [c[j * CHUNK + last:j * CHUNK + last + 1] for j in range(CPT)] + [zero_rows], axis=0)
        u_ref[bi] = conv_in[0] * conv_in[1]


def _proj(za, zb, split, modsel, g1, win, wup, aup, w0, a0, k_k, k_a, r_k, ones_bd, tri):
    B, D = za.shape[0], za.shape[2]
    T = za.shape[1] + zb.shape[1] if split else za.shape[1]
    NTL = T // TM
    P = win.shape[1]
    W = D_RWKV
    bt = _rows_per_step(B)
    tok = pl.BlockSpec((bt, TM, W), lambda b, i: (b, i, 0))
    tok2 = pl.BlockSpec((2, bt, TM, 4 * W), lambda b, i: (0, b, i, 0))
    const = lambda shape: pl.BlockSpec(shape, lambda b, i: (0,) * len(shape))
    f32_tok = jax.ShapeDtypeStruct((B, T, W), F32)
    bf_tok2 = jax.ShapeDtypeStruct((2, B, T, 4 * W), BF16)
    return pl.pallas_call(
        functools.partial(_proj_kernel, split=split),
        grid=(B // bt, NTL),
        in_specs=_z_specs(split, bt, D) + [
            pl.BlockSpec((bt, 1, 8, D), lambda b, i: (b, jnp.minimum(i, 1), 0, 0)),
            const((1, D)), const((D, P)), const((LORA_WA, 2 * W)), const((LORA_WA, 2 * W)),
            const((1, 2 * W)), const((1, 2 * W)), const((1, W)), const((1, W)),
            const((1, W)), const((W, W)), const((2, TM, TM)),
        ],
        out_specs=[tok, tok, tok, tok, pl.BlockSpec((bt, TM, G_LORA), lambda b, i: (b, i, 0)),
                   tok2,
                   pl.BlockSpec((2, bt, 1, 8, W), lambda b, i: (0, b, i, 0, 0))],
        out_shape=[f32_tok, f32_tok, jax.ShapeDtypeStruct((B, T, W), BF16), f32_tok,
                   jax.ShapeDtypeStruct((B, T, G_LORA), F32),
                   bf_tok2,
                   jax.ShapeDtypeStruct((2, B, NTL, 8, W), F32)],
        compiler_params=pltpu.CompilerParams(
            dimension_semantics=("parallel", "parallel"), vmem_limit_bytes=VMEM_LIMIT),
        name="proj",
    )(za, zb, modsel, g1, win, wup, aup, w0, a0, k_k, k_a, r_k, ones_bd, tri)


def _mm(a, x, dims):
    return _dg(a.astype(BF16), x.astype(BF16), dims)


def _scan_chunks(chains, bdmask):
    C = CHUNK
    row = lax.broadcasted_iota(jnp.int32, (C, GROUP), 0)
    col = lax.broadcasted_iota(jnp.int32, (C, GROUP), 1) & (HEAD - 1)
    eye = (row == col).astype(F32)
    blk16 = (row >> 4) == (col >> 4)
    lane_head = lax.broadcasted_iota(jnp.int32, (HEAD, GROUP), 1) >> 6
    zero_bf = jnp.zeros((), BF16)

    def each(f, *lists):
        return [f(*a) for a in zip(*lists)]

    def bd(x):
        return jnp.where(bdmask, jnp.concatenate([x.astype(BF16)] * HPG, axis=0), zero_bf)

    def bd2(x, y, axis):
        return jnp.concatenate([bd(x), bd(y)], axis=axis)

    def rows(*xs):
        return jnp.concatenate([x.astype(BF16) for x in xs], axis=0)

    def diag_blocks(full):
        out = jnp.where(lane_head == 0, full[0:HEAD], 0.0)
        for hh in range(1, HPG):
            out = out + jnp.where(lane_head == hh, full[hh * HEAD:(hh + 1) * HEAD], 0.0)
        return out

    rev = [ch["rev"] for ch in chains]
    at, rt, bt, kt, v, p_c, s0 = ([ch[n] for ch in chains]
                                  for n in ("at", "rt", "bt", "kt", "v", "p_c", "s0"))
    strict = [(col > row) if rv else (col < row) for rv in rev]
    incl = [(col >= row) if rv else (col <= row) for rv in rev]
    bh = each(lambda x, p: x * p, bt, p_c)
    kh = each(lambda x, p: x * p, kt, p_c)

    abk = each(lambda a_, r_, x, y: _mm(rows(a_, r_), bd2(x, y, 0), NT), at, rt, bt, kt)
    lmat = each(lambda m_, x: jnp.where(m_, x[:C, :GROUP], 0.0), strict, abk)
    aak = each(lambda m_, x: jnp.where(m_, x[:C, GROUP:], 0.0), strict, abk)
    arb = each(lambda m_, x: jnp.where(m_, x[C:, :GROUP], 0.0), incl, abk)
    ark = each(lambda m_, x: jnp.where(m_, x[C:, GROUP:], 0.0), incl, abk)

    S = 16
    lane16 = (lax.broadcasted_iota(jnp.int32, (S, GROUP), 1) & (HEAD - 1)) >> 4
    eye_p = (lax.broadcasted_iota(jnp.int32, (S, GROUP), 0)
             == (lax.broadcasted_iota(jnp.int32, (S, GROUP), 1) & (S - 1))).astype(F32)
    mask16 = (lax.broadcasted_iota(jnp.int32, (GROUP, GROUP), 0) >> 4) == \
             (lax.broadcasted_iota(jnp.int32, (GROUP, GROUP), 1) >> 4)

    def pack16(full):
        out = jnp.where(lane16 == 0, full[0:S], 0.0)
        for j in range(1, HEAD // S):
            out = out + jnp.where(lane16 == j, full[j * S:(j + 1) * S], 0.0)
        return out

    def unpack16(p):
        return jnp.concatenate([jnp.where(lane16 == j, p, 0.0) for j in range(HEAD // S)], axis=0)

    def bd16(p):
        return jnp.where(mask16, jnp.concatenate([p.astype(BF16)] * (GROUP // S), axis=0), zero_bf)

    ld = each(pack16, lmat)
    lo = each(lambda x: jnp.where(blk16, 0.0, x), lmat)
    p1 = each(lambda x: eye_p - x, ld)
    l2 = each(lambda x: _mm(x, bd16(x), NN), ld)
    t = each(lambda p, x: _mm(rows(p, x), bd16(x), NN), p1, l2)
    p2 = each(lambda p, x: p + x[:S], p1, t)
    l4 = each(lambda x: x[S:], t)
    t = each(lambda p, x: _mm(rows(p, x), bd16(x), NN), p2, l4)
    p3 = each(lambda p, x: p + x[:S], p2, t)
    l8 = each(lambda x: x[S:], t)
    dinv = each(lambda p, x: unpack16(p + _mm(p, bd16(x), NN)), p3, l8)
    mm = each(lambda d_, x: _mm(d_, bd(x), NN), dinv, lo)
    mm2 = each(lambda x: _mm(x, bd(x), NN), mm)
    g = each(lambda x, x2: eye - x + x2 - _mm(x, bd(x2), NN), mm, mm2)
    tinv = each(lambda g_, d_: _mm(g_, bd(d_), NN), g, dinv)

    t = each(lambda x, y, v_: _mm(rows(x, y), bd(v_), NN), aak, ark, v)
    akv = each(lambda x: x[:C], t)
    arkv = each(lambda x: x[C:], t)
    t = each(lambda t_, x, y: _mm(t_, bd2(x, y, 1), NN), tinv, at, akv)
    wm = each(lambda x: x[:, :GROUP], t)
    um = each(lambda x: x[:, GROUP:], t)
    t = each(lambda a_, x, y: _mm(a_, bd2(x, y, 1), NN), arb, wm, um)
    qh = each(lambda r_, x: r_.astype(F32) - x[:, :GROUP], rt, t)
    yl = each(lambda y_, x: y_ - x[:, GROUP:], arkv, t)
    t = each(lambda q_, w_, s_: _mm(rows(q_, w_), bd(s_), NT), qh, wm, s0)
    y = each(lambda x, y_: x[:C] + y_, t, yl)
    ds = each(lambda v_, u_, x, k_, b_: diag_blocks(_mm(rows(v_, u_, x[C:]), rows(k_, -b_, -b_), TN)),
              v, um, t, kh, bh)
    s_new = each(lambda s_, p_, d_: s_ * p_ + d_, s0, p_c, ds)
    return list(zip(y, s_new))


def _scan_kernel(vf_ref, vb_ref, opf_ref, clf_ref, opb_ref, clb_ref, yf_ref, yb_ref, s_ref,
                 *, nctx, n_chunks):
    s = pl.program_id(1)

    @pl.when(s == 0)
    def _():
        s_ref[...] = jnp.zeros_like(s_ref)

    back = jnp.where(s < nctx, nctx - 1 - s, nctx + n_chunks - 1 - s)
    bdmask = (lax.broadcasted_iota(jnp.int32, (GROUP, GROUP), 0) >> 6) == \
             (lax.broadcasted_iota(jnp.int32, (GROUP, GROUP), 1) >> 6)
    dirs = ((False, s, vf_ref, opf_ref, clf_ref, yf_ref), (True, back, vb_ref, opb_ref, clb_ref, yb_ref))
    chains, sinks = [], []
    for bi in range(s_ref.shape[0]):
        for d, (rev, chunk, v_ref, op_ref, cl_ref, y_ref) in enumerate(dirs):
            p_c = jnp.exp(cl_ref[0, bi, 0, pl.ds(chunk % CPT, 1), :])
            for q in range(D_RWKV // GROUP):
                sl = slice(q * GROUP, (q + 1) * GROUP)
                at, rt, bt, kt = (op_ref[0, bi, :, j * D_RWKV + q * GROUP:j * D_RWKV + (q + 1) * GROUP]
                                  for j in range(4))
                chains.append(dict(rev=rev, at=at, rt=rt, bt=bt, kt=kt,
                                   v=v_ref[bi, :, sl], p_c=p_c[:, sl], s0=s_ref[bi, d, :, sl]))
                sinks.append((y_ref, bi, d, sl))
    for (y, s_new), (y_ref, bi, d, sl) in zip(_scan_chunks(chains, bdmask), sinks):
        y_ref[bi, :, sl] = y
        s_ref[bi, d, :, sl] = s_new


def _scan(v, ops, cl, ctx_len):
    B, T, W = v.shape
    C = CHUNK
    NC = T // C
    nctx = ctx_len // C

    def back(s):
        return jnp.where(s < nctx, nctx - 1 - s, nctx + NC - 1 - s)

    BB = 4 if B % 4 == 0 else 2 if B % 2 == 0 else 1
    fwd = pl.BlockSpec((BB, C, W), lambda bb, s: (bb, s, 0))
    bwd = pl.BlockSpec((BB, C, W), lambda bb, s: (bb, back(s), 0))
    fwd_d = pl.BlockSpec((1, BB, C, 4 * W), lambda bb, s: (0, bb, s, 0))
    bwd_d = pl.BlockSpec((1, BB, C, 4 * W), lambda bb, s: (1, bb, back(s), 0))
    fwd_c = pl.BlockSpec((1, BB, 1, 8, W), lambda bb, s: (0, bb, s // CPT, 0, 0))
    bwd_c = pl.BlockSpec((1, BB, 1, 8, W), lambda bb, s: (1, bb, back(s) // CPT, 0, 0))
    out = jax.ShapeDtypeStruct((B, T, W), F32)
    return pl.pallas_call(
        functools.partial(_scan_kernel, nctx=nctx, n_chunks=NC),
        grid=(B // BB, NC),
        in_specs=[fwd, bwd, fwd_d, fwd_c, bwd_d, bwd_c],
        out_specs=[fwd, bwd],
        out_shape=[out, out],
        scratch_shapes=[pltpu.VMEM((BB, 2, HEAD, W), F32)],
        compiler_params=pltpu.CompilerParams(
            dimension_semantics=("parallel", "arbitrary"), vmem_limit_bytes=VMEM_LIMIT),
        name="scan",
    )(v, v, ops, cl, ops, cl)


def _mix_kernel(za_ref, zb_ref, m_ref, cb_ref, u_ref, up_ref, un_ref, yf_ref, yb_ref, prod_ref, v_ref,
                gl_ref, cw_ref, lng_ref, lnb_ref, ones_ref, gup_ref, wout_ref, x1_ref, *, n_tiles, split):
    i = pl.program_id(1)
    for bi in range(x1_ref.shape[0]):
        m = m_ref[bi, 0]
        prev_ok = (i >= 2).astype(F32)
        next_ok = jnp.logical_and(i >= 1, i <= n_tiles - 2).astype(F32)
        u = u_ref[bi]
        rowi = lax.broadcasted_iota(jnp.int32, (TM, 1), 0)
        um1 = jnp.where(rowi == 0, up_ref[bi][7:8] * prev_ok, pltpu.roll(u, 1, 0))
        up1 = jnp.where(rowi == TM - 1, un_ref[bi][0:1] * next_ok, pltpu.roll(u, TM - 1, 0))
        cw = cw_ref[...]
        yconv = cb_ref[bi] * (cw[0:1] * um1 + cw[1:2] * u + cw[2:3] * up1)

        ones_bd = ones_ref[...]
        y = yf_ref[bi] + yb_ref[bi]
        mu = _dot01(y, ones_bd) * (1.0 / HEAD)
        dlt = y - mu
        var = _dot01(dlt * dlt, ones_bd) * (1.0 / HEAD)
        yn = dlt * lax.rsqrt(var + GN_EPS) * lng_ref[...] + lnb_ref[...]
        bonus = _dot01(prod_ref[bi], ones_bd) * v_ref[bi].astype(F32)
        gate = jnp.dot(jax.nn.sigmoid(gl_ref[bi]).astype(BF16), gup_ref[...], preferred_element_type=F32)
        yrw = (yn + bonus) * gate

        out = (jnp.dot(yconv.astype(BF16), wout_ref[0:D_CONV, :], preferred_element_type=F32)
               + jnp.dot(yrw.astype(BF16), wout_ref[D_CONV:D_CONV + D_RWKV, :], preferred_element_type=F32))
        x1_ref[bi] = _z_tile(za_ref, zb_ref, split, bi) + m[2:3] * out


def _mix(za, zb, split, modsel, cb, u, yf, yb, prod, v, gl, cw, lng, lnb, ones_bd, gup, wout):
    B, T, D = cb.shape[0], cb.shape[1], za.shape[2]
    NTL = T // TM
    bt = _rows_per_step(B)
    tok = pl.BlockSpec((bt, TM, D_RWKV), lambda b, i: (b, i, 0))
    const = lambda shape: pl.BlockSpec(shape, lambda b, i: (0,) * len(shape))
    R8 = TM // 8
    return pl.pallas_call(
        functools.partial(_mix_kernel, n_tiles=NTL, split=split),
        grid=(B // bt, NTL),
        in_specs=_z_specs(split, bt, D) + [
            pl.BlockSpec((bt, 1, 8, D), lambda b, i: (b, jnp.minimum(i, 1), 0, 0)),
            tok, tok,
            pl.BlockSpec((bt, 8, D_CONV), lambda b, i: (b, jnp.maximum(i * R8 - 1, 0), 0)),
            pl.BlockSpec((bt, 8, D_CONV), lambda b, i: (b, jnp.minimum((i + 1) * R8, T // 8 - 1), 0)),
            tok, tok, tok, tok, pl.BlockSpec((bt, TM, G_LORA), lambda b, i: (b, i, 0)),
            const((8, D_CONV)), const((1, D_RWKV)), const((1, D_RWKV)), const((D_RWKV, D_RWKV)),
            const((G_LORA, D_RWKV)), const((D_CONV + D_RWKV, D)),
        ],
        out_specs=pl.BlockSpec((bt, TM, D), lambda b, i: (b, i, 0)),
        out_shape=jax.ShapeDtypeStruct((B, T, D), F32),
        compiler_params=pltpu.CompilerParams(
            dimension_semantics=("parallel", "parallel"), vmem_limit_bytes=VMEM_LIMIT),
        name="mix",
    )(za, zb, modsel, cb, u, u, u, yf, yb, prod, v, gl, cw, lng, lnb, ones_bd, gup, wout)


def _ffn_kernel(x1_ref, xb_ref, m_ref, g2_ref, wup_ref, cw_ref, cbias_ref, wdn_ref, fg_ref,
                o_ref, gtop_ref, *, n_tiles, d_ff, final):
    i = pl.program_id(1)
    lat = (i >= 1).astype(F32)
    top_ok = i >= 2
    bot_ok = jnp.logical_and(i >= 1, i <= n_tiles - 2).astype(F32)
    rowi = lax.broadcasted_iota(jnp.int32, (TM, 1), 0)
    colp = jnp.where(i == 0, rowi, rowi & (GRID_W - 1))
    width = jnp.where(i == 0, TM, GRID_W)
    lmask = (colp > 0).astype(F32)
    rmask = (colp < width - 1).astype(F32)
    NB = 256

    @pl.when(i == 0)
    def _():
        gtop_ref[...] = jnp.zeros_like(gtop_ref)

    for bi in range(x1_ref.shape[0]):
        m = m_ref[bi, 0]

        def norm2(x, keep):
            h = _rms(x) * g2_ref[...]
            return ((h * (1.0 + m[4:5]) + m[3:4]) * keep).astype(BF16)

        x1 = x1_ref[bi]
        hx = jnp.concatenate([norm2(x1, 1.0), norm2(xb_ref[bi], bot_ok)], axis=0)
        hc = hx[0:TM]

        def up(n):
            return (jnp.dot(hx, wup_ref[:, n:n + NB], preferred_element_type=F32),
                    jnp.dot(hc, wup_ref[:, d_ff + n:d_ff + n + NB], preferred_element_type=F32))

        acc = jnp.zeros((TM, o_ref.shape[-1]), F32)
        ahead = [up(j * NB) for j in range(UP_AHEAD)]
        prev = None
        for n in range(0, d_ff, NB):
            gfull, val = ahead.pop(0)
            if n + UP_AHEAD * NB < d_ff:
                ahead.append(up(n + UP_AHEAD * NB))
            if prev is not None:
                acc = acc + jnp.dot(prev, wdn_ref[n - NB:n, :], preferred_element_type=F32)
            g, gd = gfull[0:TM], gfull[GRID_W:GRID_W + TM]
            top = jnp.where(top_ok, gtop_ref[bi, :, n:n + NB], 0.0)
            gu = jnp.concatenate([top, g[0:TM - GRID_W]], axis=0)
            gtop_ref[bi, :, n:n + NB] = g[TM - GRID_W:TM]
            w = cw_ref[:, n:n + NB]
            wv = w * lat
            h0 = wv[0:1] * gu + w[3:4] * g + wv[6:7] * gd
            h1 = wv[1:2] * gu + w[4:5] * g + wv[7:8] * gd
            h2 = wv[2:3] * gu + w[5:6] * g + wv[8:9] * gd
            conv = (lmask * pltpu.roll(h0, 1, 0) + h1 + rmask * pltpu.roll(h2, TM - 1, 0)
                    + cbias_ref[:, n:n + NB])
            prev = (_silu(conv) * val).astype(BF16)
        acc = acc + jnp.dot(prev, wdn_ref[d_ff - NB:d_ff, :], preferred_element_type=F32)
        x2 = x1 + m[5:6] * acc
        if final:
            x2 = _rms(x2) * fg_ref[...]
        o_ref[bi] = x2


def _ffn(x1, modsel, g2, wup, cw, cbias, wdn, fg, final):
    B, T, D = x1.shape
    NTL = T // TM
    d_ff = wdn.shape[0]
    RW = TM // GRID_W
    bt = 1
    const = lambda shape: pl.BlockSpec(shape, lambda b, i: (0,) * len(shape))
    if final:
        out_spec = pl.BlockSpec((bt, TM, D), lambda b, i: (b, jnp.maximum(i - 1, 0), 0))
        out_shape = jax.ShapeDtypeStruct((B, T - TM, D), F32)
    else:
        out_spec = pl.BlockSpec((bt, TM, D), lambda b, i: (b, i, 0))
        out_shape = jax.ShapeDtypeStruct((B, T, D), F32)
    return pl.pallas_call(
        functools.partial(_ffn_kernel, n_tiles=NTL, d_ff=d_ff, final=final),
        grid=(B // bt, NTL),
        in_specs=[
            pl.BlockSpec((bt, TM, D), lambda b, i: (b, i, 0)),
            pl.BlockSpec((bt, GRID_W, D),
                         lambda b, i: (b, jnp.minimum((i + 1) * RW, T // GRID_W - 1), 0)),
            pl.BlockSpec((bt, 1, 8, D), lambda b, i: (b, jnp.minimum(i, 1), 0, 0)),
            const((1, D)), const((D, 2 * d_ff)),
            const((16, d_ff)), const((1, d_ff)), const((d_ff, D)), const((1, D)),
        ],
        out_specs=out_spec,
        out_shape=out_shape,
        scratch_shapes=[pltpu.VMEM((bt, GRID_W, d_ff), F32)],
        compiler_params=pltpu.CompilerParams(
            dimension_semantics=("parallel", "arbitrary"), vmem_limit_bytes=VMEM_LIMIT),
        name="ffn",
    )(x1, x1, modsel, g2, wup, cw, cbias, wdn, fg)


def kernel(x, c, ctx, c_ctx, ada_w, ada_b, norm1_g, norm2_g, w_in, conv_a_w, rw_w0, rw_w_up, rw_a0,
           rw_a_up, rw_k_k, rw_k_a, rw_r_k, rw_g_up, rw_ln_g, rw_ln_b, w_out, ffn_w_up, ffn_conv_w,
           ffn_conv_b, ffn_w_down, final_g):
    B, SEQ, D = x.shape
    CTX = ctx.shape[1]
    L = w_in.shape[0]
    d_ff = ffn_w_down.shape[1]
    assert CTX == TM and SEQ % TM == 0 and TM % GRID_W == 0 and CHUNK == HEAD and CPT <= 8
    assert w_in.shape[2] == 3 * D_CONV + 3 * D_RWKV + LORA_WA + G_LORA and d_ff % 256 == 0

    rows = -(-(B + 1) // 8) * 8
    c_rows = jnp.zeros((rows, D), F32).at[:B].set(c).at[B].set(c_ctx)
    mod = _ada(c_rows, ada_w, ada_b)

    hw = LORA_WA // 2
    zpad = jnp.zeros((L, hw, 2 * D_RWKV), F32)
    both = lambda t: jnp.concatenate([t[:, 0], t[:, 1]], axis=-1)
    wup_ext = jnp.concatenate([both(rw_w_up), zpad], axis=1).astype(BF16)
    aup_ext = jnp.concatenate([zpad, both(rw_a_up)], axis=1).astype(BF16)
    w0_2 = both(rw_w0[:, :, None, :])
    a0_2 = both(rw_a0[:, :, None, :])
    head_of = jnp.arange(D_RWKV) // HEAD
    ones_bd = (head_of[:, None] == head_of[None, :]).astype(BF16)
    tt = jnp.arange(TM)
    same = (tt[:, None] // CHUNK) == (tt[None, :] // CHUNK)
    tri = jnp.stack([same & (tt[None, :] <= tt[:, None]),
                     same & (tt[None, :] >= tt[:, None])]).astype(BF16)
    cw_a = jnp.zeros((L, 8, D_CONV), F32).at[:, :3].set(conv_a_w)
    cw_f = jnp.zeros((L, 16, d_ff), F32).at[:, :9].set(ffn_conv_w.reshape(L, 9, d_ff))

    za, zb = ctx, x
    for l in range(L):
        split = l == 0
        lat = mod[l, :B].reshape(B, 6, D)
        cm = jnp.broadcast_to(mod[l, B].reshape(1, 6, D), (B, 6, D))
        modsel = jnp.zeros((B, 2, 8, D), F32).at[:, 0, :6].set(cm).at[:, 1, :6].set(lat)

        cb, u, v, prod, gl, ops, cl = _proj(
            za, zb, split, modsel, norm1_g[l].reshape(1, D), w_in[l].astype(BF16), wup_ext[l], aup_ext[l],
            w0_2[l], a0_2[l], rw_k_k[l].reshape(1, -1), rw_k_a[l].reshape(1, -1),
            rw_r_k[l].reshape(1, -1), ones_bd, tri)
        yf, yb = _scan(v, ops, cl, CTX)
        x1 = _mix(
            za, zb, split, modsel, cb, u, yf, yb, prod, v, gl, cw_a[l], rw_ln_g[l].reshape(1, -1),
            rw_ln_b[l].reshape(1, -1), ones_bd, rw_g_up[l].astype(BF16), w_out[l].astype(BF16))
        za = zb = _ffn(x1, modsel, norm2_g[l].reshape(1, D), ffn_w_up[l].astype(BF16), cw_f[l],
                       ffn_conv_b[l].reshape(1, -1), ffn_w_down[l].astype(BF16), final_g.reshape(1, D),
                       final=(l == L - 1))
    return za
```

```python
import functools
import math

import jax
import jax.numpy as jnp
from jax import lax
from jax.experimental import pallas as pl
from jax.experimental.pallas import tpu as pltpu

F32 = jnp.float32
BF16 = jnp.bfloat16

HEAD = 64
D_CONV = 512
D_RWKV = 512
LORA_WA = 128
G_LORA = 128
GRID_W = 64
RMS_EPS = 1e-6
GN_EPS = 64e-5
DECAY_SCALE = math.exp(-0.5)

TM = 256
CHUNK = 64
CPT = TM // CHUNK
BT = 2
UP_AHEAD = 4
HPG = 2
GROUP = HPG * HEAD
VMEM_LIMIT = 56 * 1024 * 1024

NN = (((1,), (0,)), ((), ()))
NT = (((1,), (1,)), ((), ()))
TN = (((0,), (0,)), ((), ()))


def _silu(x):
    return x * jax.nn.sigmoid(x)


def _split(x):
    hi = x.astype(BF16)
    return hi, (x - hi.astype(F32)).astype(BF16)


def _dg(a, b, dims):
    return lax.dot_general(a, b, dims, preferred_element_type=F32)


def _mm3(a, b_hi, b_lo, dims):
    a_hi, a_lo = _split(a)
    return _dg(a_hi, b_hi, dims) + (_dg(a_hi, b_lo, dims) + _dg(a_lo, b_hi, dims))


def _rms(x):
    return x * lax.rsqrt(jnp.mean(x * x, axis=-1, keepdims=True) + RMS_EPS)


def _dot01(t, m01, left=False):
    hi, lo = _split(t)
    if left:
        return _dg(m01, hi, NN) + _dg(m01, lo, NN)
    return _dg(hi, m01, NN) + _dg(lo, m01, NN)


def _z_tile(za_ref, zb_ref, split, bi):
    if not split:
        return za_ref[bi]
    return jnp.where(pl.program_id(1) == 0, za_ref[bi], zb_ref[bi])


def _z_specs(split, bt, D):
    if split:
        return [pl.BlockSpec((bt, TM, D), lambda b, i: (b, 0, 0)),
                pl.BlockSpec((bt, TM, D), lambda b, i: (b, jnp.maximum(i - 1, 0), 0))]
    return [pl.BlockSpec((bt, TM, D), lambda b, i: (b, i, 0)),
            pl.BlockSpec((1, 8, D), lambda b, i: (0, 0, 0))]


def _rows_per_step(B):
    return BT if B % BT == 0 else 1


def _ada_kernel(c_ref, w_ref, b_ref, o_ref):
    s = _silu(c_ref[...])
    w_hi, w_lo = _split(w_ref[0])
    o_ref[0] = _mm3(s, w_hi, w_lo, NN) + b_ref[0]


def _ada(c_rows, ada_w, ada_b):
    L, D, N = ada_w.shape
    R = c_rows.shape[0]
    NB = 1536
    return pl.pallas_call(
        _ada_kernel,
        grid=(L, N // NB),
        in_specs=[
            pl.BlockSpec((R, D), lambda l, n: (0, 0)),
            pl.BlockSpec((1, D, NB), lambda l, n: (l, 0, n)),
            pl.BlockSpec((1, 1, NB), lambda l, n: (l, 0, n)),
        ],
        out_specs=pl.BlockSpec((1, R, NB), lambda l, n: (l, 0, n)),
        out_shape=jax.ShapeDtypeStruct((L, R, N), F32),
        compiler_params=pltpu.CompilerParams(
            dimension_semantics=("arbitrary", "arbitrary"), vmem_limit_bytes=VMEM_LIMIT),
        name="ada",
    )(c_rows, ada_w, ada_b.reshape(L, 1, N))


def _proj_kernel(za_ref, zb_ref, m_ref, g_ref, win_ref, wup_ref, aup_ref, w0_ref, a0_ref, kk_ref, ka_ref,
                 rk_ref, ones_ref, tri_ref,
                 cb_ref, u_ref, v_ref, prod_ref, gl_ref, ops_ref, cl_ref,
                 *, split):
    base = 3 * D_CONV

    def head(bi):
        m = m_ref[bi, 0]
        h = _rms(_z_tile(za_ref, zb_ref, split, bi)) * g_ref[...]
        h = (h * (1.0 + m[1:2]) + m[0:1]).astype(BF16)

        def proj(lo, width):
            return jnp.dot(h, win_ref[:, lo:lo + width], preferred_element_type=F32)

        return (proj, proj(base + 3 * D_RWKV, LORA_WA + G_LORA), proj(base + D_RWKV, D_RWKV),
                proj(base, D_RWKV), proj(base + 2 * D_RWKV, D_RWKV))

    heads = [head(bi) for bi in range(cb_ref.shape[0])]
    for bi, (proj, lora, k, r, v) in enumerate(heads):
        wa = lora[:, :LORA_WA]
        gl = lora[:, LORA_WA:]
        ones_bd = ones_ref[...]

        kraw = k * kk_ref[...]
        kkn = kraw * lax.rsqrt(jnp.maximum(_dot01(kraw * kraw, ones_bd), 1e-24))
        lw2 = -DECAY_SCALE * jax.nn.sigmoid(
            w0_ref[...] + jnp.dot(jnp.tanh(wa).astype(BF16), wup_ref[...], preferred_element_type=F32))
        a2 = jax.nn.sigmoid(
            a0_ref[...] + jnp.dot(wa.astype(BF16), aup_ref[...], preferred_element_type=F32))

        cb_ref[bi] = proj(0, D_CONV)
        v_ref[bi] = v.astype(BF16)
        prod_ref[bi] = r * k * rk_ref[...]
        gl_ref[bi] = gl

        zero_rows = jnp.zeros((8 - CPT, D_RWKV), F32)
        conv_in = []
        for d in range(2):
            conv_in.append(proj((1 + d) * D_CONV, D_CONV))
            lw = lw2[:, d * D_RWKV:(d + 1) * D_RWKV]
            a = a2[:, d * D_RWKV:(d + 1) * D_RWKV]
            c = _dot01(lw, tri_ref[d], left=True)
            e_neg = jnp.exp(-c)
            ops = (kkn * jnp.exp(c - lw), r * jnp.exp(c), kkn * a * e_neg,
                   k * (1.0 + (a - 1.0) * ka_ref[...]) * e_neg)
            for j, x in enumerate(ops):
                ops_ref[d, bi, :, j * D_RWKV:(j + 1) * D_RWKV] = x.astype(BF16)
            last = CHUNK - 1 if d == 0 else 0
            cl_ref[d, bi, 0] = jnp.concatenate(
                [c[j * CHUNK + last:j * CHUNK + last + 1] for j in range(CPT)] + [zero_rows], axis=0)
        u_ref[bi] = conv_in[0] * conv_in[1]


def _proj(za, zb, split, modsel, g1, win, wup, aup, w0, a0, k_k, k_a, r_k, ones_bd, tri):
    B, D = za.shape[0], za.shape[2]
    T = za.shape[1] + zb.shape[1] if split else za.shape[1]
    NTL = T // TM
    P = win.shape[1]
    W = D_RWKV
    bt = _rows_per_step(B)
    tok = pl.BlockSpec((bt, TM, W), lambda b, i: (b, i, 0))
    tok2 = pl.BlockSpec((2, bt, TM, 4 * W), lambda b, i: (0, b, i, 0))
    const = lambda shape: pl.BlockSpec(shape, lambda b, i: (0,) * len(shape))
    f32_tok = jax.ShapeDtypeStruct((B, T, W), F32)
    bf_tok2 = jax.ShapeDtypeStruct((2, B, T, 4 * W), BF16)
    return pl.pallas_call(
        functools.partial(_proj_kernel, split=split),
        grid=(B // bt, NTL),
        in_specs=_z_specs(split, bt, D) + [
            pl.BlockSpec((bt, 1, 8, D), lambda b, i: (b, jnp.minimum(i, 1), 0, 0)),
            const((1, D)), const((D, P)), const((LORA_WA, 2 * W)), const((LORA_WA, 2 * W)),
            const((1, 2 * W)), const((1, 2 * W)), const((1, W)), const((1, W)),
            const((1, W)), const((W, W)), const((2, TM, TM)),
        ],
        out_specs=[tok, tok, tok, tok, pl.BlockSpec((bt, TM, G_LORA), lambda b, i: (b, i, 0)),
                   tok2,
                   pl.BlockSpec((2, bt, 1, 8, W), lambda b, i: (0, b, i, 0, 0))],
        out_shape=[f32_tok, f32_tok, jax.ShapeDtypeStruct((B, T, W), BF16), f32_tok,
                   jax.ShapeDtypeStruct((B, T, G_LORA), F32),
                   bf_tok2,
                   jax.ShapeDtypeStruct((2, B, NTL, 8, W), F32)],
        compiler_params=pltpu.CompilerParams(
            dimension_semantics=("parallel", "parallel"), vmem_limit_bytes=VMEM_LIMIT),
        name="proj",
    )(za, zb, modsel, g1, win, wup, aup, w0, a0, k_k, k_a, r_k, ones_bd, tri)


def _mm(a, x, dims):
    return _dg(a.astype(BF16), x.astype(BF16), dims)


def _scan_chunks(chains, bdmask):
    C = CHUNK
    row = lax.broadcasted_iota(jnp.int32, (C, GROUP), 0)
    col = lax.broadcasted_iota(jnp.int32, (C, GROUP), 1) & (HEAD - 1)
    eye = (row == col).astype(F32)
    blk16 = (row >> 4) == (col >> 4)
    lane_head = lax.broadcasted_iota(jnp.int32, (HEAD, GROUP), 1) >> 6
    zero_bf = jnp.zeros((), BF16)

    def each(f, *lists):
        return [f(*a) for a in zip(*lists)]

    def bd(x):
        return jnp.where(bdmask, jnp.concatenate([x.astype(BF16)] * HPG, axis=0), zero_bf)

    def bd2(x, y, axis):
        return jnp.concatenate([bd(x), bd(y)], axis=axis)

    def rows(*xs):
        return jnp.concatenate([x.astype(BF16) for x in xs], axis=0)

    def diag_blocks(full):
        out = jnp.where(lane_head == 0, full[0:HEAD], 0.0)
        for hh in range(1, HPG):
            out = out + jnp.where(lane_head == hh, full[hh * HEAD:(hh + 1) * HEAD], 0.0)
        return out

    rev = [ch["rev"] for ch in chains]
    at, rt, bt, kt, v, p_c, s0 = ([ch[n] for ch in chains]
                                  for n in ("at", "rt", "bt", "kt", "v", "p_c", "s0"))
    strict = [(col > row) if rv else (col < row) for rv in rev]
    incl = [(col >= row) if rv else (col <= row) for rv in rev]
    bh = each(lambda x, p: x * p, bt, p_c)
    kh = each(lambda x, p: x * p, kt, p_c)

    abk = each(lambda a_, r_, x, y: _mm(rows(a_, r_), bd2(x, y, 0), NT), at, rt, bt, kt)
    lmat = each(lambda m_, x: jnp.where(m_, x[:C, :GROUP], 0.0), strict, abk)
    aak = each(lambda m_, x: jnp.where(m_, x[:C, GROUP:], 0.0), strict, abk)
    arb = each(lambda m_, x: jnp.where(m_, x[C:, :GROUP], 0.0), incl, abk)
    ark = each(lambda m_, x: jnp.where(m_, x[C:, GROUP:], 0.0), incl, abk)

    S = 16
    lane16 = (lax.broadcasted_iota(jnp.int32, (S, GROUP), 1) & (HEAD - 1)) >> 4
    eye_p = (lax.broadcasted_iota(jnp.int32, (S, GROUP), 0)
             == (lax.broadcasted_iota(jnp.int32, (S, GROUP), 1) & (S - 1))).astype(F32)
    mask16 = (lax.broadcasted_iota(jnp.int32, (GROUP, GROUP), 0) >> 4) == \
             (lax.broadcasted_iota(jnp.int32, (GROUP, GROUP), 1) >> 4)

    def pack16(full):
        out = jnp.where(lane16 == 0, full[0:S], 0.0)
        for j in range(1, HEAD // S):
            out = out + jnp.where(lane16 == j, full[j * S:(j + 1) * S], 0.0)
        return out

    def unpack16(p):
        return jnp.concatenate([jnp.where(lane16 == j, p, 0.0) for j in range(HEAD // S)], axis=0)

    def bd16(p):
        return jnp.where(mask16, jnp.concatenate([p.astype(BF16)] * (GROUP // S), axis=0), zero_bf)

    ld = each(pack16, lmat)
    lo = each(lambda x: jnp.where(blk16, 0.0, x), lmat)
    p1 = each(lambda x: eye_p - x, ld)
    l2 = each(lambda x: _mm(x, bd16(x), NN), ld)
    t = each(lambda p, x: _mm(rows(p, x), bd16(x), NN), p1, l2)
    p2 = each(lambda p, x: p + x[:S], p1, t)
    l4 = each(lambda x: x[S:], t)
    t = each(lambda p, x: _mm(rows(p, x), bd16(x), NN), p2, l4)
    p3 = each(lambda p, x: p + x[:S], p2, t)
    l8 = each(lambda x: x[S:], t)
    dinv = each(lambda p, x: unpack16(p + _mm(p, bd16(x), NN)), p3, l8)
    mm = each(lambda d_, x: _mm(d_, bd(x), NN), dinv, lo)
    mm2 = each(lambda x: _mm(x, bd(x), NN), mm)
    g = each(lambda x, x2: eye - x + x2 - _mm(x, bd(x2), NN), mm, mm2)
    tinv = each(lambda g_, d_: _mm(g_, bd(d_), NN), g, dinv)

    t = each(lambda x, y, v_: _mm(rows(x, y), bd(v_), NN), aak, ark, v)
    akv = each(lambda x: x[:C], t)
    arkv = each(lambda x: x[C:], t)
    t = each(lambda t_, x, y: _mm(t_, bd2(x, y, 1), NN), tinv, at, akv)
    wm = each(lambda x: x[:, :GROUP], t)
    um = each(lambda x: x[:, GROUP:], t)
    t = each(lambda a_, x, y: _mm(a_, bd2(x, y, 1), NN), arb, wm, um)
    qh = each(lambda r_, x: r_.astype(F32) - x[:, :GROUP], rt, t)
    yl = each(lambda y_, x: y_ - x[:, GROUP:], arkv, t)
    t = each(lambda q_, w_, s_: _mm(rows(q_, w_), bd(s_), NT), qh, wm, s0)
    y = each(lambda x, y_: x[:C] + y_, t, yl)
    ds = each(lambda v_, u_, x, k_, b_: diag_blocks(_mm(rows(v_, u_, x[C:]), rows(k_, -b_, -b_), TN)),
              v, um, t, kh, bh)
    s_new = each(lambda s_, p_, d_: s_ * p_ + d_, s0, p_c, ds)
    return list(zip(y, s_new))


def _scan_kernel(vf_ref, vb_ref, opf_ref, clf_ref, opb_ref, clb_ref, yf_ref, yb_ref, s_ref,
                 *, nctx, n_chunks):
    s = pl.program_id(1)

    @pl.when(s == 0)
    def _():
        s_ref[...] = jnp.zeros_like(s_ref)

    back = jnp.where(s < nctx, nctx - 1 - s, nctx + n_chunks - 1 - s)
    bdmask = (lax.broadcasted_iota(jnp.int32, (GROUP, GROUP), 0) >> 6) == \
             (lax.broadcasted_iota(jnp.int32, (GROUP, GROUP), 1) >> 6)
    dirs = ((False, s, vf_ref, opf_ref, clf_ref, yf_ref), (True, back, vb_ref, opb_ref, clb_ref, yb_ref))
    chains, sinks = [], []
    for bi in range(s_ref.shape[0]):
        for d, (rev, chunk, v_ref, op_ref, cl_ref, y_ref) in enumerate(dirs):
            p_c = jnp.exp(cl_ref[0, bi, 0, pl.ds(chunk % CPT, 1), :])
            for q in range(D_RWKV // GROUP):
                sl = slice(q * GROUP, (q + 1) * GROUP)
                at, rt, bt, kt = (op_ref[0, bi, :, j * D_RWKV + q * GROUP:j * D_RWKV + (q + 1) * GROUP]
                                  for j in range(4))
                chains.append(dict(rev=rev, at=at, rt=rt, bt=bt, kt=kt,
                                   v=v_ref[bi, :, sl], p_c=p_c[:, sl], s0=s_ref[bi, d, :, sl]))
                sinks.append((y_ref, bi, d, sl))
    for (y, s_new), (y_ref, bi, d, sl) in zip(_scan_chunks(chains, bdmask), sinks):
        y_ref[bi, :, sl] = y
        s_ref[bi, d, :, sl] = s_new


def _scan(v, ops, cl, ctx_len):
    B, T, W = v.shape
    C = CHUNK
    NC = T // C
    nctx = ctx_len // C

    def back(s):
        return jnp.where(s < nctx, nctx - 1 - s, nctx + NC - 1 - s)

    BB = 4 if B % 4 == 0 else 2 if B % 2 == 0 else 1
    fwd = pl.BlockSpec((BB, C, W), lambda bb, s: (bb, s, 0))
    bwd = pl.BlockSpec((BB, C, W), lambda bb, s: (bb, back(s), 0))
    fwd_d = pl.BlockSpec((1, BB, C, 4 * W), lambda bb, s: (0, bb, s, 0))
    bwd_d = pl.BlockSpec((1, BB, C, 4 * W), lambda bb, s: (1, bb, back(s), 0))
    fwd_c = pl.BlockSpec((1, BB, 1, 8, W), lambda bb, s: (0, bb, s // CPT, 0, 0))
    bwd_c = pl.BlockSpec((1, BB, 1, 8, W), lambda bb, s: (1, bb, back(s) // CPT, 0, 0))
    out = jax.ShapeDtypeStruct((B, T, W), F32)
    return pl.pallas_call(
        functools.partial(_scan_kernel, nctx=nctx, n_chunks=NC),
        grid=(B // BB, NC),
        in_specs=[fwd, bwd, fwd_d, fwd_c, bwd_d, bwd_c],
        out_specs=[fwd, bwd],
        out_shape=[out, out],
        scratch_shapes=[pltpu.VMEM((BB, 2, HEAD, W), F32)],
        compiler_params=pltpu.CompilerParams(
            dimension_semantics=("parallel", "arbitrary"), vmem_limit_bytes=VMEM_LIMIT),
        name="scan",
    )(v, v, ops, cl, ops, cl)


def _mix_kernel(za_ref, zb_ref, m_ref, cb_ref, u_ref, up_ref, un_ref, yf_ref, yb_ref, prod_ref, v_ref,
                gl_ref, cw_ref, lng_ref, lnb_ref, ones_ref, gup_ref, wout_ref, x1_ref, *, n_tiles, split):
    i = pl.program_id(1)
    for bi in range(x1_ref.shape[0]):
        m = m_ref[bi, 0]
        prev_ok = (i >= 2).astype(F32)
        next_ok = jnp.logical_and(i >= 1, i <= n_tiles - 2).astype(F32)
        u = u_ref[bi]
        rowi = lax.broadcasted_iota(jnp.int32, (TM, 1), 0)
        um1 = jnp.where(rowi == 0, up_ref[bi][7:8] * prev_ok, pltpu.roll(u, 1, 0))
        up1 = jnp.where(rowi == TM - 1, un_ref[bi][0:1] * next_ok, pltpu.roll(u, TM - 1, 0))
        cw = cw_ref[...]
        yconv = cb_ref[bi] * (cw[0:1] * um1 + cw[1:2] * u + cw[2:3] * up1)

        ones_bd = ones_ref[...]
        y = yf_ref[bi] + yb_ref[bi]
        mu = _dot01(y, ones_bd) * (1.0 / HEAD)
        dlt = y - mu
        var = _dot01(dlt * dlt, ones_bd) * (1.0 / HEAD)
        yn = dlt * lax.rsqrt(var + GN_EPS) * lng_ref[...] + lnb_ref[...]
        bonus = _dot01(prod_ref[bi], ones_bd) * v_ref[bi].astype(F32)
        gate = jnp.dot(jax.nn.sigmoid(gl_ref[bi]).astype(BF16), gup_ref[...], preferred_element_type=F32)
        yrw = (yn + bonus) * gate

        out = (jnp.dot(yconv.astype(BF16), wout_ref[0:D_CONV, :], preferred_element_type=F32)
               + jnp.dot(yrw.astype(BF16), wout_ref[D_CONV:D_CONV + D_RWKV, :], preferred_element_type=F32))
        x1_ref[bi] = _z_tile(za_ref, zb_ref, split, bi) + m[2:3] * out


def _mix(za, zb, split, modsel, cb, u, yf, yb, prod, v, gl, cw, lng, lnb, ones_bd, gup, wout):
    B, T, D = cb.shape[0], cb.shape[1], za.shape[2]
    NTL = T // TM
    bt = _rows_per_step(B)
    tok = pl.BlockSpec((bt, TM, D_RWKV), lambda b, i: (b, i, 0))
    const = lambda shape: pl.BlockSpec(shape, lambda b, i: (0,) * len(shape))
    R8 = TM // 8
    return pl.pallas_call(
        functools.partial(_mix_kernel, n_tiles=NTL, split=split),
        grid=(B // bt, NTL),
        in_specs=_z_specs(split, bt, D) + [
            pl.BlockSpec((bt, 1, 8, D), lambda b, i: (b, jnp.minimum(i, 1), 0, 0)),
            tok, tok,
            pl.BlockSpec((bt, 8, D_CONV), lambda b, i: (b, jnp.maximum(i * R8 - 1, 0), 0)),
            pl.BlockSpec((bt, 8, D_CONV), lambda b, i: (b, jnp.minimum((i + 1) * R8, T // 8 - 1), 0)),
            tok, tok, tok, tok, pl.BlockSpec((bt, TM, G_LORA), lambda b, i: (b, i, 0)),
            const((8, D_CONV)), const((1, D_RWKV)), const((1, D_RWKV)), const((D_RWKV, D_RWKV)),
            const((G_LORA, D_RWKV)), const((D_CONV + D_RWKV, D)),
        ],
        out_specs=pl.BlockSpec((bt, TM, D), lambda b, i: (b, i, 0)),
        out_shape=jax.ShapeDtypeStruct((B, T, D), F32),
        compiler_params=pltpu.CompilerParams(
            dimension_semantics=("parallel", "parallel"), vmem_limit_bytes=VMEM_LIMIT),
        name="mix",
    )(za, zb, modsel, cb, u, u, u, yf, yb, prod, v, gl, cw, lng, lnb, ones_bd, gup, wout)


def _ffn_kernel(x1_ref, xb_ref, m_ref, g2_ref, wup_ref, cw_ref, cbias_ref, wdn_ref, fg_ref,
                o_ref, gtop_ref, *, n_tiles, d_ff, final):
    i = pl.program_id(1)
    lat = (i >= 1).astype(F32)
    top_ok = i >= 2
    bot_ok = jnp.logical_and(i >= 1, i <= n_tiles - 2).astype(F32)
    rowi = lax.broadcasted_iota(jnp.int32, (TM, 1), 0)
    colp = jnp.where(i == 0, rowi, rowi & (GRID_W - 1))
    width = jnp.where(i == 0, TM, GRID_W)
    lmask = (colp > 0).astype(F32)
    rmask = (colp < width - 1).astype(F32)
    NB = 256

    @pl.when(i == 0)
    def _():
        gtop_ref[...] = jnp.zeros_like(gtop_ref)

    for bi in range(x1_ref.shape[0]):
        m = m_ref[bi, 0]

        def norm2(x, keep):
            h = _rms(x) * g2_ref[...]
            return ((h * (1.0 + m[4:5]) + m[3:4]) * keep).astype(BF16)

        x1 = x1_ref[bi]
        hx = jnp.concatenate([norm2(x1, 1.0), norm2(xb_ref[bi], bot_ok)], axis=0)
        hc = hx[0:TM]

        def up(n):
            return (jnp.dot(hx, wup_ref[:, n:n + NB], preferred_element_type=F32),
                    jnp.dot(hc, wup_ref[:, d_ff + n:d_ff + n + NB], preferred_element_type=F32))

        acc = jnp.zeros((TM, o_ref.shape[-1]), F32)
        ahead = [up(j * NB) for j in range(UP_AHEAD)]
        prev = None
        for n in range(0, d_ff, NB):
            gfull, val = ahead.pop(0)
            if n + UP_AHEAD * NB < d_ff:
                ahead.append(up(n + UP_AHEAD * NB))
            if prev is not None:
                acc = acc + jnp.dot(prev, wdn_ref[n - NB:n, :], preferred_element_type=F32)
            g, gd = gfull[0:TM], gfull[GRID_W:GRID_W + TM]
            top = jnp.where(top_ok, gtop_ref[bi, :, n:n + NB], 0.0)
            gu = jnp.concatenate([top, g[0:TM - GRID_W]], axis=0)
            gtop_ref[bi, :, n:n + NB] = g[TM - GRID_W:TM]
            w = cw_ref[:, n:n + NB]
            wv = w * lat
            h0 = wv[0:1] * gu + w[3:4] * g + wv[6:7] * gd
            h1 = wv[1:2] * gu + w[4:5] * g + wv[7:8] * gd
            h2 = wv[2:3] * gu + w[5:6] * g + wv[8:9] * gd
            conv = (lmask * pltpu.roll(h0, 1, 0) + h1 + rmask * pltpu.roll(h2, TM - 1, 0)
                    + cbias_ref[:, n:n + NB])
            prev = (_silu(conv) * val).astype(BF16)
        acc = acc + jnp.dot(prev, wdn_ref[d_ff - NB:d_ff, :], preferred_element_type=F32)
        x2 = x1 + m[5:6] * acc
        if final:
            x2 = _rms(x2) * fg_ref[...]
        o_ref[bi] = x2


def _ffn(x1, modsel, g2, wup, cw, cbias, wdn, fg, final):
    B, T, D = x1.shape
    NTL = T // TM
    d_ff = wdn.shape[0]
    RW = TM // GRID_W
    bt = 1
    const = lambda shape: pl.BlockSpec(shape, lambda b, i: (0,) * len(shape))
    if final:
        out_spec = pl.BlockSpec((bt, TM, D), lambda b, i: (b, jnp.maximum(i - 1, 0), 0))
        out_shape = jax.ShapeDtypeStruct((B, T - TM, D), F32)
    else:
        out_spec = pl.BlockSpec((bt, TM, D), lambda b, i: (b, i, 0))
        out_shape = jax.ShapeDtypeStruct((B, T, D), F32)
    return pl.pallas_call(
        functools.partial(_ffn_kernel, n_tiles=NTL, d_ff=d_ff, final=final),
        grid=(B // bt, NTL),
        in_specs=[
            pl.BlockSpec((bt, TM, D), lambda b, i: (b, i, 0)),
            pl.BlockSpec((bt, GRID_W, D),
                         lambda b, i: (b, jnp.minimum((i + 1) * RW, T // GRID_W - 1), 0)),
            pl.BlockSpec((bt, 1, 8, D), lambda b, i: (b, jnp.minimum(i, 1), 0, 0)),
            const((1, D)), const((D, 2 * d_ff)),
            const((16, d_ff)), const((1, d_ff)), const((d_ff, D)), const((1, D)),
        ],
        out_specs=out_spec,
        out_shape=out_shape,
        scratch_shapes=[pltpu.VMEM((bt, GRID_W, d_ff), F32)],
        compiler_params=pltpu.CompilerParams(
            dimension_semantics=("parallel", "arbitrary"), vmem_limit_bytes=VMEM_LIMIT),
        name="ffn",
    )(x1, x1, modsel, g2, wup, cw, cbias, wdn, fg)


def kernel(x, c, ctx, c_ctx, ada_w, ada_b, norm1_g, norm2_g, w_in, conv_a_w, rw_w0, rw_w_up, rw_a0,
           rw_a_up, rw_k_k, rw_k_a, rw_r_k, rw_g_up, rw_ln_g, rw_ln_b, w_out, ffn_w_up, ffn_conv_w,
           ffn_conv_b, ffn_w_down, final_g):
    B, SEQ, D = x.shape
    CTX = ctx.shape[1]
    L = w_in.shape[0]
    d_ff = ffn_w_down.shape[1]
    assert CTX == TM and SEQ % TM == 0 and TM % GRID_W == 0 and CHUNK == HEAD and CPT <= 8
    assert w_in.shape[2] == 3 * D_CONV + 3 * D_RWKV + LORA_WA + G_LORA and d_ff % 256 == 0

    rows = -(-(B + 1) // 8) * 8
    c_rows = jnp.zeros((rows, D), F32).at[:B].set(c).at[B].set(c_ctx)
    mod = _ada(c_rows, ada_w, ada_b)

    hw = LORA_WA // 2
    zpad = jnp.zeros((L, hw, 2 * D_RWKV), F32)
    both = lambda t: jnp.concatenate([t[:, 0], t[:, 1]], axis=-1)
    wup_ext = jnp.concatenate([both(rw_w_up), zpad], axis=1).astype(BF16)
    aup_ext = jnp.concatenate([zpad, both(rw_a_up)], axis=1).astype(BF16)
    w0_2 = both(rw_w0[:, :, None, :])
    a0_2 = both(rw_a0[:, :, None, :])
    head_of = jnp.arange(D_RWKV) // HEAD
    ones_bd = (head_of[:, None] == head_of[None, :]).astype(BF16)
    tt = jnp.arange(TM)
    same = (tt[:, None] // CHUNK) == (tt[None, :] // CHUNK)
    tri = jnp.stack([same & (tt[None, :] <= tt[:, None]),
                     same & (tt[None, :] >= tt[:, None])]).astype(BF16)
    cw_a = jnp.zeros((L, 8, D_CONV), F32).at[:, :3].set(conv_a_w)
    cw_f = jnp.zeros((L, 16, d_ff), F32).at[:, :9].set(ffn_conv_w.reshape(L, 9, d_ff))

    za, zb = ctx, x
    for l in range(L):
        split = l == 0
        lat = mod[l, :B].reshape(B, 6, D)
        cm = jnp.broadcast_to(mod[l, B].reshape(1, 6, D), (B, 6, D))
        modsel = jnp.zeros((B, 2, 8, D), F32).at[:, 0, :6].set(cm).at[:, 1, :6].set(lat)

        cb, u, v, prod, gl, ops, cl = _proj(
            za, zb, split, modsel, norm1_g[l].reshape(1, D), w_in[l].astype(BF16), wup_ext[l], aup_ext[l],
            w0_2[l], a0_2[l], rw_k_k[l].reshape(1, -1), rw_k_a[l].reshape(1, -1),
            rw_r_k[l].reshape(1, -1), ones_bd, tri)
        yf, yb = _scan(v, ops, cl, CTX)
        x1 = _mix(
            za, zb, split, modsel, cb, u, yf, yb, prod, v, gl, cw_a[l], rw_ln_g[l].reshape(1, -1),
            rw_ln_b[l].reshape(1, -1), ones_bd, rw_g_up[l].astype(BF16), w_out[l].astype(BF16))
        za = zb = _ffn(x1, modsel, norm2_g[l].reshape(1, D), ffn_w_up[l].astype(BF16), cw_f[l],
                       ffn_conv_b[l].reshape(1, -1), ffn_w_down[l].astype(BF16), final_g.reshape(1, D),
                       final=(l == L - 1))
    return za
```

```python
import functools
import math

import jax
import jax.numpy as jnp
from jax import lax
from jax.experimental import pallas as pl
from jax.experimental.pallas import tpu as pltpu

F32 = jnp.float32
BF16 = jnp.bfloat16

HEAD = 64
D_CONV = 512
D_RWKV = 512
LORA_WA = 128
G_LORA = 128
GRID_W = 64
RMS_EPS = 1e-6
GN_EPS = 64e-5
DECAY_SCALE = math.exp(-0.5)

TM = 256
CHUNK = 64
CPT = TM // CHUNK
BT = 2
UP_AHEAD = 4
HPG = 2
GROUP = HPG * HEAD
HEAD_LOG2 = 6
BLK = 16
BLK_LOG2 = 4
assert 1 << HEAD_LOG2 == HEAD and 1 << BLK_LOG2 == BLK

SUBLANES = 8
MXU_W = 256
ADA_NB = 1536
VMEM_LIMIT = 56 * 1024 * 1024

NN = (((1,), (0,)), ((), ()))
NT = (((1,), (1,)), ((), ()))
TN = (((0,), (0,)), ((), ()))


def _silu(x):
    return x * jax.nn.sigmoid(x)


def _split(x):
    hi = x.astype(BF16)
    return hi, (x - hi.astype(F32)).astype(BF16)


def _dg(a, b, dims):
    return lax.dot_general(a, b, dims, preferred_element_type=F32)


def _mm3(a, b_hi, b_lo, dims):
    a_hi, a_lo = _split(a)
    return _dg(a_hi, b_hi, dims) + (_dg(a_hi, b_lo, dims) + _dg(a_lo, b_hi, dims))


def _rms(x):
    return x * lax.rsqrt(jnp.mean(x * x, axis=-1, keepdims=True) + RMS_EPS)


def _dot01(t, m01, left=False):
    hi, lo = _split(t)
    if left:
        return _dg(m01, hi, NN) + _dg(m01, lo, NN)
    return _dg(hi, m01, NN) + _dg(lo, m01, NN)


def _z_tile(za_ref, zb_ref, split, bi):
    if not split:
        return za_ref[bi]
    return jnp.where(pl.program_id(1) == 0, za_ref[bi], zb_ref[bi])


def _z_specs(split, bt, D):
    if split:
        return [pl.BlockSpec((bt, TM, D), lambda b, i: (b, 0, 0)),
                pl.BlockSpec((bt, TM, D), lambda b, i: (b, jnp.maximum(i - 1, 0), 0))]
    return [pl.BlockSpec((bt, TM, D), lambda b, i: (b, i, 0)),
            pl.BlockSpec((1, SUBLANES, D), lambda b, i: (0, 0, 0))]


def _rows_per_step(B):
    return BT if B % BT == 0 else 1


def _ada_kernel(c_ref, w_ref, b_ref, o_ref):
    s = _silu(c_ref[...])
    w_hi, w_lo = _split(w_ref[0])
    o_ref[0] = _mm3(s, w_hi, w_lo, NN) + b_ref[0]


def _ada(c_rows, ada_w, ada_b):
    L, D, N = ada_w.shape
    R = c_rows.shape[0]
    NB = ADA_NB
    return pl.pallas_call(
        _ada_kernel,
        grid=(L, N // NB),
        in_specs=[
            pl.BlockSpec((R, D), lambda l, n: (0, 0)),
            pl.BlockSpec((1, D, NB), lambda l, n: (l, 0, n)),
            pl.BlockSpec((1, 1, NB), lambda l, n: (l, 0, n)),
        ],
        out_specs=pl.BlockSpec((1, R, NB), lambda l, n: (l, 0, n)),
        out_shape=jax.ShapeDtypeStruct((L, R, N), F32),
        compiler_params=pltpu.CompilerParams(
            dimension_semantics=("arbitrary", "arbitrary"), vmem_limit_bytes=VMEM_LIMIT),
        name="ada",
    )(c_rows, ada_w, ada_b.reshape(L, 1, N))


def _proj_kernel(za_ref, zb_ref, m_ref, g_ref, win_ref, wup_ref, aup_ref, w0_ref, a0_ref, kk_ref, ka_ref,
                 rk_ref, ones_ref, tri_ref,
                 cb_ref, u_ref, v_ref, prod_ref, gl_ref, ops_ref, cl_ref,
                 *, split):
    base = 3 * D_CONV

    def head(bi):
        m = m_ref[bi, 0]
        h = _rms(_z_tile(za_ref, zb_ref, split, bi)) * g_ref[...]
        h = (h * (1.0 + m[1:2]) + m[0:1]).astype(BF16)

        def proj(lo, width):
            return jnp.dot(h, win_ref[:, lo:lo + width], preferred_element_type=F32)

        return (proj, proj(base + 3 * D_RWKV, LORA_WA + G_LORA), proj(base + D_RWKV, D_RWKV),
                proj(base, D_RWKV), proj(base + 2 * D_RWKV, D_RWKV))

    heads = [head(bi) for bi in range(cb_ref.shape[0])]
    for bi, (proj, lora, k, r, v) in enumerate(heads):
        wa = lora[:, :LORA_WA]
        gl = lora[:, LORA_WA:]
        ones_bd = ones_ref[...]

        kraw = k * kk_ref[...]
        kkn = kraw * lax.rsqrt(jnp.maximum(_dot01(kraw * kraw, ones_bd), 1e-24))
        lw2 = -DECAY_SCALE * jax.nn.sigmoid(
            w0_ref[...] + jnp.dot(jnp.tanh(wa).astype(BF16), wup_ref[...], preferred_element_type=F32))
        a2 = jax.nn.sigmoid(
            a0_ref[...] + jnp.dot(wa.astype(BF16), aup_ref[...], preferred_element_type=F32))

        cb_ref[bi] = proj(0, D_CONV)
        v_ref[bi] = v.astype(BF16)
        prod_ref[bi] = r * k * rk_ref[...]
        gl_ref[bi] = gl

        zero_rows = jnp.zeros((SUBLANES - CPT, D_RWKV), F32)
        conv_in = []
        for d in range(2):
            conv_in.append(proj((1 + d) * D_CONV, D_CONV))
            lw = lw2[:, d * D_RWKV:(d + 1) * D_RWKV]
            a = a2[:, d * D_RWKV:(d + 1) * D_RWKV]
            c = _dot01(lw, tri_ref[d], left=True)
            e_neg = jnp.exp(-c)
            ops = (kkn * jnp.exp(c - lw), r * jnp.exp(c), kkn * a * e_neg,
                   k * (1.0 + (a - 1.0) * ka_ref[...]) * e_neg)
            for j, x in enumerate(ops):
                ops_ref[d, bi, :, j * D_RWKV:(j + 1) * D_RWKV] = x.astype(BF16)
            last = CHUNK - 1 if d == 0 else 0
            cl_ref[d, bi, 0] = jnp.concatenate(
                [c[j * CHUNK + last:j * CHUNK + last + 1] for j in range(CPT)] + [zero_rows], axis=0)
        u_ref[bi] = conv_in[0] * conv_in[1]


def _proj(za, zb, split, modsel, g1, win, wup, aup, w0, a0, k_k, k_a, r_k, ones_bd, tri):
    B, D = za.shape[0], za.shape[2]
    T = za.shape[1] + zb.shape[1] if split else za.shape[1]
    NTL = T // TM
    P = win.shape[1]
    W = D_RWKV
    bt = _rows_per_step(B)
    tok = pl.BlockSpec((bt, TM, W), lambda b, i: (b, i, 0))
    tok2 = pl.BlockSpec((2, bt, TM, 4 * W), lambda b, i: (0, b, i, 0))
    const = lambda shape: pl.BlockSpec(shape, lambda b, i: (0,) * len(shape))
    f32_tok = jax.ShapeDtypeStruct((B, T, W), F32)
    bf_tok2 = jax.ShapeDtypeStruct((2, B, T, 4 * W), BF16)
    return pl.pallas_call(
        functools.partial(_proj_kernel, split=split),
        grid=(B // bt, NTL),
        in_specs=_z_specs(split, bt, D) + [
            pl.BlockSpec((bt, 1, SUBLANES, D), lambda b, i: (b, jnp.minimum(i, 1), 0, 0)),
            const((1, D)), const((D, P)), const((LORA_WA, 2 * W)), const((LORA_WA, 2 * W)),
            const((1, 2 * W)), const((1, 2 * W)), const((1, W)), const((1, W)),
            const((1, W)), const((W, W)), const((2, TM, TM)),
        ],
        out_specs=[tok, tok, tok, tok, pl.BlockSpec((bt, TM, G_LORA), lambda b, i: (b, i, 0)),
                   tok2,
                   pl.BlockSpec((2, bt, 1, SUBLANES, W), lambda b, i: (0, b, i, 0, 0))],
        out_shape=[f32_tok, f32_tok, jax.ShapeDtypeStruct((B, T, W), BF16), f32_tok,
                   jax.ShapeDtypeStruct((B, T, G_LORA), F32),
                   bf_tok2,
                   jax.ShapeDtypeStruct((2, B, NTL, SUBLANES, W), F32)],
        compiler_params=pltpu.CompilerParams(
            dimension_semantics=("parallel", "parallel"), vmem_limit_bytes=VMEM_LIMIT),
        name="proj",
    )(za, zb, modsel, g1, win, wup, aup, w0, a0, k_k, k_a, r_k, ones_bd, tri)


def _mm(a, x, dims):
    return _dg(a.astype(BF16), x.astype(BF16), dims)


def _scan_chunks(chains, bdmask):
    C = CHUNK
    row = lax.broadcasted_iota(jnp.int32, (C, GROUP), 0)
    col = lax.broadcasted_iota(jnp.int32, (C, GROUP), 1) & (HEAD - 1)
    eye = (row == col).astype(F32)
    blk16 = (row >> BLK_LOG2) == (col >> BLK_LOG2)
    lane_head = lax.broadcasted_iota(jnp.int32, (HEAD, GROUP), 1) >> HEAD_LOG2
    zero_bf = jnp.zeros((), BF16)

    def each(f, *lists):
        return [f(*a) for a in zip(*lists)]

    def bd(x):
        return jnp.where(bdmask, jnp.concatenate([x.astype(BF16)] * HPG, axis=0), zero_bf)

    def bd2(x, y, axis):
        return jnp.concatenate([bd(x), bd(y)], axis=axis)

    def rows(*xs):
        return jnp.concatenate([x.astype(BF16) for x in xs], axis=0)

    def diag_blocks(full):
        out = jnp.where(lane_head == 0, full[0:HEAD], 0.0)
        for hh in range(1, HPG):
            out = out + jnp.where(lane_head == hh, full[hh * HEAD:(hh + 1) * HEAD], 0.0)
        return out

    rev = [ch["rev"] for ch in chains]
    at, rt, bt, kt, v, p_c, s0 = ([ch[n] for ch in chains]
                                  for n in ("at", "rt", "bt", "kt", "v", "p_c", "s0"))
    strict = [(col > row) if rv else (col < row) for rv in rev]
    incl = [(col >= row) if rv else (col <= row) for rv in rev]
    bh = each(lambda x, p: x * p, bt, p_c)
    kh = each(lambda x, p: x * p, kt, p_c)

    abk = each(lambda a_, r_, x, y: _mm(rows(a_, r_), bd2(x, y, 0), NT), at, rt, bt, kt)
    lmat = each(lambda m_, x: jnp.where(m_, x[:C, :GROUP], 0.0), strict, abk)
    aak = each(lambda m_, x: jnp.where(m_, x[:C, GROUP:], 0.0), strict, abk)
    arb = each(lambda m_, x: jnp.where(m_, x[C:, :GROUP], 0.0), incl, abk)
    ark = each(lambda m_, x: jnp.where(m_, x[C:, GROUP:], 0.0), incl, abk)

    S = BLK
    lane16 = (lax.broadcasted_iota(jnp.int32, (S, GROUP), 1) & (HEAD - 1)) >> BLK_LOG2
    eye_p = (lax.broadcasted_iota(jnp.int32, (S, GROUP), 0)
             == (lax.broadcasted_iota(jnp.int32, (S, GROUP), 1) & (S - 1))).astype(F32)
    mask16 = (lax.broadcasted_iota(jnp.int32, (GROUP, GROUP), 0) >> BLK_LOG2) == \
             (lax.broadcasted_iota(jnp.int32, (GROUP, GROUP), 1) >> BLK_LOG2)

    def pack16(full):
        out = jnp.where(lane16 == 0, full[0:S], 0.0)
        for j in range(1, HEAD // S):
            out = out + jnp.where(lane16 == j, full[j * S:(j + 1) * S], 0.0)
        return out

    def unpack16(p):
        return jnp.concatenate([jnp.where(lane16 == j, p, 0.0) for j in range(HEAD // S)], axis=0)

    def bd16(p):
        return jnp.where(mask16, jnp.concatenate([p.astype(BF16)] * (GROUP // S), axis=0), zero_bf)

    ld = each(pack16, lmat)
    lo = each(lambda x: jnp.where(blk16, 0.0, x), lmat)
    p1 = each(lambda x: eye_p - x, ld)
    l2 = each(lambda x: _mm(x, bd16(x), NN), ld)
    t = each(lambda p, x: _mm(rows(p, x), bd16(x), NN), p1, l2)
    p2 = each(lambda p, x: p + x[:S], p1, t)
    l4 = each(lambda x: x[S:], t)
    t = each(lambda p, x: _mm(rows(p, x), bd16(x), NN), p2, l4)
    p3 = each(lambda p, x: p + x[:S], p2, t)
    l8 = each(lambda x: x[S:], t)
    dinv = each(lambda p, x: unpack16(p + _mm(p, bd16(x), NN)), p3, l8)
    mm = each(lambda d_, x: _mm(d_, bd(x), NN), dinv, lo)
    mm2 = each(lambda x: _mm(x, bd(x), NN), mm)
    g = each(lambda x, x2: eye - x + x2 - _mm(x, bd(x2), NN), mm, mm2)
    tinv = each(lambda g_, d_: _mm(g_, bd(d_), NN), g, dinv)

    t = each(lambda x, y, v_: _mm(rows(x, y), bd(v_), NN), aak, ark, v)
    akv = each(lambda x: x[:C], t)
    arkv = each(lambda x: x[C:], t)
    t = each(lambda t_, x, y: _mm(t_, bd2(x, y, 1), NN), tinv, at, akv)
    wm = each(lambda x: x[:, :GROUP], t)
    um = each(lambda x: x[:, GROUP:], t)
    t = each(lambda a_, x, y: _mm(a_, bd2(x, y, 1), NN), arb, wm, um)
    qh = each(lambda r_, x: r_.astype(F32) - x[:, :GROUP], rt, t)
    yl = each(lambda y_, x: y_ - x[:, GROUP:], arkv, t)
    t = each(lambda q_, w_, s_: _mm(rows(q_, w_), bd(s_), NT), qh, wm, s0)
    y = each(lambda x, y_: x[:C] + y_, t, yl)
    ds = each(lambda v_, u_, x, k_, b_: diag_blocks(_mm(rows(v_, u_, x[C:]), rows(k_, -b_, -b_), TN)),
              v, um, t, kh, bh)
    s_new = each(lambda s_, p_, d_: s_ * p_ + d_, s0, p_c, ds)
    return list(zip(y, s_new))


def _scan_kernel(vf_ref, vb_ref, opf_ref, clf_ref, opb_ref, clb_ref, yf_ref, yb_ref, s_ref,
                 *, nctx, n_chunks):
    s = pl.program_id(1)

    @pl.when(s == 0)
    def _():
        s_ref[...] = jnp.zeros_like(s_ref)

    back = jnp.where(s < nctx, nctx - 1 - s, nctx + n_chunks - 1 - s)
    bdmask = (lax.broadcasted_iota(jnp.int32, (GROUP, GROUP), 0) >> HEAD_LOG2) == \
             (lax.broadcasted_iota(jnp.int32, (GROUP, GROUP), 1) >> HEAD_LOG2)
    dirs = ((False, s, vf_ref, opf_ref, clf_ref, yf_ref), (True, back, vb_ref, opb_ref, clb_ref, yb_ref))
    chains, sinks = [], []
    for bi in range(s_ref.shape[0]):
        for d, (rev, chunk, v_ref, op_ref, cl_ref, y_ref) in enumerate(dirs):
            p_c = jnp.exp(cl_ref[0, bi, 0, pl.ds(chunk % CPT, 1), :])
            for q in range(D_RWKV // GROUP):
                sl = slice(q * GROUP, (q + 1) * GROUP)
                at, rt, bt, kt = (op_ref[0, bi, :, j * D_RWKV + q * GROUP:j * D_RWKV + (q + 1) * GROUP]
                                  for j in range(4))
                chains.append(dict(rev=rev, at=at, rt=rt, bt=bt, kt=kt,
                                   v=v_ref[bi, :, sl], p_c=p_c[:, sl], s0=s_ref[bi, d, :, sl]))
                sinks.append((y_ref, bi, d, sl))
    for (y, s_new), (y_ref, bi, d, sl) in zip(_scan_chunks(chains, bdmask), sinks):
        y_ref[bi, :, sl] = y
        s_ref[bi, d, :, sl] = s_new


def _scan(v, ops, cl, ctx_len):
    B, T, W = v.shape
    C = CHUNK
    NC = T // C
    nctx = ctx_len // C

    def back(s):
        return jnp.where(s < nctx, nctx - 1 - s, nctx + NC - 1 - s)

    BB = 4 if B % 4 == 0 else 2 if B % 2 == 0 else 1
    fwd = pl.BlockSpec((BB, C, W), lambda bb, s: (bb, s, 0))
    bwd = pl.BlockSpec((BB, C, W), lambda bb, s: (bb, back(s), 0))
    fwd_d = pl.BlockSpec((1, BB, C, 4 * W), lambda bb, s: (0, bb, s, 0))
    bwd_d = pl.BlockSpec((1, BB, C, 4 * W), lambda bb, s: (1, bb, back(s), 0))
    fwd_c = pl.BlockSpec((1, BB, 1, SUBLANES, W), lambda bb, s: (0, bb, s // CPT, 0, 0))
    bwd_c = pl.BlockSpec((1, BB, 1, SUBLANES, W), lambda bb, s: (1, bb, back(s) // CPT, 0, 0))
    out = jax.ShapeDtypeStruct((B, T, W), F32)
    return pl.pallas_call(
        functools.partial(_scan_kernel, nctx=nctx, n_chunks=NC),
        grid=(B // BB, NC),
        in_specs=[fwd, bwd, fwd_d, fwd_c, bwd_d, bwd_c],
        out_specs=[fwd, bwd],
        out_shape=[out, out],
        scratch_shapes=[pltpu.VMEM((BB, 2, HEAD, W), F32)],
        compiler_params=pltpu.CompilerParams(
            dimension_semantics=("parallel", "arbitrary"), vmem_limit_bytes=VMEM_LIMIT),
        name="scan",
    )(v, v, ops, cl, ops, cl)


def _mix_kernel(za_ref, zb_ref, m_ref, cb_ref, u_ref, up_ref, un_ref, yf_ref, yb_ref, prod_ref, v_ref,
                gl_ref, cw_ref, lng_ref, lnb_ref, ones_ref, gup_ref, wout_ref, x1_ref, *, n_tiles, split):
    i = pl.program_id(1)
    for bi in range(x1_ref.shape[0]):
        m = m_ref[bi, 0]
        prev_ok = (i >= 2).astype(F32)
        next_ok = jnp.logical_and(i >= 1, i <= n_tiles - 2).astype(F32)
        u = u_ref[bi]
        rowi = lax.broadcasted_iota(jnp.int32, (TM, 1), 0)
        um1 = jnp.where(rowi == 0, up_ref[bi][SUBLANES - 1:SUBLANES] * prev_ok, pltpu.roll(u, 1, 0))
        up1 = jnp.where(rowi == TM - 1, un_ref[bi][0:1] * next_ok, pltpu.roll(u, TM - 1, 0))
        cw = cw_ref[...]
        yconv = cb_ref[bi] * (cw[0:1] * um1 + cw[1:2] * u + cw[2:3] * up1)

        ones_bd = ones_ref[...]
        y = yf_ref[bi] + yb_ref[bi]
        mu = _dot01(y, ones_bd) * (1.0 / HEAD)
        dlt = y - mu
        var = _dot01(dlt * dlt, ones_bd) * (1.0 / HEAD)
        yn = dlt * lax.rsqrt(var + GN_EPS) * lng_ref[...] + lnb_ref[...]
        bonus = _dot01(prod_ref[bi], ones_bd) * v_ref[bi].astype(F32)
        gate = jnp.dot(jax.nn.sigmoid(gl_ref[bi]).astype(BF16), gup_ref[...], preferred_element_type=F32)
        yrw = (yn + bonus) * gate

        out = (jnp.dot(yconv.astype(BF16), wout_ref[0:D_CONV, :], preferred_element_type=F32)
               + jnp.dot(yrw.astype(BF16), wout_ref[D_CONV:D_CONV + D_RWKV, :], preferred_element_type=F32))
        x1_ref[bi] = _z_tile(za_ref, zb_ref, split, bi) + m[2:3] * out


def _mix(za, zb, split, modsel, cb, u, yf, yb, prod, v, gl, cw, lng, lnb, ones_bd, gup, wout):
    B, T, D = cb.shape[0], cb.shape[1], za.shape[2]
    NTL = T // TM
    bt = _rows_per_step(B)
    tok = pl.BlockSpec((bt, TM, D_RWKV), lambda b, i: (b, i, 0))
    const = lambda shape: pl.BlockSpec(shape, lambda b, i: (0,) * len(shape))
    R8 = TM // SUBLANES
    return pl.pallas_call(
        functools.partial(_mix_kernel, n_tiles=NTL, split=split),
        grid=(B // bt, NTL),
        in_specs=_z_specs(split, bt, D) + [
            pl.BlockSpec((bt, 1, SUBLANES, D), lambda b, i: (b, jnp.minimum(i, 1), 0, 0)),
            tok, tok,
            pl.BlockSpec((bt, SUBLANES, D_CONV), lambda b, i: (b, jnp.maximum(i * R8 - 1, 0), 0)),
            pl.BlockSpec((bt, SUBLANES, D_CONV),
                         lambda b, i: (b, jnp.minimum((i + 1) * R8, T // SUBLANES - 1), 0)),
            tok, tok, tok, tok, pl.BlockSpec((bt, TM, G_LORA), lambda b, i: (b, i, 0)),
            const((SUBLANES, D_CONV)), const((1, D_RWKV)), const((1, D_RWKV)), const((D_RWKV, D_RWKV)),
            const((G_LORA, D_RWKV)), const((D_CONV + D_RWKV, D)),
        ],
        out_specs=pl.BlockSpec((bt, TM, D), lambda b, i: (b, i, 0)),
        out_shape=jax.ShapeDtypeStruct((B, T, D), F32),
        compiler_params=pltpu.CompilerParams(
            dimension_semantics=("parallel", "parallel"), vmem_limit_bytes=VMEM_LIMIT),
        name="mix",
    )(za, zb, modsel, cb, u, u, u, yf, yb, prod, v, gl, cw, lng, lnb, ones_bd, gup, wout)


def _ffn_kernel(x1_ref, xb_ref, m_ref, g2_ref, wup_ref, cw_ref, cbias_ref, wdn_ref, fg_ref,
                o_ref, gtop_ref, *, n_tiles, d_ff, final):
    i = pl.program_id(1)
    lat = (i >= 1).astype(F32)
    top_ok = i >= 2
    bot_ok = jnp.logical_and(i >= 1, i <= n_tiles - 2).astype(F32)
    rowi = lax.broadcasted_iota(jnp.int32, (TM, 1), 0)
    colp = jnp.where(i == 0, rowi, rowi & (GRID_W - 1))
    width = jnp.where(i == 0, TM, GRID_W)
    lmask = (colp > 0).astype(F32)
    rmask = (colp < width - 1).astype(F32)
    NB = MXU_W

    @pl.when(i == 0)
    def _():
        gtop_ref[...] = jnp.zeros_like(gtop_ref)

    for bi in range(x1_ref.shape[0]):
        m = m_ref[bi, 0]

        def norm2(x, keep):
            h = _rms(x) * g2_ref[...]
            return ((h * (1.0 + m[4:5]) + m[3:4]) * keep).astype(BF16)

        x1 = x1_ref[bi]
        hx = jnp.concatenate([norm2(x1, 1.0), norm2(xb_ref[bi], bot_ok)], axis=0)
        hc = hx[0:TM]

        def up(n):
            return (jnp.dot(hx, wup_ref[:, n:n + NB], preferred_element_type=F32),
                    jnp.dot(hc, wup_ref[:, d_ff + n:d_ff + n + NB], preferred_element_type=F32))

        acc = jnp.zeros((TM, o_ref.shape[-1]), F32)
        ahead = [up(j * NB) for j in range(UP_AHEAD)]
        prev = None
        for n in range(0, d_ff, NB):
            gfull, val = ahead.pop(0)
            if n + UP_AHEAD * NB < d_ff:
                ahead.append(up(n + UP_AHEAD * NB))
            if prev is not None:
                acc = acc + jnp.dot(prev, wdn_ref[n - NB:n, :], preferred_element_type=F32)
            g, gd = gfull[0:TM], gfull[GRID_W:GRID_W + TM]
            top = jnp.where(top_ok, gtop_ref[bi, :, n:n + NB], 0.0)
            gu = jnp.concatenate([top, g[0:TM - GRID_W]], axis=0)
            gtop_ref[bi, :, n:n + NB] = g[TM - GRID_W:TM]
            w = cw_ref[:, n:n + NB]
            wv = w * lat
            h0 = wv[0:1] * gu + w[3:4] * g + wv[6:7] * gd
            h1 = wv[1:2] * gu + w[4:5] * g + wv[7:8] * gd
            h2 = wv[2:3] * gu + w[5:6] * g + wv[8:9] * gd
            conv = (lmask * pltpu.roll(h0, 1, 0) + h1 + rmask * pltpu.roll(h2, TM - 1, 0)
                    + cbias_ref[:, n:n + NB])
            prev = (_silu(conv) * val).astype(BF16)
        acc = acc + jnp.dot(prev, wdn_ref[d_ff - NB:d_ff, :], preferred_element_type=F32)
        x2 = x1 + m[5:6] * acc
        if final:
            x2 = _rms(x2) * fg_ref[...]
        o_ref[bi] = x2


def _ffn(x1, modsel, g2, wup, cw, cbias, wdn, fg, final):
    B, T, D = x1.shape
    NTL = T // TM
    d_ff = wdn.shape[0]
    RW = TM // GRID_W
    bt = 1
    const = lambda shape: pl.BlockSpec(shape, lambda b, i: (0,) * len(shape))
    if final:
        out_spec = pl.BlockSpec((bt, TM, D), lambda b, i: (b, jnp.maximum(i - 1, 0), 0))
        out_shape = jax.ShapeDtypeStruct((B, T - TM, D), F32)
    else:
        out_spec = pl.BlockSpec((bt, TM, D), lambda b, i: (b, i, 0))
        out_shape = jax.ShapeDtypeStruct((B, T, D), F32)
    return pl.pallas_call(
        functools.partial(_ffn_kernel, n_tiles=NTL, d_ff=d_ff, final=final),
        grid=(B // bt, NTL),
        in_specs=[
            pl.BlockSpec((bt, TM, D), lambda b, i: (b, i, 0)),
            pl.BlockSpec((bt, GRID_W, D),
                         lambda b, i: (b, jnp.minimum((i + 1) * RW, T // GRID_W - 1), 0)),
            pl.BlockSpec((bt, 1, SUBLANES, D), lambda b, i: (b, jnp.minimum(i, 1), 0, 0)),
            const((1, D)), const((D, 2 * d_ff)),
            const((2 * SUBLANES, d_ff)), const((1, d_ff)), const((d_ff, D)), const((1, D)),
        ],
        out_specs=out_spec,
        out_shape=out_shape,
        scratch_shapes=[pltpu.VMEM((bt, GRID_W, d_ff), F32)],
        compiler_params=pltpu.CompilerParams(
            dimension_semantics=("parallel", "arbitrary"), vmem_limit_bytes=VMEM_LIMIT),
        name="ffn",
    )(x1, x1, modsel, g2, wup, cw, cbias, wdn, fg)


def kernel(x, c, ctx, c_ctx, ada_w, ada_b, norm1_g, norm2_g, w_in, conv_a_w, rw_w0, rw_w_up, rw_a0,
           rw_a_up, rw_k_k, rw_k_a, rw_r_k, rw_g_up, rw_ln_g, rw_ln_b, w_out, ffn_w_up, ffn_conv_w,
           ffn_conv_b, ffn_w_down, final_g):
    B, SEQ, D = x.shape
    CTX = ctx.shape[1]
    L = w_in.shape[0]
    d_ff = ffn_w_down.shape[1]
    assert CTX == TM and SEQ % TM == 0 and TM % GRID_W == 0 and CHUNK == HEAD and CPT <= SUBLANES
    assert w_in.shape[2] == 3 * D_CONV + 3 * D_RWKV + LORA_WA + G_LORA and d_ff % MXU_W == 0

    rows = -(-(B + 1) // SUBLANES) * SUBLANES
    c_rows = jnp.zeros((rows, D), F32).at[:B].set(c).at[B].set(c_ctx)
    mod = _ada(c_rows, ada_w, ada_b)

    hw = LORA_WA // 2
    zpad = jnp.zeros((L, hw, 2 * D_RWKV), F32)
    both = lambda t: jnp.concatenate([t[:, 0], t[:, 1]], axis=-1)
    wup_ext = jnp.concatenate([both(rw_w_up), zpad], axis=1).astype(BF16)
    aup_ext = jnp.concatenate([zpad, both(rw_a_up)], axis=1).astype(BF16)
    w0_2 = both(rw_w0[:, :, None, :])
    a0_2 = both(rw_a0[:, :, None, :])
    head_of = jnp.arange(D_RWKV) // HEAD
    ones_bd = (head_of[:, None] == head_of[None, :]).astype(BF16)
    tt = jnp.arange(TM)
    same = (tt[:, None] // CHUNK) == (tt[None, :] // CHUNK)
    tri = jnp.stack([same & (tt[None, :] <= tt[:, None]),
                     same & (tt[None, :] >= tt[:, None])]).astype(BF16)
    cw_a = jnp.zeros((L, SUBLANES, D_CONV), F32).at[:, :3].set(conv_a_w)
    cw_f = jnp.zeros((L, 2 * SUBLANES, d_ff), F32).at[:, :9].set(ffn_conv_w.reshape(L, 9, d_ff))

    za, zb = ctx, x
    for l in range(L):
        split = l == 0
        lat = mod[l, :B].reshape(B, 6, D)
        cm = jnp.broadcast_to(mod[l, B].reshape(1, 6, D), (B, 6, D))
        modsel = jnp.zeros((B, 2, SUBLANES, D), F32).at[:, 0, :6].set(cm).at[:, 1, :6].set(lat)

        cb, u, v, prod, gl, ops, cl = _proj(
            za, zb, split, modsel, norm1_g[l].reshape(1, D), w_in[l].astype(BF16), wup_ext[l], aup_ext[l],
            w0_2[l], a0_2[l], rw_k_k[l].reshape(1, -1), rw_k_a[l].reshape(1, -1),
            rw_r_k[l].reshape(1, -1), ones_bd, tri)
        yf, yb = _scan(v, ops, cl, CTX)
        x1 = _mix(
            za, zb, split, modsel, cb, u, yf, yb, prod, v, gl, cw_a[l], rw_ln_g[l].reshape(1, -1),
            rw_ln_b[l].reshape(1, -1), ones_bd, rw_g_up[l].astype(BF16), w_out[l].astype(BF16))
        za = zb = _ffn(x1, modsel, norm2_g[l].reshape(1, D), ffn_w_up[l].astype(BF16), cw_f[l],
                       ffn_conv_b[l].reshape(1, -1), ffn_w_down[l].astype(BF16), final_g.reshape(1, D),
                       final=(l == L - 1))
    return za
```

```python
import functools
import math

import jax
import jax.numpy as jnp
from jax import lax
from jax.experimental import pallas as pl
from jax.experimental.pallas import tpu as pltpu

F32 = jnp.float32
BF16 = jnp.bfloat16

HEAD = 64
D_CONV = 512
D_RWKV = 512
LORA_WA = 128
G_LORA = 128
GRID_W = 64
RMS_EPS = 1e-6
GN_EPS = 64e-5
DECAY_SCALE = math.exp(-0.5)

TM = 256
CHUNK = 64
CPT = TM // CHUNK
BT = 2
UP_AHEAD = 6
HPG = 2
GROUP = HPG * HEAD
HEAD_LOG2 = 6
BLK = 16
BLK_LOG2 = 4
assert 1 << HEAD_LOG2 == HEAD and 1 << BLK_LOG2 == BLK

SUBLANES = 8
MXU_W = 256
ADA_NB = 1536
VMEM_LIMIT = 56 * 1024 * 1024

NN = (((1,), (0,)), ((), ()))
NT = (((1,), (1,)), ((), ()))
TN = (((0,), (0,)), ((), ()))


def _silu(x):
    return x * jax.nn.sigmoid(x)


def _split(x):
    hi = x.astype(BF16)
    return hi, (x - hi.astype(F32)).astype(BF16)


def _dg(a, b, dims):
    return lax.dot_general(a, b, dims, preferred_element_type=F32)


def _mm3(a, b_hi, b_lo, dims):
    a_hi, a_lo = _split(a)
    return _dg(a_hi, b_hi, dims) + (_dg(a_hi, b_lo, dims) + _dg(a_lo, b_hi, dims))


def _rms(x):
    return x * lax.rsqrt(jnp.mean(x * x, axis=-1, keepdims=True) + RMS_EPS)


def _dot01(t, m01, left=False):
    hi, lo = _split(t)
    if left:
        return _dg(m01, hi, NN) + _dg(m01, lo, NN)
    return _dg(hi, m01, NN) + _dg(lo, m01, NN)


def _z_tile(za_ref, zb_ref, split, bi):
    if not split:
        return za_ref[bi]
    return jnp.where(pl.program_id(1) == 0, za_ref[bi], zb_ref[bi])


def _z_specs(split, bt, D):
    if split:
        return [pl.BlockSpec((bt, TM, D), lambda b, i: (b, 0, 0)),
                pl.BlockSpec((bt, TM, D), lambda b, i: (b, jnp.maximum(i - 1, 0), 0))]
    return [pl.BlockSpec((bt, TM, D), lambda b, i: (b, i, 0)),
            pl.BlockSpec((1, SUBLANES, D), lambda b, i: (0, 0, 0))]


def _rows_per_step(B):
    return BT if B % BT == 0 else 1


def _ada_kernel(c_ref, w_ref, b_ref, o_ref):
    s = _silu(c_ref[...])
    w_hi, w_lo = _split(w_ref[0])
    o_ref[0] = _mm3(s, w_hi, w_lo, NN) + b_ref[0]


def _ada(c_rows, ada_w, ada_b):
    L, D, N = ada_w.shape
    R = c_rows.shape[0]
    NB = ADA_NB
    return pl.pallas_call(
        _ada_kernel,
        grid=(L, N // NB),
        in_specs=[
            pl.BlockSpec((R, D), lambda l, n: (0, 0)),
            pl.BlockSpec((1, D, NB), lambda l, n: (l, 0, n)),
            pl.BlockSpec((1, 1, NB), lambda l, n: (l, 0, n)),
        ],
        out_specs=pl.BlockSpec((1, R, NB), lambda l, n: (l, 0, n)),
        out_shape=jax.ShapeDtypeStruct((L, R, N), F32),
        compiler_params=pltpu.CompilerParams(
            dimension_semantics=("arbitrary", "arbitrary"), vmem_limit_bytes=VMEM_LIMIT),
        name="ada",
    )(c_rows, ada_w, ada_b.reshape(L, 1, N))


def _proj_kernel(za_ref, zb_ref, m_ref, g_ref, win_ref, wup_ref, aup_ref, w0_ref, a0_ref, kk_ref, ka_ref,
                 rk_ref, ones_ref, tri_ref,
                 cb_ref, u_ref, v_ref, prod_ref, gl_ref, ops_ref, cl_ref,
                 *, split):
    base = 3 * D_CONV

    def head(bi):
        m = m_ref[bi, 0]
        h = _rms(_z_tile(za_ref, zb_ref, split, bi)) * g_ref[...]
        h = (h * (1.0 + m[1:2]) + m[0:1]).astype(BF16)

        def proj(lo, width):
            return jnp.dot(h, win_ref[:, lo:lo + width], preferred_element_type=F32)

        return (proj, proj(base + 3 * D_RWKV, LORA_WA + G_LORA), proj(base + D_RWKV, D_RWKV),
                proj(base, D_RWKV), proj(base + 2 * D_RWKV, D_RWKV))

    heads = [head(bi) for bi in range(cb_ref.shape[0])]
    for bi, (proj, lora, k, r, v) in enumerate(heads):
        wa = lora[:, :LORA_WA]
        gl = lora[:, LORA_WA:]
        ones_bd = ones_ref[...]

        kraw = k * kk_ref[...]
        kkn = kraw * lax.rsqrt(jnp.maximum(_dot01(kraw * kraw, ones_bd), 1e-24))
        lw2 = -DECAY_SCALE * jax.nn.sigmoid(
            w0_ref[...] + jnp.dot(jnp.tanh(wa).astype(BF16), wup_ref[...], preferred_element_type=F32))
        a2 = jax.nn.sigmoid(
            a0_ref[...] + jnp.dot(wa.astype(BF16), aup_ref[...], preferred_element_type=F32))

        cb_ref[bi] = proj(0, D_CONV)
        v_ref[bi] = v.astype(BF16)
        prod_ref[bi] = r * k * rk_ref[...]
        gl_ref[bi] = gl

        zero_rows = jnp.zeros((SUBLANES - CPT, D_RWKV), F32)
        conv_in = []
        for d in range(2):
            conv_in.append(proj((1 + d) * D_CONV, D_CONV))
            lw = lw2[:, d * D_RWKV:(d + 1) * D_RWKV]
            a = a2[:, d * D_RWKV:(d + 1) * D_RWKV]
            c = _dot01(lw, tri_ref[d], left=True)
            e_neg = jnp.exp(-c)
            ops = (kkn * jnp.exp(c - lw), r * jnp.exp(c), kkn * a * e_neg,
                   k * (1.0 + (a - 1.0) * ka_ref[...]) * e_neg)
            for j, x in enumerate(ops):
                ops_ref[d, bi, :, j * D_RWKV:(j + 1) * D_RWKV] = x.astype(BF16)
            last = CHUNK - 1 if d == 0 else 0
            cl_ref[d, bi, 0] = jnp.concatenate(
                [c[j * CHUNK + last:j * CHUNK + last + 1] for j in range(CPT)] + [zero_rows], axis=0)
        u_ref[bi] = conv_in[0] * conv_in[1]


def _proj(za, zb, split, modsel, g1, win, wup, aup, w0, a0, k_k, k_a, r_k, ones_bd, tri):
    B, D = za.shape[0], za.shape[2]
    T = za.shape[1] + zb.shape[1] if split else za.shape[1]
    NTL = T // TM
    P = win.shape[1]
    W = D_RWKV
    bt = _rows_per_step(B)
    tok = pl.BlockSpec((bt, TM, W), lambda b, i: (b, i, 0))
    tok2 = pl.BlockSpec((2, bt, TM, 4 * W), lambda b, i: (0, b, i, 0))
    const = lambda shape: pl.BlockSpec(shape, lambda b, i: (0,) * len(shape))
    f32_tok = jax.ShapeDtypeStruct((B, T, W), F32)
    bf_tok2 = jax.ShapeDtypeStruct((2, B, T, 4 * W), BF16)
    return pl.pallas_call(
        functools.partial(_proj_kernel, split=split),
        grid=(B // bt, NTL),
        in_specs=_z_specs(split, bt, D) + [
            pl.BlockSpec((bt, 1, SUBLANES, D), lambda b, i: (b, jnp.minimum(i, 1), 0, 0)),
            const((1, D)), const((D, P)), const((LORA_WA, 2 * W)), const((LORA_WA, 2 * W)),
            const((1, 2 * W)), const((1, 2 * W)), const((1, W)), const((1, W)),
            const((1, W)), const((W, W)), const((2, TM, TM)),
        ],
        out_specs=[tok, tok, tok, tok, pl.BlockSpec((bt, TM, G_LORA), lambda b, i: (b, i, 0)),
                   tok2,
                   pl.BlockSpec((2, bt, 1, SUBLANES, W), lambda b, i: (0, b, i, 0, 0))],
        out_shape=[f32_tok, f32_tok, jax.ShapeDtypeStruct((B, T, W), BF16), f32_tok,
                   jax.ShapeDtypeStruct((B, T, G_LORA), F32),
                   bf_tok2,
                   jax.ShapeDtypeStruct((2, B, NTL, SUBLANES, W), F32)],
        compiler_params=pltpu.CompilerParams(
            dimension_semantics=("parallel", "parallel"), vmem_limit_bytes=VMEM_LIMIT),
        name="proj",
    )(za, zb, modsel, g1, win, wup, aup, w0, a0, k_k, k_a, r_k, ones_bd, tri)


def _mm(a, x, dims):
    return _dg(a.astype(BF16), x.astype(BF16), dims)


def _scan_chunks(chains, bdmask):
    C = CHUNK
    row = lax.broadcasted_iota(jnp.int32, (C, GROUP), 0)
    col = lax.broadcasted_iota(jnp.int32, (C, GROUP), 1) & (HEAD - 1)
    eye = (row == col).astype(F32)
    blk16 = (row >> BLK_LOG2) == (col >> BLK_LOG2)
    lane_head = lax.broadcasted_iota(jnp.int32, (HEAD, GROUP), 1) >> HEAD_LOG2
    zero_bf = jnp.zeros((), BF16)

    def each(f, *lists):
        return [f(*a) for a in zip(*lists)]

    def bd(x):
        return jnp.where(bdmask, jnp.concatenate([x.astype(BF16)] * HPG, axis=0), zero_bf)

    def bd2(x, y, axis):
        return jnp.concatenate([bd(x), bd(y)], axis=axis)

    def rows(*xs):
        return jnp.concatenate([x.astype(BF16) for x in xs], axis=0)

    def diag_blocks(full):
        out = jnp.where(lane_head == 0, full[0:HEAD], 0.0)
        for hh in range(1, HPG):
            out = out + jnp.where(lane_head == hh, full[hh * HEAD:(hh + 1) * HEAD], 0.0)
        return out

    rev = [ch["rev"] for ch in chains]
    at, rt, bt, kt, v, p_c, s0 = ([ch[n] for ch in chains]
                                  for n in ("at", "rt", "bt", "kt", "v", "p_c", "s0"))
    strict = [(col > row) if rv else (col < row) for rv in rev]
    incl = [(col >= row) if rv else (col <= row) for rv in rev]
    bh = each(lambda x, p: x * p, bt, p_c)
    kh = each(lambda x, p: x * p, kt, p_c)

    abk = each(lambda a_, r_, x, y: _mm(rows(a_, r_), bd2(x, y, 0), NT), at, rt, bt, kt)
    lmat = each(lambda m_, x: jnp.where(m_, x[:C, :GROUP], 0.0), strict, abk)
    aak = each(lambda m_, x: jnp.where(m_, x[:C, GROUP:], 0.0), strict, abk)
    arb = each(lambda m_, x: jnp.where(m_, x[C:, :GROUP], 0.0), incl, abk)
    ark = each(lambda m_, x: jnp.where(m_, x[C:, GROUP:], 0.0), incl, abk)

    S = BLK
    lane16 = (lax.broadcasted_iota(jnp.int32, (S, GROUP), 1) & (HEAD - 1)) >> BLK_LOG2
    eye_p = (lax.broadcasted_iota(jnp.int32, (S, GROUP), 0)
             == (lax.broadcasted_iota(jnp.int32, (S, GROUP), 1) & (S - 1))).astype(F32)
    mask16 = (lax.broadcasted_iota(jnp.int32, (GROUP, GROUP), 0) >> BLK_LOG2) == \
             (lax.broadcasted_iota(jnp.int32, (GROUP, GROUP), 1) >> BLK_LOG2)

    def pack16(full):
        out = jnp.where(lane16 == 0, full[0:S], 0.0)
        for j in range(1, HEAD // S):
            out = out + jnp.where(lane16 == j, full[j * S:(j + 1) * S], 0.0)
        return out

    def unpack16(p):
        return jnp.concatenate([jnp.where(lane16 == j, p, 0.0) for j in range(HEAD // S)], axis=0)

    def bd16(p):
        return jnp.where(mask16, jnp.concatenate([p.astype(BF16)] * (GROUP // S), axis=0), zero_bf)

    ld = each(pack16, lmat)
    lo = each(lambda x: jnp.where(blk16, 0.0, x), lmat)
    p1 = each(lambda x: eye_p - x, ld)
    l2 = each(lambda x: _mm(x, bd16(x), NN), ld)
    t = each(lambda p, x: _mm(rows(p, x), bd16(x), NN), p1, l2)
    p2 = each(lambda p, x: p + x[:S], p1, t)
    l4 = each(lambda x: x[S:], t)
    t = each(lambda p, x: _mm(rows(p, x), bd16(x), NN), p2, l4)
    p3 = each(lambda p, x: p + x[:S], p2, t)
    l8 = each(lambda x: x[S:], t)
    dinv = each(lambda p, x: unpack16(p + _mm(p, bd16(x), NN)), p3, l8)
    mm = each(lambda d_, x: _mm(d_, bd(x), NN), dinv, lo)
    mm2 = each(lambda x: _mm(x, bd(x), NN), mm)
    g = each(lambda x, x2: eye - x + x2 - _mm(x, bd(x2), NN), mm, mm2)
    tinv = each(lambda g_, d_: _mm(g_, bd(d_), NN), g, dinv)

    t = each(lambda x, y, v_: _mm(rows(x, y), bd(v_), NN), aak, ark, v)
    akv = each(lambda x: x[:C], t)
    arkv = each(lambda x: x[C:], t)
    t = each(lambda t_, x, y: _mm(t_, bd2(x, y, 1), NN), tinv, at, akv)
    wm = each(lambda x: x[:, :GROUP], t)
    um = each(lambda x: x[:, GROUP:], t)
    t = each(lambda a_, x, y: _mm(a_, bd2(x, y, 1), NN), arb, wm, um)
    qh = each(lambda r_, x: r_.astype(F32) - x[:, :GROUP], rt, t)
    yl = each(lambda y_, x: y_ - x[:, GROUP:], arkv, t)
    t = each(lambda q_, w_, s_: _mm(rows(q_, w_), bd(s_), NT), qh, wm, s0)
    y = each(lambda x, y_: x[:C] + y_, t, yl)
    ds = each(lambda v_, u_, x, k_, b_: diag_blocks(_mm(rows(v_, u_, x[C:]), rows(k_, -b_, -b_), TN)),
              v, um, t, kh, bh)
    s_new = each(lambda s_, p_, d_: s_ * p_ + d_, s0, p_c, ds)
    return list(zip(y, s_new))


def _scan_kernel(vf_ref, vb_ref, opf_ref, clf_ref, opb_ref, clb_ref, yf_ref, yb_ref, s_ref,
                 *, nctx, n_chunks):
    s = pl.program_id(1)

    @pl.when(s == 0)
    def _():
        s_ref[...] = jnp.zeros_like(s_ref)

    back = jnp.where(s < nctx, nctx - 1 - s, nctx + n_chunks - 1 - s)
    bdmask = (lax.broadcasted_iota(jnp.int32, (GROUP, GROUP), 0) >> HEAD_LOG2) == \
             (lax.broadcasted_iota(jnp.int32, (GROUP, GROUP), 1) >> HEAD_LOG2)
    dirs = ((False, s, vf_ref, opf_ref, clf_ref, yf_ref), (True, back, vb_ref, opb_ref, clb_ref, yb_ref))
    chains, sinks = [], []
    for bi in range(s_ref.shape[0]):
        for d, (rev, chunk, v_ref, op_ref, cl_ref, y_ref) in enumerate(dirs):
            p_c = jnp.exp(cl_ref[0, bi, 0, pl.ds(chunk % CPT, 1), :])
            for q in range(D_RWKV // GROUP):
                sl = slice(q * GROUP, (q + 1) * GROUP)
                at, rt, bt, kt = (op_ref[0, bi, :, j * D_RWKV + q * GROUP:j * D_RWKV + (q + 1) * GROUP]
                                  for j in range(4))
                chains.append(dict(rev=rev, at=at, rt=rt, bt=bt, kt=kt,
                                   v=v_ref[bi, :, sl], p_c=p_c[:, sl], s0=s_ref[bi, d, :, sl]))
                sinks.append((y_ref, bi, d, sl))
    for (y, s_new), (y_ref, bi, d, sl) in zip(_scan_chunks(chains, bdmask), sinks):
        y_ref[bi, :, sl] = y
        s_ref[bi, d, :, sl] = s_new


def _scan(v, ops, cl, ctx_len):
    B, T, W = v.shape
    C = CHUNK
    NC = T // C
    nctx = ctx_len // C

    def back(s):
        return jnp.where(s < nctx, nctx - 1 - s, nctx + NC - 1 - s)

    BB = 4 if B % 4 == 0 else 2 if B % 2 == 0 else 1
    fwd = pl.BlockSpec((BB, C, W), lambda bb, s: (bb, s, 0))
    bwd = pl.BlockSpec((BB, C, W), lambda bb, s: (bb, back(s), 0))
    fwd_d = pl.BlockSpec((1, BB, C, 4 * W), lambda bb, s: (0, bb, s, 0))
    bwd_d = pl.BlockSpec((1, BB, C, 4 * W), lambda bb, s: (1, bb, back(s), 0))
    fwd_c = pl.BlockSpec((1, BB, 1, SUBLANES, W), lambda bb, s: (0, bb, s // CPT, 0, 0))
    bwd_c = pl.BlockSpec((1, BB, 1, SUBLANES, W), lambda bb, s: (1, bb, back(s) // CPT, 0, 0))
    out = jax.ShapeDtypeStruct((B, T, W), F32)
    return pl.pallas_call(
        functools.partial(_scan_kernel, nctx=nctx, n_chunks=NC),
        grid=(B // BB, NC),
        in_specs=[fwd, bwd, fwd_d, fwd_c, bwd_d, bwd_c],
        out_specs=[fwd, bwd],
        out_shape=[out, out],
        scratch_shapes=[pltpu.VMEM((BB, 2, HEAD, W), F32)],
        compiler_params=pltpu.CompilerParams(
            dimension_semantics=("parallel", "arbitrary"), vmem_limit_bytes=VMEM_LIMIT),
        name="scan",
    )(v, v, ops, cl, ops, cl)


def _mix_kernel(za_ref, zb_ref, m_ref, cb_ref, u_ref, up_ref, un_ref, yf_ref, yb_ref, prod_ref, v_ref,
                gl_ref, cw_ref, lng_ref, lnb_ref, ones_ref, gup_ref, wout_ref, x1_ref, *, n_tiles, split):
    i = pl.program_id(1)
    for bi in range(x1_ref.shape[0]):
        m = m_ref[bi, 0]
        prev_ok = (i >= 2).astype(F32)
        next_ok = jnp.logical_and(i >= 1, i <= n_tiles - 2).astype(F32)
        u = u_ref[bi]
        rowi = lax.broadcasted_iota(jnp.int32, (TM, 1), 0)
        um1 = jnp.where(rowi == 0, up_ref[bi][SUBLANES - 1:SUBLANES] * prev_ok, pltpu.roll(u, 1, 0))
        up1 = jnp.where(rowi == TM - 1, un_ref[bi][0:1] * next_ok, pltpu.roll(u, TM - 1, 0))
        cw = cw_ref[...]
        yconv = cb_ref[bi] * (cw[0:1] * um1 + cw[1:2] * u + cw[2:3] * up1)

        ones_bd = ones_ref[...]
        y = yf_ref[bi] + yb_ref[bi]
        mu = _dot01(y, ones_bd) * (1.0 / HEAD)
        dlt = y - mu
        var = _dot01(dlt * dlt, ones_bd) * (1.0 / HEAD)
        yn = dlt * lax.rsqrt(var + GN_EPS) * lng_ref[...] + lnb_ref[...]
        bonus = _dot01(prod_ref[bi], ones_bd) * v_ref[bi].astype(F32)
        gate = jnp.dot(jax.nn.sigmoid(gl_ref[bi]).astype(BF16), gup_ref[...], preferred_element_type=F32)
        yrw = (yn + bonus) * gate

        out = (jnp.dot(yconv.astype(BF16), wout_ref[0:D_CONV, :], preferred_element_type=F32)
               + jnp.dot(yrw.astype(BF16), wout_ref[D_CONV:D_CONV + D_RWKV, :], preferred_element_type=F32))
        x1_ref[bi] = _z_tile(za_ref, zb_ref, split, bi) + m[2:3] * out


def _mix(za, zb, split, modsel, cb, u, yf, yb, prod, v, gl, cw, lng, lnb, ones_bd, gup, wout):
    B, T, D = cb.shape[0], cb.shape[1], za.shape[2]
    NTL = T // TM
    bt = 2 * BT if B % (2 * BT) == 0 else _rows_per_step(B)
    tok = pl.BlockSpec((bt, TM, D_RWKV), lambda b, i: (b, i, 0))
    const = lambda shape: pl.BlockSpec(shape, lambda b, i: (0,) * len(shape))
    R8 = TM // SUBLANES
    return pl.pallas_call(
        functools.partial(_mix_kernel, n_tiles=NTL, split=split),
        grid=(B // bt, NTL),
        in_specs=_z_specs(split, bt, D) + [
            pl.BlockSpec((bt, 1, SUBLANES, D), lambda b, i: (b, jnp.minimum(i, 1), 0, 0)),
            tok, tok,
            pl.BlockSpec((bt, SUBLANES, D_CONV), lambda b, i: (b, jnp.maximum(i * R8 - 1, 0), 0)),
            pl.BlockSpec((bt, SUBLANES, D_CONV),
                         lambda b, i: (b, jnp.minimum((i + 1) * R8, T // SUBLANES - 1), 0)),
            tok, tok, tok, tok, pl.BlockSpec((bt, TM, G_LORA), lambda b, i: (b, i, 0)),
            const((SUBLANES, D_CONV)), const((1, D_RWKV)), const((1, D_RWKV)), const((D_RWKV, D_RWKV)),
            const((G_LORA, D_RWKV)), const((D_CONV + D_RWKV, D)),
        ],
        out_specs=pl.BlockSpec((bt, TM, D), lambda b, i: (b, i, 0)),
        out_shape=jax.ShapeDtypeStruct((B, T, D), F32),
        compiler_params=pltpu.CompilerParams(
            dimension_semantics=("parallel", "parallel"), vmem_limit_bytes=VMEM_LIMIT),
        name="mix",
    )(za, zb, modsel, cb, u, u, u, yf, yb, prod, v, gl, cw, lng, lnb, ones_bd, gup, wout)


def _ffn_kernel(x1_ref, xb_ref, m_ref, g2_ref, wup_ref, cw_ref, cbias_ref, wdn_ref, fg_ref,
                o_ref, gtop_ref, *, n_tiles, d_ff, final):
    i = pl.program_id(1)
    lat = (i >= 1).astype(F32)
    top_ok = i >= 2
    bot_ok = jnp.logical_and(i >= 1, i <= n_tiles - 2).astype(F32)
    rowi = lax.broadcasted_iota(jnp.int32, (TM, 1), 0)
    colp = jnp.where(i == 0, rowi, rowi & (GRID_W - 1))
    width = jnp.where(i == 0, TM, GRID_W)
    lmask = (colp > 0).astype(F32)
    rmask = (colp < width - 1).astype(F32)
    NB = MXU_W

    @pl.when(i == 0)
    def _():
        gtop_ref[...] = jnp.zeros_like(gtop_ref)

    for bi in range(x1_ref.shape[0]):
        m = m_ref[bi, 0]

        def norm2(x, keep):
            h = _rms(x) * g2_ref[...]
            return ((h * (1.0 + m[4:5]) + m[3:4]) * keep).astype(BF16)

        x1 = x1_ref[bi]
        hx = jnp.concatenate([norm2(x1, 1.0), norm2(xb_ref[bi], bot_ok)], axis=0)
        hc = hx[0:TM]

        def up(n):
            return (jnp.dot(hx, wup_ref[:, n:n + NB], preferred_element_type=F32),
                    jnp.dot(hc, wup_ref[:, d_ff + n:d_ff + n + NB], preferred_element_type=F32))

        acc = jnp.zeros((TM, o_ref.shape[-1]), F32)
        ahead = [up(j * NB) for j in range(UP_AHEAD)]
        prev = None
        for n in range(0, d_ff, NB):
            gfull, val = ahead.pop(0)
            if n + UP_AHEAD * NB < d_ff:
                ahead.append(up(n + UP_AHEAD * NB))
            if prev is not None:
                acc = acc + jnp.dot(prev, wdn_ref[n - NB:n, :], preferred_element_type=F32)
            g, gd = gfull[0:TM], gfull[GRID_W:GRID_W + TM]
            top = jnp.where(top_ok, gtop_ref[bi, :, n:n + NB], 0.0)
            gu = jnp.concatenate([top, g[0:TM - GRID_W]], axis=0)
            gtop_ref[bi, :, n:n + NB] = g[TM - GRID_W:TM]
            w = cw_ref[:, n:n + NB]
            wv = w * lat
            h0 = wv[0:1] * gu + w[3:4] * g + wv[6:7] * gd
            h1 = wv[1:2] * gu + w[4:5] * g + wv[7:8] * gd
            h2 = wv[2:3] * gu + w[5:6] * g + wv[8:9] * gd
            conv = (lmask * pltpu.roll(h0, 1, 0) + h1 + rmask * pltpu.roll(h2, TM - 1, 0)
                    + cbias_ref[:, n:n + NB])
            prev = (_silu(conv) * val).astype(BF16)
        acc = acc + jnp.dot(prev, wdn_ref[d_ff - NB:d_ff, :], preferred_element_type=F32)
        x2 = x1 + m[5:6] * acc
        if final:
            x2 = _rms(x2) * fg_ref[...]
        o_ref[bi] = x2


def _ffn(x1, modsel, g2, wup, cw, cbias, wdn, fg, final):
    B, T, D = x1.shape
    NTL = T // TM
    d_ff = wdn.shape[0]
    RW = TM // GRID_W
    bt = 1
    const = lambda shape: pl.BlockSpec(shape, lambda b, i: (0,) * len(shape))
    if final:
        out_spec = pl.BlockSpec((bt, TM, D), lambda b, i: (b, jnp.maximum(i - 1, 0), 0))
        out_shape = jax.ShapeDtypeStruct((B, T - TM, D), F32)
    else:
        out_spec = pl.BlockSpec((bt, TM, D), lambda b, i: (b, i, 0))
        out_shape = jax.ShapeDtypeStruct((B, T, D), F32)
    return pl.pallas_call(
        functools.partial(_ffn_kernel, n_tiles=NTL, d_ff=d_ff, final=final),
        grid=(B // bt, NTL),
        in_specs=[
            pl.BlockSpec((bt, TM, D), lambda b, i: (b, i, 0)),
            pl.BlockSpec((bt, GRID_W, D),
                         lambda b, i: (b, jnp.minimum((i + 1) * RW, T // GRID_W - 1), 0)),
            pl.BlockSpec((bt, 1, SUBLANES, D), lambda b, i: (b, jnp.minimum(i, 1), 0, 0)),
            const((1, D)), const((D, 2 * d_ff)),
            const((2 * SUBLANES, d_ff)), const((1, d_ff)), const((d_ff, D)), const((1, D)),
        ],
        out_specs=out_spec,
        out_shape=out_shape,
        scratch_shapes=[pltpu.VMEM((bt, GRID_W, d_ff), F32)],
        compiler_params=pltpu.CompilerParams(
            dimension_semantics=("parallel", "arbitrary"), vmem_limit_bytes=VMEM_LIMIT),
        name="ffn",
    )(x1, x1, modsel, g2, wup, cw, cbias, wdn, fg)


def kernel(x, c, ctx, c_ctx, ada_w, ada_b, norm1_g, norm2_g, w_in, conv_a_w, rw_w0, rw_w_up, rw_a0,
           rw_a_up, rw_k_k, rw_k_a, rw_r_k, rw_g_up, rw_ln_g, rw_ln_b, w_out, ffn_w_up, ffn_conv_w,
           ffn_conv_b, ffn_w_down, final_g):
    B, SEQ, D = x.shape
    CTX = ctx.shape[1]
    L = w_in.shape[0]
    d_ff = ffn_w_down.shape[1]
    assert CTX == TM and SEQ % TM == 0 and TM % GRID_W == 0 and CHUNK == HEAD and CPT <= SUBLANES
    assert w_in.shape[2] == 3 * D_CONV + 3 * D_RWKV + LORA_WA + G_LORA and d_ff % MXU_W == 0

    rows = -(-(B + 1) // SUBLANES) * SUBLANES
    c_rows = jnp.zeros((rows, D), F32).at[:B].set(c).at[B].set(c_ctx)
    mod = _ada(c_rows, ada_w, ada_b)

    hw = LORA_WA // 2
    zpad = jnp.zeros((L, hw, 2 * D_RWKV), F32)
    both = lambda t: jnp.concatenate([t[:, 0], t[:, 1]], axis=-1)
    wup_ext = jnp.concatenate([both(rw_w_up), zpad], axis=1).astype(BF16)
    aup_ext = jnp.concatenate([zpad, both(rw_a_up)], axis=1).astype(BF16)
    w0_2 = both(rw_w0[:, :, None, :])
    a0_2 = both(rw_a0[:, :, None, :])
    head_of = jnp.arange(D_RWKV) // HEAD
    ones_bd = (head_of[:, None] == head_of[None, :]).astype(BF16)
    tt = jnp.arange(TM)
    same = (tt[:, None] // CHUNK) == (tt[None, :] // CHUNK)
    tri = jnp.stack([same & (tt[None, :] <= tt[:, None]),
                     same & (tt[None, :] >= tt[:, None])]).astype(BF16)
    cw_a = jnp.zeros((L, SUBLANES, D_CONV), F32).at[:, :3].set(conv_a_w)
    cw_f = jnp.zeros((L, 2 * SUBLANES, d_ff), F32).at[:, :9].set(ffn_conv_w.reshape(L, 9, d_ff))

    za, zb = ctx, x
    for l in range(L):
        split = l == 0
        lat = mod[l, :B].reshape(B, 6, D)
        cm = jnp.broadcast_to(mod[l, B].reshape(1, 6, D), (B, 6, D))
        modsel = jnp.zeros((B, 2, SUBLANES, D), F32).at[:, 0, :6].set(cm).at[:, 1, :6].set(lat)

        cb, u, v, prod, gl, ops, cl = _proj(
            za, zb, split, modsel, norm1_g[l].reshape(1, D), w_in[l].astype(BF16), wup_ext[l], aup_ext[l],
            w0_2[l], a0_2[l], rw_k_k[l].reshape(1, -1), rw_k_a[l].reshape(1, -1),
            rw_r_k[l].reshape(1, -1), ones_bd, tri)
        yf, yb = _scan(v, ops, cl, CTX)
        x1 = _mix(
            za, zb, split, modsel, cb, u, yf, yb, prod, v, gl, cw_a[l], rw_ln_g[l].reshape(1, -1),
            rw_ln_b[l].reshape(1, -1), ones_bd, rw_g_up[l].astype(BF16), w_out[l].astype(BF16))
        za = zb = _ffn(x1, modsel, norm2_g[l].reshape(1, D), ffn_w_up[l].astype(BF16), cw_f[l],
                       ffn_conv_b[l].reshape(1, -1), ffn_w_down[l].astype(BF16), final_g.reshape(1, D),
                       final=(l == L - 1))
    return za
```

```python
import functools
import math

import jax
import jax.numpy as jnp
from jax import lax
from jax.experimental import pallas as pl
from jax.experimental.pallas import tpu as pltpu

F32 = jnp.float32
BF16 = jnp.bfloat16

HEAD = 64
D_CONV = 512
D_RWKV = 512
LORA_WA = 128
G_LORA = 128
GRID_W = 64
RMS_EPS = 1e-6
GN_EPS = 64e-5
DECAY_SCALE = math.exp(-0.5)

TM = 256
CHUNK = 64
CPT = TM // CHUNK
BT = 2
UP_AHEAD = 6
HPG = 2
GROUP = HPG * HEAD
HEAD_LOG2 = 6
BLK = 16
BLK_LOG2 = 4
assert 1 << HEAD_LOG2 == HEAD and 1 << BLK_LOG2 == BLK

SUBLANES = 8
MXU_W = 256
ADA_NB = 1536
VMEM_LIMIT = 56 * 1024 * 1024

NN = (((1,), (0,)), ((), ()))
NT = (((1,), (1,)), ((), ()))
TN = (((0,), (0,)), ((), ()))


def _silu(x):
    return x * jax.nn.sigmoid(x)


def _split(x):
    hi = x.astype(BF16)
    return hi, (x - hi.astype(F32)).astype(BF16)


def _dg(a, b, dims):
    return lax.dot_general(a, b, dims, preferred_element_type=F32)


def _mm3(a, b_hi, b_lo, dims):
    a_hi, a_lo = _split(a)
    return _dg(a_hi, b_hi, dims) + (_dg(a_hi, b_lo, dims) + _dg(a_lo, b_hi, dims))


def _rms(x):
    return x * lax.rsqrt(jnp.mean(x * x, axis=-1, keepdims=True) + RMS_EPS)


def _dot01(t, m01, left=False):
    hi, lo = _split(t)
    if left:
        return _dg(m01, hi, NN) + _dg(m01, lo, NN)
    return _dg(hi, m01, NN) + _dg(lo, m01, NN)


def _z_tile(za_ref, zb_ref, split, bi):
    if not split:
        return za_ref[bi]
    return jnp.where(pl.program_id(1) == 0, za_ref[bi], zb_ref[bi])


def _z_specs(split, bt, D):
    if split:
        return [pl.BlockSpec((bt, TM, D), lambda b, i: (b, 0, 0)),
                pl.BlockSpec((bt, TM, D), lambda b, i: (b, jnp.maximum(i - 1, 0), 0))]
    return [pl.BlockSpec((bt, TM, D), lambda b, i: (b, i, 0)),
            pl.BlockSpec((1, SUBLANES, D), lambda b, i: (0, 0, 0))]


def _const_spec(shape):
    return pl.BlockSpec(shape, lambda b, i: (0,) * len(shape), pipeline_mode=pl.Buffered(1))


def _rows_per_step(B):
    return BT if B % BT == 0 else 1


def _ada_kernel(c_ref, w_ref, b_ref, o_ref):
    s = _silu(c_ref[...])
    w_hi, w_lo = _split(w_ref[0])
    o_ref[0] = _mm3(s, w_hi, w_lo, NN) + b_ref[0]


def _ada(c_rows, ada_w, ada_b):
    L, D, N = ada_w.shape
    R = c_rows.shape[0]
    NB = ADA_NB
    return pl.pallas_call(
        _ada_kernel,
        grid=(L, N // NB),
        in_specs=[
            pl.BlockSpec((R, D), lambda l, n: (0, 0)),
            pl.BlockSpec((1, D, NB), lambda l, n: (l, 0, n)),
            pl.BlockSpec((1, 1, NB), lambda l, n: (l, 0, n)),
        ],
        out_specs=pl.BlockSpec((1, R, NB), lambda l, n: (l, 0, n)),
        out_shape=jax.ShapeDtypeStruct((L, R, N), F32),
        compiler_params=pltpu.CompilerParams(
            dimension_semantics=("arbitrary", "arbitrary"), vmem_limit_bytes=VMEM_LIMIT),
        name="ada",
    )(c_rows, ada_w, ada_b.reshape(L, 1, N))


def _proj_kernel(za_ref, zb_ref, m_ref, g_ref, win_ref, wup_ref, aup_ref, w0_ref, a0_ref, kk_ref, ka_ref,
                 rk_ref, ones_ref, tri_ref,
                 cb_ref, u_ref, v_ref, prod_ref, gl_ref, ops_ref, cl_ref,
                 *, split):
    base = 3 * D_CONV

    def head(bi):
        m = m_ref[bi, 0]
        h = _rms(_z_tile(za_ref, zb_ref, split, bi)) * g_ref[...]
        h = (h * (1.0 + m[1:2]) + m[0:1]).astype(BF16)

        def proj(lo, width):
            return jnp.dot(h, win_ref[:, lo:lo + width], preferred_element_type=F32)

        return (proj, proj(base + 3 * D_RWKV, LORA_WA + G_LORA), proj(base + D_RWKV, D_RWKV),
                proj(base, D_RWKV), proj(base + 2 * D_RWKV, D_RWKV))

    heads = [head(bi) for bi in range(cb_ref.shape[0])]
    for bi, (proj, lora, k, r, v) in enumerate(heads):
        wa = lora[:, :LORA_WA]
        gl = lora[:, LORA_WA:]
        ones_bd = ones_ref[...]

        kraw = k * kk_ref[...]
        kkn = kraw * lax.rsqrt(jnp.maximum(_dot01(kraw * kraw, ones_bd), 1e-24))
        lw2 = -DECAY_SCALE * jax.nn.sigmoid(
            w0_ref[...] + jnp.dot(jnp.tanh(wa).astype(BF16), wup_ref[...], preferred_element_type=F32))
        a2 = jax.nn.sigmoid(
            a0_ref[...] + jnp.dot(wa.astype(BF16), aup_ref[...], preferred_element_type=F32))

        cb_ref[bi] = proj(0, D_CONV)
        v_ref[bi] = v.astype(BF16)
        prod_ref[bi] = r * k * rk_ref[...]
        gl_ref[bi] = gl

        zero_rows = jnp.zeros((SUBLANES - CPT, D_RWKV), F32)
        conv_in = []
        for d in range(2):
            conv_in.append(proj((1 + d) * D_CONV, D_CONV))
            lw = lw2[:, d * D_RWKV:(d + 1) * D_RWKV]
            a = a2[:, d * D_RWKV:(d + 1) * D_RWKV]
            c = _dot01(lw, tri_ref[d], left=True)
            e_neg = jnp.exp(-c)
            ops = (kkn * jnp.exp(c - lw), r * jnp.exp(c), kkn * a * e_neg,
                   k * (1.0 + (a - 1.0) * ka_ref[...]) * e_neg)
            for j, x in enumerate(ops):
                ops_ref[d, bi, :, j * D_RWKV:(j + 1) * D_RWKV] = x.astype(BF16)
            last = CHUNK - 1 if d == 0 else 0
            cl_ref[d, bi, 0] = jnp.concatenate(
                [c[j * CHUNK + last:j * CHUNK + last + 1] for j in range(CPT)] + [zero_rows], axis=0)
        u_ref[bi] = conv_in[0] * conv_in[1]


def _proj(za, zb, split, modsel, g1, win, wup, aup, w0, a0, k_k, k_a, r_k, ones_bd, tri):
    B, D = za.shape[0], za.shape[2]
    T = za.shape[1] + zb.shape[1] if split else za.shape[1]
    NTL = T // TM
    P = win.shape[1]
    W = D_RWKV
    bt = _rows_per_step(B)
    tok = pl.BlockSpec((bt, TM, W), lambda b, i: (b, i, 0))
    tok2 = pl.BlockSpec((2, bt, TM, 4 * W), lambda b, i: (0, b, i, 0))
    const = _const_spec
    f32_tok = jax.ShapeDtypeStruct((B, T, W), F32)
    bf_tok2 = jax.ShapeDtypeStruct((2, B, T, 4 * W), BF16)
    return pl.pallas_call(
        functools.partial(_proj_kernel, split=split),
        grid=(B // bt, NTL),
        in_specs=_z_specs(split, bt, D) + [
            pl.BlockSpec((bt, 1, SUBLANES, D), lambda b, i: (b, jnp.minimum(i, 1), 0, 0)),
            const((1, D)), const((D, P)), const((LORA_WA, 2 * W)), const((LORA_WA, 2 * W)),
            const((1, 2 * W)), const((1, 2 * W)), const((1, W)), const((1, W)),
            const((1, W)), const((W, W)), const((2, TM, TM)),
        ],
        out_specs=[tok, tok, tok, tok, pl.BlockSpec((bt, TM, G_LORA), lambda b, i: (b, i, 0)),
                   tok2,
                   pl.BlockSpec((2, bt, 1, SUBLANES, W), lambda b, i: (0, b, i, 0, 0))],
        out_shape=[f32_tok, f32_tok, jax.ShapeDtypeStruct((B, T, W), BF16), f32_tok,
                   jax.ShapeDtypeStruct((B, T, G_LORA), F32),
                   bf_tok2,
                   jax.ShapeDtypeStruct((2, B, NTL, SUBLANES, W), F32)],
        compiler_params=pltpu.CompilerParams(
            dimension_semantics=("parallel", "parallel"), vmem_limit_bytes=VMEM_LIMIT),
        name="proj",
    )(za, zb, modsel, g1, win, wup, aup, w0, a0, k_k, k_a, r_k, ones_bd, tri)


def _mm(a, x, dims):
    return _dg(a.astype(BF16), x.astype(BF16), dims)


def _scan_chunks(chains, bdmask):
    C = CHUNK
    row = lax.broadcasted_iota(jnp.int32, (C, GROUP), 0)
    col = lax.broadcasted_iota(jnp.int32, (C, GROUP), 1) & (HEAD - 1)
    eye = (row == col).astype(F32)
    blk16 = (row >> BLK_LOG2) == (col >> BLK_LOG2)
    lane_head = lax.broadcasted_iota(jnp.int32, (HEAD, GROUP), 1) >> HEAD_LOG2
    zero_bf = jnp.zeros((), BF16)

    def each(f, *lists):
        return [f(*a) for a in zip(*lists)]

    def bd(x):
        return jnp.where(bdmask, jnp.concatenate([x.astype(BF16)] * HPG, axis=0), zero_bf)

    def bd2(x, y, axis):
        return jnp.concatenate([bd(x), bd(y)], axis=axis)

    def rows(*xs):
        return jnp.concatenate([x.astype(BF16) for x in xs], axis=0)

    def diag_blocks(full):
        out = jnp.where(lane_head == 0, full[0:HEAD], 0.0)
        for hh in range(1, HPG):
            out = out + jnp.where(lane_head == hh, full[hh * HEAD:(hh + 1) * HEAD], 0.0)
        return out

    rev = [ch["rev"] for ch in chains]
    at, rt, bt, kt, v, p_c, s0 = ([ch[n] for ch in chains]
                                  for n in ("at", "rt", "bt", "kt", "v", "p_c", "s0"))
    strict = [(col > row) if rv else (col < row) for rv in rev]
    incl = [(col >= row) if rv else (col <= row) for rv in rev]
    bh = each(lambda x, p: x * p, bt, p_c)
    kh = each(lambda x, p: x * p, kt, p_c)

    abk = each(lambda a_, r_, x, y: _mm(rows(a_, r_), bd2(x, y, 0), NT), at, rt, bt, kt)
    lmat = each(lambda m_, x: jnp.where(m_, x[:C, :GROUP], 0.0), strict, abk)
    aak = each(lambda m_, x: jnp.where(m_, x[:C, GROUP:], 0.0), strict, abk)
    arb = each(lambda m_, x: jnp.where(m_, x[C:, :GROUP], 0.0), incl, abk)
    ark = each(lambda m_, x: jnp.where(m_, x[C:, GROUP:], 0.0), incl, abk)

    S = BLK
    lane16 = (lax.broadcasted_iota(jnp.int32, (S, GROUP), 1) & (HEAD - 1)) >> BLK_LOG2
    eye_p = (lax.broadcasted_iota(jnp.int32, (S, GROUP), 0)
             == (lax.broadcasted_iota(jnp.int32, (S, GROUP), 1) & (S - 1))).astype(F32)
    mask16 = (lax.broadcasted_iota(jnp.int32, (GROUP, GROUP), 0) >> BLK_LOG2) == \
             (lax.broadcasted_iota(jnp.int32, (GROUP, GROUP), 1) >> BLK_LOG2)

    def pack16(full):
        out = jnp.where(lane16 == 0, full[0:S], 0.0)
        for j in range(1, HEAD // S):
            out = out + jnp.where(lane16 == j, full[j * S:(j + 1) * S], 0.0)
        return out

    def unpack16(p):
        return jnp.concatenate([jnp.where(lane16 == j, p, 0.0) for j in range(HEAD // S)], axis=0)

    def bd16(p):
        return jnp.where(mask16, jnp.concatenate([p.astype(BF16)] * (GROUP // S), axis=0), zero_bf)

    ld = each(pack16, lmat)
    lo = each(lambda x: jnp.where(blk16, 0.0, x), lmat)
    p1 = each(lambda x: eye_p - x, ld)
    l2 = each(lambda x: _mm(x, bd16(x), NN), ld)
    t = each(lambda p, x: _mm(rows(p, x), bd16(x), NN), p1, l2)
    p2 = each(lambda p, x: p + x[:S], p1, t)
    l4 = each(lambda x: x[S:], t)
    t = each(lambda p, x: _mm(rows(p, x), bd16(x), NN), p2, l4)
    p3 = each(lambda p, x: p + x[:S], p2, t)
    l8 = each(lambda x: x[S:], t)
    dinv = each(lambda p, x: unpack16(p + _mm(p, bd16(x), NN)), p3, l8)
    mm = each(lambda d_, x: _mm(d_, bd(x), NN), dinv, lo)
    mm2 = each(lambda x: _mm(x, bd(x), NN), mm)
    g = each(lambda x, x2: eye - x + x2 - _mm(x, bd(x2), NN), mm, mm2)
    tinv = each(lambda g_, d_: _mm(g_, bd(d_), NN), g, dinv)

    t = each(lambda x, y, v_: _mm(rows(x, y), bd(v_), NN), aak, ark, v)
    akv = each(lambda x: x[:C], t)
    arkv = each(lambda x: x[C:], t)
    t = each(lambda t_, x, y: _mm(t_, bd2(x, y, 1), NN), tinv, at, akv)
    wm = each(lambda x: x[:, :GROUP], t)
    um = each(lambda x: x[:, GROUP:], t)
    t = each(lambda a_, x, y: _mm(a_, bd2(x, y, 1), NN), arb, wm, um)
    qh = each(lambda r_, x: r_.astype(F32) - x[:, :GROUP], rt, t)
    yl = each(lambda y_, x: y_ - x[:, GROUP:], arkv, t)
    t = each(lambda q_, w_, s_: _mm(rows(q_, w_), bd(s_), NT), qh, wm, s0)
    y = each(lambda x, y_: x[:C] + y_, t, yl)
    ds = each(lambda v_, u_, x, k_, b_: diag_blocks(_mm(rows(v_, u_, x[C:]), rows(k_, -b_, -b_), TN)),
              v, um, t, kh, bh)
    s_new = each(lambda s_, p_, d_: s_ * p_ + d_, s0, p_c, ds)
    return list(zip(y, s_new))


def _scan_kernel(vf_ref, vb_ref, opf_ref, clf_ref, opb_ref, clb_ref, yf_ref, yb_ref, s_ref,
                 *, nctx, n_chunks):
    s = pl.program_id(1)

    @pl.when(s == 0)
    def _():
        s_ref[...] = jnp.zeros_like(s_ref)

    back = jnp.where(s < nctx, nctx - 1 - s, nctx + n_chunks - 1 - s)
    bdmask = (lax.broadcasted_iota(jnp.int32, (GROUP, GROUP), 0) >> HEAD_LOG2) == \
             (lax.broadcasted_iota(jnp.int32, (GROUP, GROUP), 1) >> HEAD_LOG2)
    dirs = ((False, s, vf_ref, opf_ref, clf_ref, yf_ref), (True, back, vb_ref, opb_ref, clb_ref, yb_ref))
    chains, sinks = [], []
    for bi in range(s_ref.shape[0]):
        for d, (rev, chunk, v_ref, op_ref, cl_ref, y_ref) in enumerate(dirs):
            p_c = jnp.exp(cl_ref[0, bi, 0, pl.ds(chunk % CPT, 1), :])
            for q in range(D_RWKV // GROUP):
                sl = slice(q * GROUP, (q + 1) * GROUP)
                at, rt, bt, kt = (op_ref[0, bi, :, j * D_RWKV + q * GROUP:j * D_RWKV + (q + 1) * GROUP]
                                  for j in range(4))
                chains.append(dict(rev=rev, at=at, rt=rt, bt=bt, kt=kt,
                                   v=v_ref[bi, :, sl], p_c=p_c[:, sl], s0=s_ref[bi, d, :, sl]))
                sinks.append((y_ref, bi, d, sl))
    for (y, s_new), (y_ref, bi, d, sl) in zip(_scan_chunks(chains, bdmask), sinks):
        y_ref[bi, :, sl] = y
        s_ref[bi, d, :, sl] = s_new


def _scan(v, ops, cl, ctx_len):
    B, T, W = v.shape
    C = CHUNK
    NC = T // C
    nctx = ctx_len // C

    def back(s):
        return jnp.where(s < nctx, nctx - 1 - s, nctx + NC - 1 - s)

    BB = 4 if B % 4 == 0 else 2 if B % 2 == 0 else 1
    fwd = pl.BlockSpec((BB, C, W), lambda bb, s: (bb, s, 0))
    bwd = pl.BlockSpec((BB, C, W), lambda bb, s: (bb, back(s), 0))
    fwd_d = pl.BlockSpec((1, BB, C, 4 * W), lambda bb, s: (0, bb, s, 0))
    bwd_d = pl.BlockSpec((1, BB, C, 4 * W), lambda bb, s: (1, bb, back(s), 0))
    fwd_c = pl.BlockSpec((1, BB, 1, SUBLANES, W), lambda bb, s: (0, bb, s // CPT, 0, 0))
    bwd_c = pl.BlockSpec((1, BB, 1, SUBLANES, W), lambda bb, s: (1, bb, back(s) // CPT, 0, 0))
    out = jax.ShapeDtypeStruct((B, T, W), F32)
    return pl.pallas_call(
        functools.partial(_scan_kernel, nctx=nctx, n_chunks=NC),
        grid=(B // BB, NC),
        in_specs=[fwd, bwd, fwd_d, fwd_c, bwd_d, bwd_c],
        out_specs=[fwd, bwd],
        out_shape=[out, out],
        scratch_shapes=[pltpu.VMEM((BB, 2, HEAD, W), F32)],
        compiler_params=pltpu.CompilerParams(
            dimension_semantics=("parallel", "arbitrary"), vmem_limit_bytes=VMEM_LIMIT),
        name="scan",
    )(v, v, ops, cl, ops, cl)


def _mix_kernel(za_ref, zb_ref, m_ref, cb_ref, u_ref, up_ref, un_ref, yf_ref, yb_ref, prod_ref, v_ref,
                gl_ref, cw_ref, lng_ref, lnb_ref, ones_ref, gup_ref, wout_ref, x1_ref, *, n_tiles, split):
    i = pl.program_id(1)
    for bi in range(x1_ref.shape[0]):
        m = m_ref[bi, 0]
        prev_ok = (i >= 2).astype(F32)
        next_ok = jnp.logical_and(i >= 1, i <= n_tiles - 2).astype(F32)
        u = u_ref[bi]
        rowi = lax.broadcasted_iota(jnp.int32, (TM, 1), 0)
        um1 = jnp.where(rowi == 0, up_ref[bi][SUBLANES - 1:SUBLANES] * prev_ok, pltpu.roll(u, 1, 0))
        up1 = jnp.where(rowi == TM - 1, un_ref[bi][0:1] * next_ok, pltpu.roll(u, TM - 1, 0))
        cw = cw_ref[...]
        yconv = cb_ref[bi] * (cw[0:1] * um1 + cw[1:2] * u + cw[2:3] * up1)

        ones_bd = ones_ref[...]
        y = yf_ref[bi] + yb_ref[bi]
        mu = _dot01(y, ones_bd) * (1.0 / HEAD)
        dlt = y - mu
        var = _dot01(dlt * dlt, ones_bd) * (1.0 / HEAD)
        yn = dlt * lax.rsqrt(var + GN_EPS) * lng_ref[...] + lnb_ref[...]
        bonus = _dot01(prod_ref[bi], ones_bd) * v_ref[bi].astype(F32)
        gate = jnp.dot(jax.nn.sigmoid(gl_ref[bi]).astype(BF16), gup_ref[...], preferred_element_type=F32)
        yrw = (yn + bonus) * gate

        out = (jnp.dot(yconv.astype(BF16), wout_ref[0:D_CONV, :], preferred_element_type=F32)
               + jnp.dot(yrw.astype(BF16), wout_ref[D_CONV:D_CONV + D_RWKV, :], preferred_element_type=F32))
        x1_ref[bi] = _z_tile(za_ref, zb_ref, split, bi) + m[2:3] * out


def _mix(za, zb, split, modsel, cb, u, yf, yb, prod, v, gl, cw, lng, lnb, ones_bd, gup, wout):
    B, T, D = cb.shape[0], cb.shape[1], za.shape[2]
    NTL = T // TM
    bt = 2 * BT if B % (2 * BT) == 0 else _rows_per_step(B)
    tok = pl.BlockSpec((bt, TM, D_RWKV), lambda b, i: (b, i, 0))
    const = _const_spec
    R8 = TM // SUBLANES
    return pl.pallas_call(
        functools.partial(_mix_kernel, n_tiles=NTL, split=split),
        grid=(B // bt, NTL),
        in_specs=_z_specs(split, bt, D) + [
            pl.BlockSpec((bt, 1, SUBLANES, D), lambda b, i: (b, jnp.minimum(i, 1), 0, 0)),
            tok, tok,
            pl.BlockSpec((bt, SUBLANES, D_CONV), lambda b, i: (b, jnp.maximum(i * R8 - 1, 0), 0)),
            pl.BlockSpec((bt, SUBLANES, D_CONV),
                         lambda b, i: (b, jnp.minimum((i + 1) * R8, T // SUBLANES - 1), 0)),
            tok, tok, tok, tok, pl.BlockSpec((bt, TM, G_LORA), lambda b, i: (b, i, 0)),
            const((SUBLANES, D_CONV)), const((1, D_RWKV)), const((1, D_RWKV)), const((D_RWKV, D_RWKV)),
            const((G_LORA, D_RWKV)), const((D_CONV + D_RWKV, D)),
        ],
        out_specs=pl.BlockSpec((bt, TM, D), lambda b, i: (b, i, 0)),
        out_shape=jax.ShapeDtypeStruct((B, T, D), F32),
        compiler_params=pltpu.CompilerParams(
            dimension_semantics=("parallel", "parallel"), vmem_limit_bytes=VMEM_LIMIT),
        name="mix",
    )(za, zb, modsel, cb, u, u, u, yf, yb, prod, v, gl, cw, lng, lnb, ones_bd, gup, wout)


def _ffn_kernel(x1_ref, xb_ref, m_ref, g2_ref, wup_ref, cw_ref, cbias_ref, wdn_ref, fg_ref,
                o_ref, gtop_ref, *, n_tiles, d_ff, final):
    i = pl.program_id(1)
    lat = (i >= 1).astype(F32)
    top_ok = i >= 2
    bot_ok = jnp.logical_and(i >= 1, i <= n_tiles - 2).astype(F32)
    rowi = lax.broadcasted_iota(jnp.int32, (TM, 1), 0)
    colp = jnp.where(i == 0, rowi, rowi & (GRID_W - 1))
    width = jnp.where(i == 0, TM, GRID_W)
    lmask = (colp > 0).astype(F32)
    rmask = (colp < width - 1).astype(F32)
    NB = MXU_W

    @pl.when(i == 0)
    def _():
        gtop_ref[...] = jnp.zeros_like(gtop_ref)

    for bi in range(x1_ref.shape[0]):
        m = m_ref[bi, 0]

        def norm2(x, keep):
            h = _rms(x) * g2_ref[...]
            return ((h * (1.0 + m[4:5]) + m[3:4]) * keep).astype(BF16)

        x1 = x1_ref[bi]
        hx = jnp.concatenate([norm2(x1, 1.0), norm2(xb_ref[bi], bot_ok)], axis=0)
        hc = hx[0:TM]

        def up(n):
            return (jnp.dot(hx, wup_ref[:, n:n + NB], preferred_element_type=F32),
                    jnp.dot(hc, wup_ref[:, d_ff + n:d_ff + n + NB], preferred_element_type=F32))

        acc = jnp.zeros((TM, o_ref.shape[-1]), F32)
        ahead = [up(j * NB) for j in range(UP_AHEAD)]
        prev = None
        for n in range(0, d_ff, NB):
            gfull, val = ahead.pop(0)
            if n + UP_AHEAD * NB < d_ff:
                ahead.append(up(n + UP_AHEAD * NB))
            if prev is not None:
                acc = acc + jnp.dot(prev, wdn_ref[n - NB:n, :], preferred_element_type=F32)
            g, gd = gfull[0:TM], gfull[GRID_W:GRID_W + TM]
            top = jnp.where(top_ok, gtop_ref[bi, :, n:n + NB], 0.0)
            gu = jnp.concatenate([top, g[0:TM - GRID_W]], axis=0)
            gtop_ref[bi, :, n:n + NB] = g[TM - GRID_W:TM]
            w = cw_ref[:, n:n + NB]
            wv = w * lat
            h0 = wv[0:1] * gu + w[3:4] * g + wv[6:7] * gd
            h1 = wv[1:2] * gu + w[4:5] * g + wv[7:8] * gd
            h2 = wv[2:3] * gu + w[5:6] * g + wv[8:9] * gd
            conv = (lmask * pltpu.roll(h0, 1, 0) + h1 + rmask * pltpu.roll(h2, TM - 1, 0)
                    + cbias_ref[:, n:n + NB])
            prev = (_silu(conv) * val).astype(BF16)
        acc = acc + jnp.dot(prev, wdn_ref[d_ff - NB:d_ff, :], preferred_element_type=F32)
        x2 = x1 + m[5:6] * acc
        if final:
            x2 = _rms(x2) * fg_ref[...]
        o_ref[bi] = x2


def _ffn(x1, modsel, g2, wup, cw, cbias, wdn, fg, final):
    B, T, D = x1.shape
    NTL = T // TM
    d_ff = wdn.shape[0]
    RW = TM // GRID_W
    bt = 1
    const = _const_spec
    if final:
        out_spec = pl.BlockSpec((bt, TM, D), lambda b, i: (b, jnp.maximum(i - 1, 0), 0))
        out_shape = jax.ShapeDtypeStruct((B, T - TM, D), F32)
    else:
        out_spec = pl.BlockSpec((bt, TM, D), lambda b, i: (b, i, 0))
        out_shape = jax.ShapeDtypeStruct((B, T, D), F32)
    return pl.pallas_call(
        functools.partial(_ffn_kernel, n_tiles=NTL, d_ff=d_ff, final=final),
        grid=(B // bt, NTL),
        in_specs=[
            pl.BlockSpec((bt, TM, D), lambda b, i: (b, i, 0)),
            pl.BlockSpec((bt, GRID_W, D),
                         lambda b, i: (b, jnp.minimum((i + 1) * RW, T // GRID_W - 1), 0)),
            pl.BlockSpec((bt, 1, SUBLANES, D), lambda b, i: (b, jnp.minimum(i, 1), 0, 0)),
            const((1, D)), const((D, 2 * d_ff)),
            const((2 * SUBLANES, d_ff)), const((1, d_ff)), const((d_ff, D)), const((1, D)),
        ],
        out_specs=out_spec,
        out_shape=out_shape,
        scratch_shapes=[pltpu.VMEM((bt, GRID_W, d_ff), F32)],
        compiler_params=pltpu.CompilerParams(
            dimension_semantics=("parallel", "arbitrary"), vmem_limit_bytes=VMEM_LIMIT),
        name="ffn",
    )(x1, x1, modsel, g2, wup, cw, cbias, wdn, fg)


def kernel(x, c, ctx, c_ctx, ada_w, ada_b, norm1_g, norm2_g, w_in, conv_a_w, rw_w0, rw_w_up, rw_a0,
           rw_a_up, rw_k_k, rw_k_a, rw_r_k, rw_g_up, rw_ln_g, rw_ln_b, w_out, ffn_w_up, ffn_conv_w,
           ffn_conv_b, ffn_w_down, final_g):
    B, SEQ, D = x.shape
    CTX = ctx.shape[1]
    L = w_in.shape[0]
    d_ff = ffn_w_down.shape[1]
    assert CTX == TM and SEQ % TM == 0 and TM % GRID_W == 0 and CHUNK == HEAD and CPT <= SUBLANES
    assert w_in.shape[2] == 3 * D_CONV + 3 * D_RWKV + LORA_WA + G_LORA and d_ff % MXU_W == 0

    rows = -(-(B + 1) // SUBLANES) * SUBLANES
    c_rows = jnp.zeros((rows, D), F32).at[:B].set(c).at[B].set(c_ctx)
    mod = _ada(c_rows, ada_w, ada_b)

    hw = LORA_WA // 2
    zpad = jnp.zeros((L, hw, 2 * D_RWKV), F32)
    both = lambda t: jnp.concatenate([t[:, 0], t[:, 1]], axis=-1)
    wup_ext = jnp.concatenate([both(rw_w_up), zpad], axis=1).astype(BF16)
    aup_ext = jnp.concatenate([zpad, both(rw_a_up)], axis=1).astype(BF16)
    w0_2 = both(rw_w0[:, :, None, :])
    a0_2 = both(rw_a0[:, :, None, :])
    head_of = jnp.arange(D_RWKV) // HEAD
    ones_bd = (head_of[:, None] == head_of[None, :]).astype(BF16)
    tt = jnp.arange(TM)
    same = (tt[:, None] // CHUNK) == (tt[None, :] // CHUNK)
    tri = jnp.stack([same & (tt[None, :] <= tt[:, None]),
                     same & (tt[None, :] >= tt[:, None])]).astype(BF16)
    cw_a = jnp.zeros((L, SUBLANES, D_CONV), F32).at[:, :3].set(conv_a_w)
    cw_f = jnp.zeros((L, 2 * SUBLANES, d_ff), F32).at[:, :9].set(ffn_conv_w.reshape(L, 9, d_ff))

    za, zb = ctx, x
    for l in range(L):
        split = l == 0
        lat = mod[l, :B].reshape(B, 6, D)
        cm = jnp.broadcast_to(mod[l, B].reshape(1, 6, D), (B, 6, D))
        modsel = jnp.zeros((B, 2, SUBLANES, D), F32).at[:, 0, :6].set(cm).at[:, 1, :6].set(lat)

        cb, u, v, prod, gl, ops, cl = _proj(
            za, zb, split, modsel, norm1_g[l].reshape(1, D), w_in[l].astype(BF16), wup_ext[l], aup_ext[l],
            w0_2[l], a0_2[l], rw_k_k[l].reshape(1, -1), rw_k_a[l].reshape(1, -1),
            rw_r_k[l].reshape(1, -1), ones_bd, tri)
        yf, yb = _scan(v, ops, cl, CTX)
        x1 = _mix(
            za, zb, split, modsel, cb, u, yf, yb, prod, v, gl, cw_a[l], rw_ln_g[l].reshape(1, -1),
            rw_ln_b[l].reshape(1, -1), ones_bd, rw_g_up[l].astype(BF16), w_out[l].astype(BF16))
        za = zb = _ffn(x1, modsel, norm2_g[l].reshape(1, D), ffn_w_up[l].astype(BF16), cw_f[l],
                       ffn_conv_b[l].reshape(1, -1), ffn_w_down[l].astype(BF16), final_g.reshape(1, D),
                       final=(l == L - 1))
    return za
```

```python
import functools
import math

import jax
import jax.numpy as jnp
from jax import lax
from jax.experimental import pallas as pl
from jax.experimental.pallas import tpu as pltpu

F32 = jnp.float32
BF16 = jnp.bfloat16

HEAD = 64
D_CONV = 512
D_RWKV = 512
LORA_WA = 128
G_LORA = 128
GRID_W = 64
RMS_EPS = 1e-6
GN_EPS = 64e-5
DECAY_SCALE = math.exp(-0.5)

TM = 256
CHUNK = 64
CPT = TM // CHUNK
BT = 2
UP_AHEAD = 6
HPG = 2
GROUP = HPG * HEAD
HEAD_LOG2 = 6
BLK = 16
BLK_LOG2 = 4
assert 1 << HEAD_LOG2 == HEAD and 1 << BLK_LOG2 == BLK

SUBLANES = 8
MXU_W = 256
ADA_NB = 1536
VMEM_LIMIT = 56 * 1024 * 1024

NN = (((1,), (0,)), ((), ()))
NT = (((1,), (1,)), ((), ()))
TN = (((0,), (0,)), ((), ()))


def _silu(x):
    return x * jax.nn.sigmoid(x)


def _split(x):
    hi = x.astype(BF16)
    return hi, (x - hi.astype(F32)).astype(BF16)


def _dg(a, b, dims):
    return lax.dot_general(a, b, dims, preferred_element_type=F32)


def _mm3(a, b_hi, b_lo, dims):
    a_hi, a_lo = _split(a)
    return _dg(a_hi, b_hi, dims) + (_dg(a_hi, b_lo, dims) + _dg(a_lo, b_hi, dims))


def _rms(x):
    return x * lax.rsqrt(jnp.mean(x * x, axis=-1, keepdims=True) + RMS_EPS)


def _dot01(t, m01, left=False):
    hi, lo = _split(t)
    if left:
        return _dg(m01, hi, NN) + _dg(m01, lo, NN)
    return _dg(hi, m01, NN) + _dg(lo, m01, NN)


def _z_tile(za_ref, zb_ref, split, bi):
    if not split:
        return za_ref[bi]
    return jnp.where(pl.program_id(1) == 0, za_ref[bi], zb_ref[bi])


def _z_specs(split, bt, D):
    if split:
        return [pl.BlockSpec((bt, TM, D), lambda b, i: (b, 0, 0)),
                pl.BlockSpec((bt, TM, D), lambda b, i: (b, jnp.maximum(i - 1, 0), 0))]
    return [pl.BlockSpec((bt, TM, D), lambda b, i: (b, i, 0)),
            pl.BlockSpec((1, SUBLANES, D), lambda b, i: (0, 0, 0))]


def _rows_per_step(B):
    return BT if B % BT == 0 else 1


def _ada_kernel(c_ref, w_ref, b_ref, o_ref):
    s = _silu(c_ref[...])
    w_hi, w_lo = _split(w_ref[0])
    o_ref[0] = _mm3(s, w_hi, w_lo, NN) + b_ref[0]


def _ada(c_rows, ada_w, ada_b):
    L, D, N = ada_w.shape
    R = c_rows.shape[0]
    NB = ADA_NB
    return pl.pallas_call(
        _ada_kernel,
        grid=(L, N // NB),
        in_specs=[
            pl.BlockSpec((R, D), lambda l, n: (0, 0)),
            pl.BlockSpec((1, D, NB), lambda l, n: (l, 0, n)),
            pl.BlockSpec((1, 1, NB), lambda l, n: (l, 0, n)),
        ],
        out_specs=pl.BlockSpec((1, R, NB), lambda l, n: (l, 0, n)),
        out_shape=jax.ShapeDtypeStruct((L, R, N), F32),
        compiler_params=pltpu.CompilerParams(
            dimension_semantics=("arbitrary", "arbitrary"), vmem_limit_bytes=VMEM_LIMIT),
        name="ada",
    )(c_rows, ada_w, ada_b.reshape(L, 1, N))


def _proj_kernel(za_ref, zb_ref, m_ref, g_ref, win_ref, wup_ref, aup_ref, w0_ref, a0_ref, kk_ref, ka_ref,
                 rk_ref, ones_ref, tri_ref,
                 cb_ref, u_ref, v_ref, prod_ref, gl_ref, ops_ref, cl_ref,
                 *, split):
    base = 3 * D_CONV

    def head(bi):
        m = m_ref[bi, 0]
        h = _rms(_z_tile(za_ref, zb_ref, split, bi)) * g_ref[...]
        h = (h * (1.0 + m[1:2]) + m[0:1]).astype(BF16)

        def proj(lo, width):
            return jnp.dot(h, win_ref[:, lo:lo + width], preferred_element_type=F32)

        return (proj, proj(base + 3 * D_RWKV, LORA_WA + G_LORA), proj(base + D_RWKV, D_RWKV),
                proj(base, D_RWKV), proj(base + 2 * D_RWKV, D_RWKV))

    heads = [head(bi) for bi in range(cb_ref.shape[0])]
    for bi, (proj, lora, k, r, v) in enumerate(heads):
        wa = lora[:, :LORA_WA]
        gl = lora[:, LORA_WA:]
        ones_bd = ones_ref[...]

        kraw = k * kk_ref[...]
        kkn = kraw * lax.rsqrt(jnp.maximum(_dot01(kraw * kraw, ones_bd), 1e-24))
        lw2 = -DECAY_SCALE * jax.nn.sigmoid(
            w0_ref[...] + jnp.dot(jnp.tanh(wa).astype(BF16), wup_ref[...], preferred_element_type=F32))
        a2 = jax.nn.sigmoid(
            a0_ref[...] + jnp.dot(wa.astype(BF16), aup_ref[...], preferred_element_type=F32))

        cb_ref[bi] = proj(0, D_CONV)
        v_ref[bi] = v.astype(BF16)
        prod_ref[bi] = r * k * rk_ref[...]
        gl_ref[bi] = gl

        zero_rows = jnp.zeros((SUBLANES - CPT, D_RWKV), F32)
        conv_in = []
        for d in range(2):
            conv_in.append(proj((1 + d) * D_CONV, D_CONV))
            lw = lw2[:, d * D_RWKV:(d + 1) * D_RWKV]
            a = a2[:, d * D_RWKV:(d + 1) * D_RWKV]
            c = _dot01(lw, tri_ref[d], left=True)
            e_neg = jnp.exp(-c)
            ops = (kkn * jnp.exp(c - lw), r * jnp.exp(c), kkn * a * e_neg,
                   k * (1.0 + (a - 1.0) * ka_ref[...]) * e_neg)
            for j, x in enumerate(ops):
                ops_ref[d, bi, :, j * D_RWKV:(j + 1) * D_RWKV] = x.astype(BF16)
            last = CHUNK - 1 if d == 0 else 0
            cl_ref[d, bi, 0] = jnp.concatenate(
                [c[j * CHUNK + last:j * CHUNK + last + 1] for j in range(CPT)] + [zero_rows], axis=0)
        u_ref[bi] = conv_in[0] * conv_in[1]


def _proj(za, zb, split, modsel, g1, win, wup, aup, w0, a0, k_k, k_a, r_k, ones_bd, tri):
    B, D = za.shape[0], za.shape[2]
    T = za.shape[1] + zb.shape[1] if split else za.shape[1]
    NTL = T // TM
    P = win.shape[1]
    W = D_RWKV
    bt = _rows_per_step(B)
    tok = pl.BlockSpec((bt, TM, W), lambda b, i: (b, i, 0))
    tok2 = pl.BlockSpec((2, bt, TM, 4 * W), lambda b, i: (0, b, i, 0))
    const = lambda shape: pl.BlockSpec(shape, lambda b, i: (0,) * len(shape))
    f32_tok = jax.ShapeDtypeStruct((B, T, W), F32)
    bf_tok2 = jax.ShapeDtypeStruct((2, B, T, 4 * W), BF16)
    return pl.pallas_call(
        functools.partial(_proj_kernel, split=split),
        grid=(B // bt, NTL),
        in_specs=_z_specs(split, bt, D) + [
            pl.BlockSpec((bt, 1, SUBLANES, D), lambda b, i: (b, jnp.minimum(i, 1), 0, 0)),
            const((1, D)), const((D, P)), const((LORA_WA, 2 * W)), const((LORA_WA, 2 * W)),
            const((1, 2 * W)), const((1, 2 * W)), const((1, W)), const((1, W)),
            const((1, W)), const((W, W)), const((2, TM, TM)),
        ],
        out_specs=[tok, tok, tok, tok, pl.BlockSpec((bt, TM, G_LORA), lambda b, i: (b, i, 0)),
                   tok2,
                   pl.BlockSpec((2, bt, 1, SUBLANES, W), lambda b, i: (0, b, i, 0, 0))],
        out_shape=[f32_tok, f32_tok, jax.ShapeDtypeStruct((B, T, W), BF16), f32_tok,
                   jax.ShapeDtypeStruct((B, T, G_LORA), F32),
                   bf_tok2,
                   jax.ShapeDtypeStruct((2, B, NTL, SUBLANES, W), F32)],
        compiler_params=pltpu.CompilerParams(
            dimension_semantics=("parallel", "parallel"), vmem_limit_bytes=VMEM_LIMIT,
            allow_input_fusion=[j in (4, 5, 6) for j in range(14)]),
        name="proj",
    )(za, zb, modsel, g1, win, wup, aup, w0, a0, k_k, k_a, r_k, ones_bd, tri)


def _mm(a, x, dims):
    return _dg(a.astype(BF16), x.astype(BF16), dims)


def _scan_chunks(chains, bdmask):
    C = CHUNK
    row = lax.broadcasted_iota(jnp.int32, (C, GROUP), 0)
    col = lax.broadcasted_iota(jnp.int32, (C, GROUP), 1) & (HEAD - 1)
    eye = (row == col).astype(F32)
    blk16 = (row >> BLK_LOG2) == (col >> BLK_LOG2)
    lane_head = lax.broadcasted_iota(jnp.int32, (HEAD, GROUP), 1) >> HEAD_LOG2
    zero_bf = jnp.zeros((), BF16)

    def each(f, *lists):
        return [f(*a) for a in zip(*lists)]

    def bd(x):
        return jnp.where(bdmask, jnp.concatenate([x.astype(BF16)] * HPG, axis=0), zero_bf)

    def bd2(x, y, axis):
        return jnp.concatenate([bd(x), bd(y)], axis=axis)

    def rows(*xs):
        return jnp.concatenate([x.astype(BF16) for x in xs], axis=0)

    def diag_blocks(full):
        out = jnp.where(lane_head == 0, full[0:HEAD], 0.0)
        for hh in range(1, HPG):
            out = out + jnp.where(lane_head == hh, full[hh * HEAD:(hh + 1) * HEAD], 0.0)
        return out

    rev = [ch["rev"] for ch in chains]
    at, rt, bt, kt, v, p_c, s0 = ([ch[n] for ch in chains]
                                  for n in ("at", "rt", "bt", "kt", "v", "p_c", "s0"))
    strict = [(col > row) if rv else (col < row) for rv in rev]
    incl = [(col >= row) if rv else (col <= row) for rv in rev]
    bh = each(lambda x, p: x * p, bt, p_c)
    kh = each(lambda x, p: x * p, kt, p_c)

    abk = each(lambda a_, r_, x, y: _mm(rows(a_, r_), bd2(x, y, 0), NT), at, rt, bt, kt)
    lmat = each(lambda m_, x: jnp.where(m_, x[:C, :GROUP], 0.0), strict, abk)
    aak = each(lambda m_, x: jnp.where(m_, x[:C, GROUP:], 0.0), strict, abk)
    arb = each(lambda m_, x: jnp.where(m_, x[C:, :GROUP], 0.0), incl, abk)
    ark = each(lambda m_, x: jnp.where(m_, x[C:, GROUP:], 0.0), incl, abk)

    S = BLK
    lane16 = (lax.broadcasted_iota(jnp.int32, (S, GROUP), 1) & (HEAD - 1)) >> BLK_LOG2
    eye_p = (lax.broadcasted_iota(jnp.int32, (S, GROUP), 0)
             == (lax.broadcasted_iota(jnp.int32, (S, GROUP), 1) & (S - 1))).astype(F32)
    mask16 = (lax.broadcasted_iota(jnp.int32, (GROUP, GROUP), 0) >> BLK_LOG2) == \
             (lax.broadcasted_iota(jnp.int32, (GROUP, GROUP), 1) >> BLK_LOG2)

    def pack16(full):
        out = jnp.where(lane16 == 0, full[0:S], 0.0)
        for j in range(1, HEAD // S):
            out = out + jnp.where(lane16 == j, full[j * S:(j + 1) * S], 0.0)
        return out

    def unpack16(p):
        return jnp.concatenate([jnp.where(lane16 == j, p, 0.0) for j in range(HEAD // S)], axis=0)

    def bd16(p):
        return jnp.where(mask16, jnp.concatenate([p.astype(BF16)] * (GROUP // S), axis=0), zero_bf)

    ld = each(pack16, lmat)
    lo = each(lambda x: jnp.where(blk16, 0.0, x), lmat)
    p1 = each(lambda x: eye_p - x, ld)
    l2 = each(lambda x: _mm(x, bd16(x), NN), ld)
    t = each(lambda p, x: _mm(rows(p, x), bd16(x), NN), p1, l2)
    p2 = each(lambda p, x: p + x[:S], p1, t)
    l4 = each(lambda x: x[S:], t)
    t = each(lambda p, x: _mm(rows(p, x), bd16(x), NN), p2, l4)
    p3 = each(lambda p, x: p + x[:S], p2, t)
    l8 = each(lambda x: x[S:], t)
    dinv = each(lambda p, x: unpack16(p + _mm(p, bd16(x), NN)), p3, l8)
    mm = each(lambda d_, x: _mm(d_, bd(x), NN), dinv, lo)
    mm2 = each(lambda x: _mm(x, bd(x), NN), mm)
    g = each(lambda x, x2: eye - x + x2 - _mm(x, bd(x2), NN), mm, mm2)
    tinv = each(lambda g_, d_: _mm(g_, bd(d_), NN), g, dinv)

    t = each(lambda x, y, v_: _mm(rows(x, y), bd(v_), NN), aak, ark, v)
    akv = each(lambda x: x[:C], t)
    arkv = each(lambda x: x[C:], t)
    t = each(lambda t_, x, y: _mm(t_, bd2(x, y, 1), NN), tinv, at, akv)
    wm = each(lambda x: x[:, :GROUP], t)
    um = each(lambda x: x[:, GROUP:], t)
    t = each(lambda a_, x, y: _mm(a_, bd2(x, y, 1), NN), arb, wm, um)
    qh = each(lambda r_, x: r_.astype(F32) - x[:, :GROUP], rt, t)
    yl = each(lambda y_, x: y_ - x[:, GROUP:], arkv, t)
    t = each(lambda q_, w_, s_: _mm(rows(q_, w_), bd(s_), NT), qh, wm, s0)
    y = each(lambda x, y_: x[:C] + y_, t, yl)
    ds = each(lambda v_, u_, x, k_, b_: diag_blocks(_mm(rows(v_, u_, x[C:]), rows(k_, -b_, -b_), TN)),
              v, um, t, kh, bh)
    s_new = each(lambda s_, p_, d_: s_ * p_ + d_, s0, p_c, ds)
    return list(zip(y, s_new))


def _scan_kernel(vf_ref, vb_ref, opf_ref, clf_ref, opb_ref, clb_ref, yf_ref, yb_ref, s_ref,
                 *, nctx, n_chunks):
    s = pl.program_id(1)

    @pl.when(s == 0)
    def _():
        s_ref[...] = jnp.zeros_like(s_ref)

    back = jnp.where(s < nctx, nctx - 1 - s, nctx + n_chunks - 1 - s)
    bdmask = (lax.broadcasted_iota(jnp.int32, (GROUP, GROUP), 0) >> HEAD_LOG2) == \
             (lax.broadcasted_iota(jnp.int32, (GROUP, GROUP), 1) >> HEAD_LOG2)
    dirs = ((False, s, vf_ref, opf_ref, clf_ref, yf_ref), (True, back, vb_ref, opb_ref, clb_ref, yb_ref))
    chains, sinks = [], []
    for bi in range(s_ref.shape[0]):
        for d, (rev, chunk, v_ref, op_ref, cl_ref, y_ref) in enumerate(dirs):
            p_c = jnp.exp(cl_ref[0, bi, 0, pl.ds(chunk % CPT, 1), :])
            for q in range(D_RWKV // GROUP):
                sl = slice(q * GROUP, (q + 1) * GROUP)
                at, rt, bt, kt = (op_ref[0, bi, :, j * D_RWKV + q * GROUP:j * D_RWKV + (q + 1) * GROUP]
                                  for j in range(4))
                chains.append(dict(rev=rev, at=at, rt=rt, bt=bt, kt=kt,
                                   v=v_ref[bi, :, sl], p_c=p_c[:, sl], s0=s_ref[bi, d, :, sl]))
                sinks.append((y_ref, bi, d, sl))
    for (y, s_new), (y_ref, bi, d, sl) in zip(_scan_chunks(chains, bdmask), sinks):
        y_ref[bi, :, sl] = y
        s_ref[bi, d, :, sl] = s_new


def _scan(v, ops, cl, ctx_len):
    B, T, W = v.shape
    C = CHUNK
    NC = T // C
    nctx = ctx_len // C

    def back(s):
        return jnp.where(s < nctx, nctx - 1 - s, nctx + NC - 1 - s)

    BB = 4 if B % 4 == 0 else 2 if B % 2 == 0 else 1
    fwd = pl.BlockSpec((BB, C, W), lambda bb, s: (bb, s, 0))
    bwd = pl.BlockSpec((BB, C, W), lambda bb, s: (bb, back(s), 0))
    fwd_d = pl.BlockSpec((1, BB, C, 4 * W), lambda bb, s: (0, bb, s, 0))
    bwd_d = pl.BlockSpec((1, BB, C, 4 * W), lambda bb, s: (1, bb, back(s), 0))
    fwd_c = pl.BlockSpec((1, BB, 1, SUBLANES, W), lambda bb, s: (0, bb, s // CPT, 0, 0))
    bwd_c = pl.BlockSpec((1, BB, 1, SUBLANES, W), lambda bb, s: (1, bb, back(s) // CPT, 0, 0))
    out = jax.ShapeDtypeStruct((B, T, W), F32)
    return pl.pallas_call(
        functools.partial(_scan_kernel, nctx=nctx, n_chunks=NC),
        grid=(B // BB, NC),
        in_specs=[fwd, bwd, fwd_d, fwd_c, bwd_d, bwd_c],
        out_specs=[fwd, bwd],
        out_shape=[out, out],
        scratch_shapes=[pltpu.VMEM((BB, 2, HEAD, W), F32)],
        compiler_params=pltpu.CompilerParams(
            dimension_semantics=("parallel", "arbitrary"), vmem_limit_bytes=VMEM_LIMIT),
        name="scan",
    )(v, v, ops, cl, ops, cl)


def _mix_kernel(za_ref, zb_ref, m_ref, cb_ref, u_ref, up_ref, un_ref, yf_ref, yb_ref, prod_ref, v_ref,
                gl_ref, cw_ref, lng_ref, lnb_ref, ones_ref, gup_ref, wout_ref, x1_ref, *, n_tiles, split):
    i = pl.program_id(1)
    for bi in range(x1_ref.shape[0]):
        m = m_ref[bi, 0]
        prev_ok = (i >= 2).astype(F32)
        next_ok = jnp.logical_and(i >= 1, i <= n_tiles - 2).astype(F32)
        u = u_ref[bi]
        rowi = lax.broadcasted_iota(jnp.int32, (TM, 1), 0)
        um1 = jnp.where(rowi == 0, up_ref[bi][SUBLANES - 1:SUBLANES] * prev_ok, pltpu.roll(u, 1, 0))
        up1 = jnp.where(rowi == TM - 1, un_ref[bi][0:1] * next_ok, pltpu.roll(u, TM - 1, 0))
        cw = cw_ref[...]
        yconv = cb_ref[bi] * (cw[0:1] * um1 + cw[1:2] * u + cw[2:3] * up1)

        ones_bd = ones_ref[...]
        y = yf_ref[bi] + yb_ref[bi]
        mu = _dot01(y, ones_bd) * (1.0 / HEAD)
        dlt = y - mu
        var = _dot01(dlt * dlt, ones_bd) * (1.0 / HEAD)
        yn = dlt * lax.rsqrt(var + GN_EPS) * lng_ref[...] + lnb_ref[...]
        bonus = _dot01(prod_ref[bi], ones_bd) * v_ref[bi].astype(F32)
        gate = jnp.dot(jax.nn.sigmoid(gl_ref[bi]).astype(BF16), gup_ref[...], preferred_element_type=F32)
        yrw = (yn + bonus) * gate

        out = (jnp.dot(yconv.astype(BF16), wout_ref[0:D_CONV, :], preferred_element_type=F32)
               + jnp.dot(yrw.astype(BF16), wout_ref[D_CONV:D_CONV + D_RWKV, :], preferred_element_type=F32))
        x1_ref[bi] = _z_tile(za_ref, zb_ref, split, bi) + m[2:3] * out


def _mix(za, zb, split, modsel, cb, u, yf, yb, prod, v, gl, cw, lng, lnb, ones_bd, gup, wout):
    B, T, D = cb.shape[0], cb.shape[1], za.shape[2]
    NTL = T // TM
    bt = 2 * BT if B % (2 * BT) == 0 else _rows_per_step(B)
    tok = pl.BlockSpec((bt, TM, D_RWKV), lambda b, i: (b, i, 0))
    const = lambda shape: pl.BlockSpec(shape, lambda b, i: (0,) * len(shape))
    R8 = TM // SUBLANES
    return pl.pallas_call(
        functools.partial(_mix_kernel, n_tiles=NTL, split=split),
        grid=(B // bt, NTL),
        in_specs=_z_specs(split, bt, D) + [
            pl.BlockSpec((bt, 1, SUBLANES, D), lambda b, i: (b, jnp.minimum(i, 1), 0, 0)),
            tok, tok,
            pl.BlockSpec((bt, SUBLANES, D_CONV), lambda b, i: (b, jnp.maximum(i * R8 - 1, 0), 0)),
            pl.BlockSpec((bt, SUBLANES, D_CONV),
                         lambda b, i: (b, jnp.minimum((i + 1) * R8, T // SUBLANES - 1), 0)),
            tok, tok, tok, tok, pl.BlockSpec((bt, TM, G_LORA), lambda b, i: (b, i, 0)),
            const((SUBLANES, D_CONV)), const((1, D_RWKV)), const((1, D_RWKV)), const((D_RWKV, D_RWKV)),
            const((G_LORA, D_RWKV)), const((D_CONV + D_RWKV, D)),
        ],
        out_specs=pl.BlockSpec((bt, TM, D), lambda b, i: (b, i, 0)),
        out_shape=jax.ShapeDtypeStruct((B, T, D), F32),
        compiler_params=pltpu.CompilerParams(
            dimension_semantics=("parallel", "parallel"), vmem_limit_bytes=VMEM_LIMIT,
            allow_input_fusion=[j in (16, 17) for j in range(18)]),
        name="mix",
    )(za, zb, modsel, cb, u, u, u, yf, yb, prod, v, gl, cw, lng, lnb, ones_bd, gup, wout)


def _ffn_kernel(x1_ref, xb_ref, m_ref, g2_ref, wup_ref, cw_ref, cbias_ref, wdn_ref, fg_ref,
                o_ref, gtop_ref, *, n_tiles, d_ff, final):
    i = pl.program_id(1)
    lat = (i >= 1).astype(F32)
    top_ok = i >= 2
    bot_ok = jnp.logical_and(i >= 1, i <= n_tiles - 2).astype(F32)
    rowi = lax.broadcasted_iota(jnp.int32, (TM, 1), 0)
    colp = jnp.where(i == 0, rowi, rowi & (GRID_W - 1))
    width = jnp.where(i == 0, TM, GRID_W)
    lmask = (colp > 0).astype(F32)
    rmask = (colp < width - 1).astype(F32)
    NB = MXU_W

    @pl.when(i == 0)
    def _():
        gtop_ref[...] = jnp.zeros_like(gtop_ref)

    for bi in range(x1_ref.shape[0]):
        m = m_ref[bi, 0]

        def norm2(x, keep):
            h = _rms(x) * g2_ref[...]
            return ((h * (1.0 + m[4:5]) + m[3:4]) * keep).astype(BF16)

        x1 = x1_ref[bi]
        hx = jnp.concatenate([norm2(x1, 1.0), norm2(xb_ref[bi], bot_ok)], axis=0)
        hc = hx[0:TM]

        def up(n):
            return (jnp.dot(hx, wup_ref[:, n:n + NB], preferred_element_type=F32),
                    jnp.dot(hc, wup_ref[:, d_ff + n:d_ff + n + NB], preferred_element_type=F32))

        acc = jnp.zeros((TM, o_ref.shape[-1]), F32)
        ahead = [up(j * NB) for j in range(UP_AHEAD)]
        prev = None
        for n in range(0, d_ff, NB):
            gfull, val = ahead.pop(0)
            if n + UP_AHEAD * NB < d_ff:
                ahead.append(up(n + UP_AHEAD * NB))
            if prev is not None:
                acc = acc + jnp.dot(prev, wdn_ref[n - NB:n, :], preferred_element_type=F32)
            g, gd = gfull[0:TM], gfull[GRID_W:GRID_W + TM]
            top = jnp.where(top_ok, gtop_ref[bi, :, n:n + NB], 0.0)
            gu = jnp.concatenate([top, g[0:TM - GRID_W]], axis=0)
            gtop_ref[bi, :, n:n + NB] = g[TM - GRID_W:TM]
            w = cw_ref[:, n:n + NB]
            wv = w * lat
            h0 = wv[0:1] * gu + w[3:4] * g + wv[6:7] * gd
            h1 = wv[1:2] * gu + w[4:5] * g + wv[7:8] * gd
            h2 = wv[2:3] * gu + w[5:6] * g + wv[8:9] * gd
            conv = (lmask * pltpu.roll(h0, 1, 0) + h1 + rmask * pltpu.roll(h2, TM - 1, 0)
                    + cbias_ref[:, n:n + NB])
            prev = (_silu(conv) * val).astype(BF16)
        acc = acc + jnp.dot(prev, wdn_ref[d_ff - NB:d_ff, :], preferred_element_type=F32)
        x2 = x1 + m[5:6] * acc
        if final:
            x2 = _rms(x2) * fg_ref[...]
        o_ref[bi] = x2


def _ffn(x1, modsel, g2, wup, cw, cbias, wdn, fg, final):
    B, T, D = x1.shape
    NTL = T // TM
    d_ff = wdn.shape[0]
    RW = TM // GRID_W
    bt = 1
    const = lambda shape: pl.BlockSpec(shape, lambda b, i: (0,) * len(shape))
    if final:
        out_spec = pl.BlockSpec((bt, TM, D), lambda b, i: (b, jnp.maximum(i - 1, 0), 0))
        out_shape = jax.ShapeDtypeStruct((B, T - TM, D), F32)
    else:
        out_spec = pl.BlockSpec((bt, TM, D), lambda b, i: (b, i, 0))
        out_shape = jax.ShapeDtypeStruct((B, T, D), F32)
    return pl.pallas_call(
        functools.partial(_ffn_kernel, n_tiles=NTL, d_ff=d_ff, final=final),
        grid=(B // bt, NTL),
        in_specs=[
            pl.BlockSpec((bt, TM, D), lambda b, i: (b, i, 0)),
            pl.BlockSpec((bt, GRID_W, D),
                         lambda b, i: (b, jnp.minimum((i + 1) * RW, T // GRID_W - 1), 0)),
            pl.BlockSpec((bt, 1, SUBLANES, D), lambda b, i: (b, jnp.minimum(i, 1), 0, 0)),
            const((1, D)), const((D, 2 * d_ff)),
            const((2 * SUBLANES, d_ff)), const((1, d_ff)), const((d_ff, D)), const((1, D)),
        ],
        out_specs=out_spec,
        out_shape=out_shape,
        scratch_shapes=[pltpu.VMEM((bt, GRID_W, d_ff), F32)],
        compiler_params=pltpu.CompilerParams(
            dimension_semantics=("parallel", "arbitrary"), vmem_limit_bytes=VMEM_LIMIT,
            allow_input_fusion=[j in (4, 7) for j in range(9)]),
        name="ffn",
    )(x1, x1, modsel, g2, wup, cw, cbias, wdn, fg)


def kernel(x, c, ctx, c_ctx, ada_w, ada_b, norm1_g, norm2_g, w_in, conv_a_w, rw_w0, rw_w_up, rw_a0,
           rw_a_up, rw_k_k, rw_k_a, rw_r_k, rw_g_up, rw_ln_g, rw_ln_b, w_out, ffn_w_up, ffn_conv_w,
           ffn_conv_b, ffn_w_down, final_g):
    B, SEQ, D = x.shape
    CTX = ctx.shape[1]
    L = w_in.shape[0]
    d_ff = ffn_w_down.shape[1]
    assert CTX == TM and SEQ % TM == 0 and TM % GRID_W == 0 and CHUNK == HEAD and CPT <= SUBLANES
    assert w_in.shape[2] == 3 * D_CONV + 3 * D_RWKV + LORA_WA + G_LORA and d_ff % MXU_W == 0

    rows = -(-(B + 1) // SUBLANES) * SUBLANES
    c_rows = jnp.zeros((rows, D), F32).at[:B].set(c).at[B].set(c_ctx)
    mod = _ada(c_rows, ada_w, ada_b)

    hw = LORA_WA // 2
    zpad = jnp.zeros((L, hw, 2 * D_RWKV), F32)
    both = lambda t: jnp.concatenate([t[:, 0], t[:, 1]], axis=-1)
    wup_ext = jnp.concatenate([both(rw_w_up), zpad], axis=1).astype(BF16)
    aup_ext = jnp.concatenate([zpad, both(rw_a_up)], axis=1).astype(BF16)
    w0_2 = both(rw_w0[:, :, None, :])
    a0_2 = both(rw_a0[:, :, None, :])
    head_of = jnp.arange(D_RWKV) // HEAD
    ones_bd = (head_of[:, None] == head_of[None, :]).astype(BF16)
    tt = jnp.arange(TM)
    same = (tt[:, None] // CHUNK) == (tt[None, :] // CHUNK)
    tri = jnp.stack([same & (tt[None, :] <= tt[:, None]),
                     same & (tt[None, :] >= tt[:, None])]).astype(BF16)
    cw_a = jnp.zeros((L, SUBLANES, D_CONV), F32).at[:, :3].set(conv_a_w)
    cw_f = jnp.zeros((L, 2 * SUBLANES, d_ff), F32).at[:, :9].set(ffn_conv_w.reshape(L, 9, d_ff))

    za, zb = ctx, x
    for l in range(L):
        split = l == 0
        lat = mod[l, :B].reshape(B, 6, D)
        cm = jnp.broadcast_to(mod[l, B].reshape(1, 6, D), (B, 6, D))
        modsel = jnp.zeros((B, 2, SUBLANES, D), F32).at[:, 0, :6].set(cm).at[:, 1, :6].set(lat)

        cb, u, v, prod, gl, ops, cl = _proj(
            za, zb, split, modsel, norm1_g[l].reshape(1, D), w_in[l].astype(BF16), wup_ext[l], aup_ext[l],
            w0_2[l], a0_2[l], rw_k_k[l].reshape(1, -1), rw_k_a[l].reshape(1, -1),
            rw_r_k[l].reshape(1, -1), ones_bd, tri)
        yf, yb = _scan(v, ops, cl, CTX)
        x1 = _mix(
            za, zb, split, modsel, cb, u, yf, yb, prod, v, gl, cw_a[l], rw_ln_g[l].reshape(1, -1),
            rw_ln_b[l].reshape(1, -1), ones_bd, rw_g_up[l].astype(BF16), w_out[l].astype(BF16))
        za = zb = _ffn(x1, modsel, norm2_g[l].reshape(1, D), ffn_w_up[l].astype(BF16), cw_f[l],
                       ffn_conv_b[l].reshape(1, -1), ffn_w_down[l].astype(BF16), final_g.reshape(1, D),
                       final=(l == L - 1))
    return za
```
